```python
import math
import jax, jax.numpy as jnp
from jax import lax
import numpy as np

D_MODEL = 1024
BATCH = 32
SEQ = 256
DEPTH = 4
DEC_BATCH = 4
DEC_SEQ = 2048
PAST_LEN = 512

GRID_W = 64
F_GROUPS = 4
F_GROUP_W = 128
F_W = F_GROUPS * F_GROUP_W
CONV_W = 512
CONV_K = 31
N_HEADS = 8
QK_NOPE = 64
QK_ROPE = 32
V_DIM = 64
QK_DIM = QK_NOPE + QK_ROPE
Q_LORA = 384
KV_LORA = 256
ROPE_BASE = 10000.0
Q_BLOCK = 128
N_BRANCH = 3
IN_COLS = F_W + 2 * CONV_W + Q_LORA + KV_LORA + QK_ROPE
D_FF = 2816
FFN_K = 3
EPS = 1e-6

kernel_name = 'hybrid_fourier_conformer_mla_prefix_dit'


def rms_norm(x, g):
    xf = x.astype(jnp.float32)
    y = xf * lax.rsqrt(jnp.mean(xf * xf, axis=-1, keepdims=True) + EPS)
    return y.astype(x.dtype) * g


def layer_norm(x, g, b):
    xf = x.astype(jnp.float32)
    mu = jnp.mean(xf, axis=-1, keepdims=True)
    var = jnp.mean(jnp.square(xf - mu), axis=-1, keepdims=True)
    y = (xf - mu) * lax.rsqrt(var + EPS)
    return y.astype(x.dtype) * g + b


def adaln(cvec, ada_w, ada_b):
    m = jax.nn.silu(cvec) @ ada_w + ada_b
    return jnp.split(m[:, None, :], 6, axis=-1)


def grid_rope_tables(L):
    rows = L // GRID_W
    row = jnp.repeat(jnp.arange(rows), GRID_W, total_repeat_length=rows * GRID_W).astype(jnp.float32)
    col = jnp.tile(jnp.arange(GRID_W), rows).astype(jnp.float32)
    half = QK_ROPE // 2
    inv = ROPE_BASE ** (-jnp.arange(0, half, 2, dtype=jnp.float32) / half)
    ang = jnp.stack([row[:, None] * inv, col[:, None] * inv], axis=1)
    return jnp.cos(ang), jnp.sin(ang)


def apply_rope2d(x, cos, sin):
    shp = x.shape
    xr = x.astype(jnp.float32).reshape(shp[:-1] + (2, 2, QK_ROPE // 4))
    x1 = xr[..., 0, :]
    x2 = xr[..., 1, :]
    c = cos[None, :, None]
    s = sin[None, :, None]
    out = jnp.stack([x1 * c - x2 * s, x1 * s + x2 * c], axis=-2)
    return out.reshape(shp).astype(x.dtype)


def mla_queries(q_lat, q_norm_g, w_q_up, qk_q_g, rope):
    B, L, _ = q_lat.shape
    q = (rms_norm(q_lat, q_norm_g) @ w_q_up).reshape(B, L, N_HEADS, QK_DIM)
    q = rms_norm(q, qk_q_g)
    if rope is not None:
        q = jnp.concatenate([q[..., :QK_NOPE], apply_rope2d(q[..., QK_NOPE:], rope[0], rope[1])], axis=-1)
    return q


def mla_keys_values(c_kv, k_rope, w_kv_up, qk_k_g, rope):
    B, L, _ = c_kv.shape
    kv = (c_kv @ w_kv_up).reshape(B, L, N_HEADS, QK_NOPE + V_DIM)
    k_nope = kv[..., :QK_NOPE]
    v = kv[..., QK_NOPE:]
    k_r = jnp.broadcast_to(k_rope[:, :, None, :], (B, L, N_HEADS, QK_ROPE))
    k = rms_norm(jnp.concatenate([k_nope, k_r], axis=-1), qk_k_g)
    if rope is not None:
        k = jnp.concatenate([k[..., :QK_NOPE], apply_rope2d(k[..., QK_NOPE:], rope[0], rope[1])], axis=-1)
    return k, v


def block_attention(q, k, v):
    B, Lq, H, D = q.shape
    nb = Lq // Q_BLOCK
    qb = q.reshape(B, nb, Q_BLOCK, H, D).transpose(1, 0, 2, 3, 4)
    scale = 1.0 / math.sqrt(QK_DIM)

    def one(qblk):
        s = jnp.einsum('bqhd,bkhd->bhqk', qblk, k, preferred_element_type=jnp.float32) * scale
        p = jax.nn.softmax(s, axis=-1).astype(v.dtype)
        return jnp.einsum('bhqk,bkhe->bqhe', p, v)

    o = lax.map(one, qb)
    return o.transpose(1, 0, 2, 3, 4).reshape(B, Lq, H * V_DIM)


def fourier_mix(f_in):
    B, L, _ = f_in.shape
    z = f_in.astype(jnp.float32).reshape(B, L, F_GROUPS, F_GROUP_W)
    y = jnp.fft.fft2(z, axes=(1, 3), norm='ortho').real
    return y.reshape(B, L, F_W).astype(f_in.dtype)


def depthwise_conv(x, w, b):
    C = x.shape[-1]
    pad = w.shape[0] // 2
    y = lax.conv_general_dilated(x, w[:, None, :].astype(x.dtype), (1,), [(pad, pad)],
                                 dimension_numbers=('NWC', 'WIO', 'NWC'), feature_group_count=C)
    return y + b


def conv_module(conv_in, dw, dw_b, ln_g, ln_b, w_out):
    a, g = jnp.split(conv_in, 2, axis=-1)
    u = a * jax.nn.sigmoid(g)
    u = depthwise_conv(u, dw, dw_b)
    u = jax.nn.silu(layer_norm(u, ln_g, ln_b))
    return u @ w_out


def conv_ffn(h, up, dw, dw_b, down):
    u = depthwise_conv(h @ up, dw, dw_b)
    a, b = jnp.split(u, 2, axis=-1)
    return (jax.nn.silu(a) * b) @ down


def trunk_layer(x, cvec, W, l, rope, ctx_cache):
    sh1, sc1, g1, sh2, sc2, g2 = adaln(cvec, W['ada_w'][l], W['ada_b'][l])
    h = rms_norm(x, W['norm1_g'][l]) * (1 + sc1) + sh1
    u = h @ W['w_in'][l]
    cuts = [F_W, F_W + 2 * CONV_W, F_W + 2 * CONV_W + Q_LORA, F_W + 2 * CONV_W + Q_LORA + KV_LORA]
    f_in, conv_in, q_lat, kv_lat, k_rope = jnp.split(u, cuts, axis=-1)
    c_kv = rms_norm(kv_lat, W['kv_norm_g'][l])
    y_f = fourier_mix(f_in) @ W['w_fourier'][l]
    y_c = conv_module(conv_in, W['conv_dw'][l], W['conv_dw_b'][l], W['conv_ln_g'][l],
                      W['conv_ln_b'][l], W['w_conv_out'][l])
    q = mla_queries(q_lat, W['q_norm_g'][l], W['w_q_up'][l], W['qk_q_g'][l], rope)
    k, v = mla_keys_values(c_kv, k_rope, W['w_kv_up'][l], W['qk_k_g'][l], rope)
    if ctx_cache is not None:
        kc, vc = mla_keys_values(ctx_cache[0], ctx_cache[1], W['w_kv_up'][l], W['qk_k_g'][l], None)
        k = jnp.concatenate([kc, k], axis=1)
        v = jnp.concatenate([vc, v], axis=1)
    y_a = block_attention(q, k, v) @ W['w_mla_out'][l]
    gates = jax.nn.sigmoid(h @ W['w_gate'][l] + W['b_gate'][l])
    gf, gc, ga = jnp.split(gates, N_BRANCH, axis=-1)
    x = x + g1 * ((gf * y_f + gc * y_c + ga * y_a) @ W['w_out'][l])
    h2 = rms_norm(x, W['norm2_g'][l]) * (1 + sc2) + sh2
    x = x + g2 * conv_ffn(h2, W['ffn_up'][l], W['ffn_dw'][l], W['ffn_dw_b'][l], W['ffn_down'][l])
    return x, c_kv, k_rope


def setup_inputs(seed: int = 0) -> dict:
    key = jax.random.key(seed)
    ks = jax.random.split(key, 40)
    f32 = jnp.float32

    def nrm(k, shape, scale):
        return jax.random.normal(k, shape, f32) * scale

    def gain(k, shape):
        return 1.0 + 0.02 * jax.random.normal(k, shape, f32)

    L = DEPTH
    return {
        'x_prompt': nrm(ks[0], (BATCH, SEQ, D_MODEL), 1.0),
        'x_sample': nrm(ks[1], (DEC_BATCH, DEC_SEQ, D_MODEL), 1.0),
        'cache_ckv': nrm(ks[2], (DEC_BATCH, DEPTH, PAST_LEN, KV_LORA), 1.0),
        'cache_krope': nrm(ks[3], (DEC_BATCH, DEPTH, PAST_LEN, QK_ROPE), 1.0),
        'c': nrm(ks[4], (DEC_BATCH, D_MODEL), 1.0),
        'c_ctx': nrm(ks[5], (D_MODEL,), 1.0),
        'ada_w': nrm(ks[6], (L, D_MODEL, 6 * D_MODEL), D_MODEL ** -0.5),
        'ada_b': nrm(ks[7], (L, 6 * D_MODEL), 0.02),
        'norm1_g': gain(ks[8], (L, D_MODEL)),
        'norm2_g': gain(ks[9], (L, D_MODEL)),
        'w_in': nrm(ks[10], (L, D_MODEL, IN_COLS), D_MODEL ** -0.5),
        'w_gate': nrm(ks[11], (L, D_MODEL, N_BRANCH * D_MODEL), D_MODEL ** -0.5),
        'b_gate': nrm(ks[12], (L, N_BRANCH * D_MODEL), 0.02),
        'w_fourier': nrm(ks[13], (L, F_W, D_MODEL), F_W ** -0.5),
        'conv_dw': nrm(ks[14], (L, CONV_K, CONV_W), CONV_K ** -0.5),
        'conv_dw_b': nrm(ks[15], (L, CONV_W), 0.02),
        'conv_ln_g': gain(ks[16], (L, CONV_W)),
        'conv_ln_b': nrm(ks[17], (L, CONV_W), 0.02),
        'w_conv_out': nrm(ks[18], (L, CONV_W, D_MODEL), CONV_W ** -0.5),
        'q_norm_g': gain(ks[19], (L, Q_LORA)),
        'w_q_up': nrm(ks[20], (L, Q_LORA, N_HEADS * QK_DIM), Q_LORA ** -0.5),
        'kv_norm_g': gain(ks[21], (L, KV_LORA)),
        'w_kv_up': nrm(ks[22], (L, KV_LORA, N_HEADS * (QK_NOPE + V_DIM)), KV_LORA ** -0.5),
        'qk_q_g': gain(ks[23], (L, QK_DIM)),
        'qk_k_g': gain(ks[24], (L, QK_DIM)),
        'w_mla_out': nrm(ks[25], (L, N_HEADS * V_DIM, D_MODEL), (N_HEADS * V_DIM) ** -0.5),
        'w_out': nrm(ks[26], (L, D_MODEL, D_MODEL), D_MODEL ** -0.5),
        'ffn_up': nrm(ks[27], (L, D_MODEL, 2 * D_FF), D_MODEL ** -0.5),
        'ffn_dw': nrm(ks[28], (L, FFN_K, 2 * D_FF), FFN_K ** -0.5),
        'ffn_dw_b': nrm(ks[29], (L, 2 * D_FF), 0.02),
        'ffn_down': nrm(ks[30], (L, D_FF, D_MODEL), D_FF ** -0.5),
    }


def reference(x_prompt, x_sample, cache_ckv, cache_krope, c, c_ctx, ada_w, ada_b, norm1_g, norm2_g,
              w_in, w_gate, b_gate, w_fourier, conv_dw, conv_dw_b, conv_ln_g, conv_ln_b, w_conv_out,
              q_norm_g, w_q_up, kv_norm_g, w_kv_up, qk_q_g, qk_k_g, w_mla_out, w_out,
              ffn_up, ffn_dw, ffn_dw_b, ffn_down):
    W = {
        'ada_w': ada_w, 'ada_b': ada_b, 'norm1_g': norm1_g, 'norm2_g': norm2_g,
        'w_in': w_in, 'w_gate': w_gate, 'b_gate': b_gate, 'w_fourier': w_fourier,
        'conv_dw': conv_dw, 'conv_dw_b': conv_dw_b, 'conv_ln_g': conv_ln_g, 'conv_ln_b': conv_ln_b,
        'w_conv_out': w_conv_out, 'q_norm_g': q_norm_g, 'w_q_up': w_q_up, 'kv_norm_g': kv_norm_g,
        'w_kv_up': w_kv_up, 'qk_q_g': qk_q_g, 'qk_k_g': qk_k_g, 'w_mla_out': w_mla_out,
        'w_out': w_out, 'ffn_up': ffn_up, 'ffn_dw': ffn_dw, 'ffn_dw_b': ffn_dw_b, 'ffn_down': ffn_down,
    }
    ctx_vec = c_ctx[None, :]
    xp = x_prompt
    ckv_list = []
    krope_list = []
    for l in range(DEPTH):
        xp, ckv, kr = trunk_layer(xp, ctx_vec, W, l, None, None)
        ckv_list.append(ckv)
        krope_list.append(kr)
    new_ckv = jnp.stack(ckv_list, axis=1)
    new_krope = jnp.stack(krope_list, axis=1)
    rope = grid_rope_tables(x_sample.shape[1])
    xs = x_sample
    for l in range(DEPTH):
        xs, _, _ = trunk_layer(xs, c, W, l, rope, (cache_ckv[:, l], cache_krope[:, l]))
    return (xp, xs, new_ckv, new_krope)
```

```python
import functools
import math

import numpy as np
import jax
import jax.numpy as jnp
from jax import lax
from jax.experimental import pallas as pl
from jax.experimental.pallas import tpu as pltpu

F32 = jnp.float32
BF16 = jnp.bfloat16

EPS = 1e-6
N_HEADS = 8
QK_NOPE = 64
QK_ROPE = 32
V_DIM = 64
QK_DIM = QK_NOPE + QK_ROPE
F_GROUPS = 4
F_GROUP_W = 128
F_W = F_GROUPS * F_GROUP_W
CONV_W = 512
CONV_K = 31
Q_LORA = 384
KV_LORA = 256
GRID_W = 64
ROPE_BASE = 10000.0
FFN_K = 3
N_MOD = 6

LANES = 128
SUBLANES = 8
V7X_VMEM_BYTES = 64 * 2**20
VMEM_RESERVE_BYTES = 6 * 2**20

HEAD_TILE = LANES
HW = N_HEADS * HEAD_TILE
CONV_PAD = 16
FFN_PAD = 16

TM = 512
CONV_RB = 64
FFN_SUB = 512
FFN_CK = 256


def _vmem_limit(block_bytes, scratch_bytes, temp_bytes):
    need = 2 * block_bytes + scratch_bytes + temp_bytes
    return int(min(need, V7X_VMEM_BYTES - VMEM_RESERVE_BYTES))


def _nbytes(shape, dtype):
    return int(np.prod(shape)) * jnp.dtype(dtype).itemsize


def _dot(a, b):
    return jnp.dot(a, b, preferred_element_type=F32)


def _dot_nt(a, b):
    return lax.dot_general(a, b, (((1,), (1,)), ((), ())), preferred_element_type=F32)


def _rms(x):
    return x * lax.rsqrt(jnp.mean(x * x, axis=-1, keepdims=True) + EPS)


def _sigmoid(x):
    return 1.0 / (1.0 + jnp.exp(-x))


def _silu(x):
    return x * _sigmoid(x)


def _mod_kernel(c_ref, w_ref, b_ref, o_ref):
    s = _silu(c_ref[...]).astype(BF16)
    o_ref[...] = _dot(s, w_ref[...].astype(BF16)) + b_ref[...]


def _mod_table(cvec, ada_w, ada_b):
    depth, d, n = ada_w.shape
    rows = cvec.shape[0]
    tn = n // 4
    blocks = _nbytes((d, tn), F32) + _nbytes((rows, tn), F32) * 2
    return pl.pallas_call(
        _mod_kernel,
        grid=(depth, n // tn),
        in_specs=[
            pl.BlockSpec((rows, d), lambda l, j: (0, 0)),
            pl.BlockSpec((None, d, tn), lambda l, j: (l, 0, j)),
            pl.BlockSpec((None, 1, tn), lambda l, j: (l, 0, j)),
        ],
        out_specs=pl.BlockSpec((None, rows, tn), lambda l, j: (l, 0, j)),
        out_shape=jax.ShapeDtypeStruct((depth, rows, n), F32),
        compiler_params=pltpu.CompilerParams(
            dimension_semantics=("arbitrary", "arbitrary"),
            vmem_limit_bytes=_vmem_limit(blocks, 0, _nbytes((d, tn), BF16) + 2**20)),
        name="adaln_table",
    )(cvec, ada_w, ada_b.reshape(depth, 1, n))


def _head_norm_rope(t, gain, rope):
    ss = jnp.sum(t * t, axis=-1, keepdims=True)
    t = t * lax.rsqrt(ss * (1.0 / QK_DIM) + EPS) * gain
    if rope is not None:
        cos, sin_a, sin_b = rope
        t = (t * cos + pltpu.roll(t, HEAD_TILE - QK_ROPE // 4, axis=1) * sin_a
             + pltpu.roll(t, QK_ROPE // 4, axis=1) * sin_b)
    return t


def _in_kernel(x_ref, mod_ref, g1_ref, win_ref, cs_ref, qg_ref, wq_ref, kvg_ref, wkv_ref,
               gq_ref, gk_ref, rope_ref,
               fa_ref, ug_ref, q_ref, k_ref, v_ref, ckv_ref, kr_ref, *, d):
    sh1 = mod_ref[:, 0:d]
    sc1 = mod_ref[:, d:2 * d]
    h = _rms(x_ref[...]) * g1_ref[...] * (1.0 + sc1) + sh1
    hb = h.astype(BF16)

    c0 = 0
    fb = _dot(hb, win_ref[:, c0:c0 + F_W]).astype(BF16)
    for g in range(F_GROUPS):
        a = _dot(fb[:, g * F_GROUP_W:(g + 1) * F_GROUP_W], cs_ref[...])
        fa_ref[:, g * F_GROUP_W:(g + 1) * F_GROUP_W] = a[:, :F_GROUP_W].astype(BF16)
        fa_ref[:, F_W + g * F_GROUP_W:F_W + (g + 1) * F_GROUP_W] = a[:, F_GROUP_W:].astype(BF16)
    c0 += F_W

    ga = _dot(hb, win_ref[:, c0:c0 + CONV_W])
    gg = _dot(hb, win_ref[:, c0 + CONV_W:c0 + 2 * CONV_W])
    ug_ref[...] = ga * _sigmoid(gg)
    c0 += 2 * CONV_W

    cos = rope_ref[:, 0:HEAD_TILE]
    sin_a = rope_ref[:, HEAD_TILE:2 * HEAD_TILE]
    sin_b = rope_ref[:, 2 * HEAD_TILE:3 * HEAD_TILE]
    rope = (cos, sin_a, sin_b)

    qn = (_rms(_dot(hb, win_ref[:, c0:c0 + Q_LORA])) * qg_ref[...]).astype(BF16)
    c0 += Q_LORA
    q_gain = gq_ref[...] * (1.0 / math.sqrt(QK_DIM))
    for hd in range(N_HEADS):
        sl = slice(hd * HEAD_TILE, (hd + 1) * HEAD_TILE)
        t = _dot(qn, wq_ref[:, sl])
        q_ref[:, sl] = _head_norm_rope(t, q_gain, rope).astype(BF16)

    ckv = _rms(_dot(hb, win_ref[:, c0:c0 + KV_LORA])) * kvg_ref[...]
    ckv_ref[...] = ckv
    c0 += KV_LORA
    kr = _dot(hb, win_ref[:, c0:c0 + HEAD_TILE])
    kr_ref[...] = kr
    cb = ckv.astype(BF16)
    k_gain = gk_ref[...]
    for hd in range(N_HEADS):
        sl = slice(hd * HEAD_TILE, (hd + 1) * HEAD_TILE)
        t = _dot(cb, wkv_ref[:, sl]) + kr
        k_ref[:, sl] = _head_norm_rope(t, k_gain, rope).astype(BF16)
        vs = slice(HW + hd * HEAD_TILE, HW + (hd + 1) * HEAD_TILE)
        v_ref[:, sl] = _dot(cb, wkv_ref[:, vs]).astype(BF16)


def _in_proj(x, mod, l, P, rope_tab, *, n_p, per_seq):
    t_rows, d = x.shape
    n_tiles = t_rows // TM
    win_cols = P["w_in"].shape[-1]

    def mod_idx(i):
        return (l, jnp.where(i < n_p, 0, 1 + (i - n_p) // per_seq), 0, 0)

    def rope_idx(i):
        return (jnp.where(i < n_p, per_seq, (i - n_p) % per_seq), 0)

    row = lambda w: pl.BlockSpec((TM, w), lambda i: (i, 0))
    lay = lambda *s: pl.BlockSpec((None,) + s, lambda i: (l,) + (0,) * len(s))
    out_shapes = [
        jax.ShapeDtypeStruct((t_rows, 2 * F_W), BF16),
        jax.ShapeDtypeStruct((t_rows, CONV_W), F32),
        jax.ShapeDtypeStruct((t_rows, HW), BF16),
        jax.ShapeDtypeStruct((t_rows, HW), BF16),
        jax.ShapeDtypeStruct((t_rows, HW), BF16),
        jax.ShapeDtypeStruct((t_rows, KV_LORA), F32),
        jax.ShapeDtypeStruct((t_rows, HEAD_TILE), F32),
    ]
    blocks = (_nbytes((TM, d), F32) + _nbytes((d, win_cols), BF16) + _nbytes((Q_LORA, HW), BF16)
              + _nbytes((KV_LORA, 2 * HW), BF16) + _nbytes((TM, 3 * HEAD_TILE), F32)
              + sum(_nbytes((TM, s.shape[1]), s.dtype) for s in out_shapes))
    return pl.pallas_call(
        functools.partial(_in_kernel, d=d),
        grid=(n_tiles,),
        in_specs=[
            row(d),
            pl.BlockSpec((None, None, 1, N_MOD * d), mod_idx),
            lay(1, d),
            lay(d, win_cols),
            pl.BlockSpec((F_GROUP_W, 2 * F_GROUP_W), lambda i: (0, 0)),
            lay(1, Q_LORA),
            lay(Q_LORA, HW),
            lay(1, KV_LORA),
            lay(KV_LORA, 2 * HW),
            lay(1, HEAD_TILE),
            lay(1, HEAD_TILE),
            pl.BlockSpec((TM, 3 * HEAD_TILE), rope_idx),
        ],
        out_specs=[row(s.shape[1]) for s in out_shapes],
        out_shape=out_shapes,
        compiler_params=pltpu.CompilerParams(
            dimension_semantics=("arbitrary",),
            vmem_limit_bytes=_vmem_limit(blocks, 0, 12 * _nbytes((TM, d), F32))),
        name="in_proj",
    )(x, mod, P["norm1_g"], P["w_in"], P["dft_cs"], P["q_norm_g"], P["w_q"], P["kv_norm_g"],
      P["w_kv"], P["gq"], P["gk"], rope_tab)


def _cache_kv_kernel(ckv_ref, kr_ref, wkv_ref, gk_ref, k_ref, v_ref):
    cb = ckv_ref[...].astype(BF16)
    kr = kr_ref[...]
    k_gain = gk_ref[...]
    for hd in range(N_HEADS):
        sl = slice(hd * HEAD_TILE, (hd + 1) * HEAD_TILE)
        t = _dot(cb, wkv_ref[:, sl]) + kr
        k_ref[:, sl] = _head_norm_rope(t, k_gain, None).astype(BF16)
        vs = slice(HW + hd * HEAD_TILE, HW + (hd + 1) * HEAD_TILE)
        v_ref[:, sl] = _dot(cb, wkv_ref[:, vs]).astype(BF16)


def _cache_kv(cache_ckv, cache_kr_tile, P):
    nb, depth, past, _ = cache_ckv.shape
    blk = lambda w: pl.BlockSpec((None, None, past, w), lambda b, l: (b, l, 0, 0))
    lay = lambda *s: pl.BlockSpec((None,) + s, lambda b, l: (l,) + (0,) * len(s))
    blocks = (_nbytes((past, KV_LORA + HEAD_TILE), F32) + _nbytes((KV_LORA, 2 * HW), BF16)
              + 2 * _nbytes((past, HW), BF16))
    return pl.pallas_call(
        _cache_kv_kernel,
        grid=(nb, depth),
        in_specs=[blk(KV_LORA), blk(HEAD_TILE), lay(KV_LORA, 2 * HW), lay(1, HEAD_TILE)],
        out_specs=[blk(HW), blk(HW)],
        out_shape=[jax.ShapeDtypeStruct((nb, depth, past, HW), BF16)] * 2,
        compiler_params=pltpu.CompilerParams(
            dimension_semantics=("arbitrary", "arbitrary"),
            vmem_limit_bytes=_vmem_limit(blocks, 0, 8 * _nbytes((past, HW), F32))),
        name="cache_kv",
    )(cache_ckv, cache_kr_tile, P["w_kv"], P["gk"])


def _fourier_kernel(fa_ref, c_ref, s_ref, o_ref, *, seq, n_seq):
    for s in range(n_seq):
        rows = slice(s * seq, (s + 1) * seq)
        y = _dot(c_ref[...], fa_ref[rows, 0:F_W]) - _dot(s_ref[...], fa_ref[rows, F_W:2 * F_W])
        o_ref[rows, :] = y.astype(BF16)


def _fourier_prompt(fa, dft_c, dft_s, *, t_p, seq):
    n_seq = TM // seq
    blocks = _nbytes((TM, 2 * F_W), BF16) + 2 * _nbytes((seq, seq), BF16) + _nbytes((TM, F_W), BF16)
    return pl.pallas_call(
        functools.partial(_fourier_kernel, seq=seq, n_seq=n_seq),
        grid=(t_p // TM,),
        in_specs=[
            pl.BlockSpec((TM, 2 * F_W), lambda i: (i, 0)),
            pl.BlockSpec((seq, seq), lambda i: (0, 0)),
            pl.BlockSpec((seq, seq), lambda i: (0, 0)),
        ],
        out_specs=pl.BlockSpec((TM, F_W), lambda i: (i, 0)),
        out_shape=jax.ShapeDtypeStruct((t_p, F_W), BF16),
        compiler_params=pltpu.CompilerParams(
            dimension_semantics=("arbitrary",),
            vmem_limit_bytes=_vmem_limit(blocks, 0, 4 * _nbytes((TM, F_W), F32))),
        name="fourier_prompt",
    )(fa, dft_c, dft_s)


def _fourier_sample_kernel(fa_ref, c_ref, s_ref, o_ref):
    y = _dot(c_ref[...], fa_ref[:, 0:F_W]) - _dot(s_ref[...], fa_ref[:, F_W:2 * F_W])
    o_ref[...] = y.astype(BF16)


def _fourier_sample(fa, dft_c, dft_s, *, t_p, n_b, seq):
    tr = TM
    per = seq // tr
    first = t_p // seq
    blocks = _nbytes((seq, 2 * F_W), BF16) + 2 * _nbytes((tr, seq), BF16) + _nbytes((tr, F_W), BF16)
    return pl.pallas_call(
        _fourier_sample_kernel,
        grid=(n_b, per),
        in_specs=[
            pl.BlockSpec((seq, 2 * F_W), lambda b, j: (first + b, 0)),
            pl.BlockSpec((tr, seq), lambda b, j: (j, 0)),
            pl.BlockSpec((tr, seq), lambda b, j: (j, 0)),
        ],
        out_specs=pl.BlockSpec((tr, F_W), lambda b, j: (b * per + j, 0)),
        out_shape=jax.ShapeDtypeStruct((n_b * seq, F_W), BF16),
        compiler_params=pltpu.CompilerParams(
            dimension_semantics=("arbitrary", "arbitrary"),
            vmem_limit_bytes=_vmem_limit(blocks, 0, 4 * _nbytes((tr, F_W), F32))),
        name="fourier_sample",
    )(fa, dft_c, dft_s)


def _conv_kernel(u_ref, dw_ref, dwb_ref, lng_ref, lnb_ref, o_ref, pad_ref, *, seq):
    zeros = jnp.zeros((CONV_PAD, CONV_W), F32)
    pad_ref[0:CONV_PAD, :] = zeros
    pad_ref[CONV_PAD + seq:2 * CONV_PAD + seq, :] = zeros
    pad_ref[CONV_PAD:CONV_PAD + seq, :] = u_ref[...]
    half = CONV_K // 2

    def body(rb, carry):
        r0 = pl.multiple_of(rb * CONV_RB, CONV_RB)
        win = pad_ref[pl.ds(r0, CONV_RB + 2 * CONV_PAD), :]
        acc = jnp.zeros((CONV_RB, CONV_W), F32)
        for phase in range(SUBLANES):
            shifted = win[phase:phase + CONV_RB + 2 * CONV_PAD - SUBLANES, :]
            for k in range(CONV_K):
                off = CONV_PAD - half + k
                if off % SUBLANES == phase:
                    base = off - phase
                    acc = acc + dw_ref[k:k + 1, :] * shifted[base:base + CONV_RB, :]
        acc = acc + dwb_ref[...]
        mu = jnp.mean(acc, axis=-1, keepdims=True)
        cen = acc - mu
        var = jnp.mean(cen * cen, axis=-1, keepdims=True)
        y = cen * lax.rsqrt(var + EPS) * lng_ref[...] + lnb_ref[...]
        o_ref[pl.ds(r0, CONV_RB), :] = _silu(y).astype(BF16)
        return carry

    lax.fori_loop(0, seq // CONV_RB, body, 0)


def _conv_module(ug, l, P, *, first, n_seq, seq):
    lay = lambda *s: pl.BlockSpec((None,) + s, lambda i: (l,) + (0,) * len(s))
    blocks = _nbytes((seq, CONV_W), F32) + _nbytes((seq, CONV_W), BF16) + _nbytes((CONV_K + 4, CONV_W), F32)
    scratch = _nbytes((seq + 2 * CONV_PAD, CONV_W), F32)
    return pl.pallas_call(
        functools.partial(_conv_kernel, seq=seq),
        grid=(n_seq,),
        in_specs=[
            pl.BlockSpec((seq, CONV_W), lambda i: (first + i, 0)),
            lay(CONV_K, CONV_W), lay(1, CONV_W), lay(1, CONV_W), lay(1, CONV_W),
        ],
        out_specs=pl.BlockSpec((seq, CONV_W), lambda i: (i, 0)),
        out_shape=jax.ShapeDtypeStruct((n_seq * seq, CONV_W), BF16),
        scratch_shapes=[pltpu.VMEM((seq + 2 * CONV_PAD, CONV_W), F32)],
        compiler_params=pltpu.CompilerParams(
            dimension_semantics=("arbitrary",),
            vmem_limit_bytes=_vmem_limit(blocks, scratch, 4 * 2**20)),
        name=f"conv_seq{seq}",
    )(ug, P["conv_dw"], P["conv_dw_b"], P["conv_ln_g"], P["conv_ln_b"])


def _attn_kernel(*refs, with_cache):
    if with_cache:
        q_ref, k_ref, v_ref, kc_ref, vc_ref, o_ref = refs
    else:
        q_ref, k_ref, v_ref, o_ref = refs
    tq = q_ref.shape[0]
    lane = lax.broadcasted_iota(jnp.int32, (tq, HEAD_TILE), 1)
    for hp in range(N_HEADS // 2):
        acc = None
        inv = []
        for hh in range(2):
            sl = slice((2 * hp + hh) * HEAD_TILE, (2 * hp + hh + 1) * HEAD_TILE)
            qh = q_ref[:, sl]
            s_m = _dot_nt(qh, k_ref[:, sl])
            m = jnp.max(s_m, axis=-1, keepdims=True)
            if with_cache:
                s_c = _dot_nt(qh, kc_ref[:, sl])
                m = jnp.maximum(m, jnp.max(s_c, axis=-1, keepdims=True))
            e_m = jnp.exp(s_m - m)
            den = jnp.sum(e_m, axis=-1, keepdims=True)
            pv = _dot(e_m.astype(BF16), v_ref[:, sl])
            if with_cache:
                e_c = jnp.exp(s_c - m)
                den = den + jnp.sum(e_c, axis=-1, keepdims=True)
                pv = pv + _dot(e_c.astype(BF16), vc_ref[:, sl])
            acc = pv if acc is None else acc + pv
            inv.append(1.0 / den)
        scale = jnp.where(lane < V_DIM, inv[0], inv[1])
        o_ref[:, hp * HEAD_TILE:(hp + 1) * HEAD_TILE] = (acc * scale).astype(BF16)


def _attn_prompt(q, k, v, *, n_seq, seq):
    blk = pl.BlockSpec((seq, HW), lambda i: (i, 0))
    blocks = 3 * _nbytes((seq, HW), BF16) + _nbytes((seq, N_HEADS * V_DIM), BF16)
    return pl.pallas_call(
        functools.partial(_attn_kernel, with_cache=False),
        grid=(n_seq,),
        in_specs=[blk, blk, blk],
        out_specs=pl.BlockSpec((seq, N_HEADS * V_DIM), lambda i: (i, 0)),
        out_shape=jax.ShapeDtypeStruct((n_seq * seq, N_HEADS * V_DIM), BF16),
        compiler_params=pltpu.CompilerParams(
            dimension_semantics=("arbitrary",),
            vmem_limit_bytes=_vmem_limit(blocks, 0, 8 * _nbytes((seq, seq), F32) + 2**21)),
        name="attn_prompt",
    )(q, k, v)


def _attn_sample(q, k, v, kc, vc, l, *, t_p, n_b, seq, tq):
    per = seq // tq
    first_q = t_p // tq
    first_s = t_p // seq
    past = kc.shape[2]
    kv_blk = pl.BlockSpec((seq, HW), lambda b, j: (first_s + b, 0))
    c_blk = pl.BlockSpec((None, None, past, HW), lambda b, j: (b, l, 0, 0))
    blocks = (_nbytes((tq, HW), BF16) + 2 * _nbytes((seq + past, HW), BF16)
              + _nbytes((tq, N_HEADS * V_DIM), BF16))
    return pl.pallas_call(
        functools.partial(_attn_kernel, with_cache=True),
        grid=(n_b, per),
        in_specs=[pl.BlockSpec((tq, HW), lambda b, j: (first_q + b * per + j, 0)),
                  kv_blk, kv_blk, c_blk, c_blk],
        out_specs=pl.BlockSpec((tq, N_HEADS * V_DIM), lambda b, j: (b * per + j, 0)),
        out_shape=jax.ShapeDtypeStruct((n_b * seq, N_HEADS * V_DIM), BF16),
        compiler_params=pltpu.CompilerParams(
            dimension_semantics=("arbitrary", "arbitrary"),
            vmem_limit_bytes=_vmem_limit(blocks, 0, 6 * _nbytes((tq, seq + past), F32))),
        name="attn_sample",
    )(q, k, v, kc, vc)


def _merge_kernel(x_ref, mod_ref, g1_ref, fp_ref, fs_ref, cp_ref, cs_ref, ap_ref, as_ref,
                  wg_ref, bg_ref, wf_ref, wc_ref, wa_ref, wo_ref, o_ref, *, d, n_p):
    is_prompt = pl.program_id(0) < n_p
    x = x_ref[...]
    sh1 = mod_ref[:, 0:d]
    sc1 = mod_ref[:, d:2 * d]
    g1 = mod_ref[:, 2 * d:3 * d]
    hb = (_rms(x) * g1_ref[...] * (1.0 + sc1) + sh1).astype(BF16)
    merged = None
    branches = ((fp_ref, fs_ref, wf_ref), (cp_ref, cs_ref, wc_ref), (ap_ref, as_ref, wa_ref))
    for b, (yp_ref, ys_ref, w_ref) in enumerate(branches):
        y_in = jnp.where(is_prompt, yp_ref[...], ys_ref[...])
        y = _dot(y_in, w_ref[...])
        gate = _sigmoid(_dot(hb, wg_ref[:, b * d:(b + 1) * d]) + bg_ref[:, b * d:(b + 1) * d])
        merged = gate * y if merged is None else merged + gate * y
    o_ref[...] = x + g1 * _dot(merged.astype(BF16), wo_ref[...])


def _merge(x, mod, l, P, yf, yc, ya, *, n_p, per_seq):
    t_rows, d = x.shape
    n_tiles = t_rows // TM
    n_s = n_tiles - n_p

    def mod_idx(i):
        return (l, jnp.where(i < n_p, 0, 1 + (i - n_p) // per_seq), 0, 0)

    row = pl.BlockSpec((TM, d), lambda i: (i, 0))
    lay = lambda *s: pl.BlockSpec((None,) + s, lambda i: (l,) + (0,) * len(s))
    p_blk = pl.BlockSpec((TM, F_W), lambda i: (jnp.minimum(i, n_p - 1), 0))
    s_blk = pl.BlockSpec((TM, F_W), lambda i: (jnp.clip(i - n_p, 0, n_s - 1), 0))
    blocks = (2 * _nbytes((TM, d), F32) + 6 * _nbytes((TM, F_W), BF16) + _nbytes((d, 3 * d), BF16)
              + 3 * _nbytes((F_W, d), BF16) + _nbytes((d, d), BF16))
    return pl.pallas_call(
        functools.partial(_merge_kernel, d=d, n_p=n_p),
        grid=(n_tiles,),
        in_specs=[row, pl.BlockSpec((None, None, 1, N_MOD * d), mod_idx), lay(1, d),
                  p_blk, s_blk, p_blk, s_blk, p_blk, s_blk,
                  lay(d, 3 * d), lay(1, 3 * d), lay(F_W, d), lay(CONV_W, d), lay(N_HEADS * V_DIM, d),
                  lay(d, d)],
        out_specs=row,
        out_shape=jax.ShapeDtypeStruct((t_rows, d), F32),
        compiler_params=pltpu.CompilerParams(
            dimension_semantics=("arbitrary",),
            vmem_limit_bytes=_vmem_limit(blocks, 0, 8 * _nbytes((TM, d), F32))),
        name="merge",
    )(x, mod, P["norm1_g"], yf[0], yf[1], yc[0], yc[1], ya[0], ya[1],
      P["w_gate"], P["b_gate"], P["w_fourier"], P["w_conv_out"], P["w_mla_out"], P["w_out"])


def _ffn_kernel(x_ref, mod_ref, g2_ref, ua_ref, ub_ref, dwa_ref, dwb_ref, ba_ref, bb_ref, dn_ref,
                o_ref, h_ref, *, d, rows, n_p, seq_p, seq_s):
    i = pl.program_id(0)
    f = pl.program_id(1)
    n_sub = rows // FFN_SUB

    @pl.when(f == 0)
    def _():
        sh2 = mod_ref[:, 3 * d:4 * d]
        sc2 = mod_ref[:, 4 * d:5 * d]
        zeros = jnp.zeros((FFN_PAD, d), BF16)
        h_ref[0:FFN_PAD, :] = zeros
        h_ref[FFN_PAD + rows:2 * FFN_PAD + rows, :] = zeros
        for sb in range(n_sub):
            r = slice(sb * FFN_SUB, (sb + 1) * FFN_SUB)
            h2 = _rms(x_ref[r, :]) * g2_ref[...] * (1.0 + sc2) + sh2
            h_ref[FFN_PAD + sb * FFN_SUB:FFN_PAD + (sb + 1) * FFN_SUB, :] = h2.astype(BF16)
            o_ref[r, :] = jnp.zeros((FFN_SUB, d), F32)

    period_mask = jnp.where(i < n_p, seq_p - 1, seq_s - 1)
    ext = FFN_SUB + 2 * FFN_PAD
    for sb in range(n_sub):
        hx = h_ref[sb * FFN_SUB:sb * FFN_SUB + ext, :]
        pos = (sb * FFN_SUB + lax.broadcasted_iota(jnp.int32, (FFN_SUB, 1), 0)) & period_mask
        is_first = pos == 0
        is_last = pos == period_mask

        def conv(u, dw_ref, b_ref):
            prev = jnp.where(is_first, 0.0, u[FFN_PAD - 1:FFN_PAD - 1 + FFN_SUB, :])
            mid = u[FFN_PAD:FFN_PAD + FFN_SUB, :]
            nxt = jnp.where(is_last, 0.0, u[FFN_PAD + 1:FFN_PAD + 1 + FFN_SUB, :])
            return dw_ref[0:1, :] * prev + dw_ref[1:2, :] * mid + dw_ref[2:3, :] * nxt + b_ref[...]

        ca = conv(_dot(hx, ua_ref[...]), dwa_ref, ba_ref)
        cb = conv(_dot(hx, ub_ref[...]), dwb_ref, bb_ref)
        act = (_silu(ca) * cb).astype(BF16)
        r = slice(sb * FFN_SUB, (sb + 1) * FFN_SUB)
        o_ref[r, :] += _dot(act, dn_ref[...])

    @pl.when(f == pl.num_programs(1) - 1)
    def _():
        g2 = mod_ref[:, 5 * d:6 * d]
        for sb in range(n_sub):
            r = slice(sb * FFN_SUB, (sb + 1) * FFN_SUB)
            o_ref[r, :] = x_ref[r, :] + g2 * o_ref[r, :]


def _ffn(x, mod, l, P, *, rows, n_p, seq_p, seq_s):
    t_rows, d = x.shape
    d_ff = P["ffn_down"].shape[1]
    n_ck = d_ff // FFN_CK
    n_tiles = t_rows // rows

    def mod_idx(i, f):
        return (l, jnp.where(i < n_p, 0, 1 + (i - n_p)), 0, 0)

    row = pl.BlockSpec((rows, d), lambda i, f: (i, 0))
    blocks = (2 * _nbytes((rows, d), F32) + 2 * _nbytes((d, FFN_CK), BF16)
              + _nbytes((FFN_CK, d), BF16) + 4 * _nbytes((8, FFN_CK), F32))
    scratch = _nbytes((rows + 2 * FFN_PAD, d), BF16)
    return pl.pallas_call(
        functools.partial(_ffn_kernel, d=d, rows=rows, n_p=n_p, seq_p=seq_p, seq_s=seq_s),
        grid=(n_tiles, n_ck),
        in_specs=[
            row,
            pl.BlockSpec((None, None, 1, N_MOD * d), mod_idx),
            pl.BlockSpec((None, 1, d), lambda i, f: (l, 0, 0)),
            pl.BlockSpec((None, d, FFN_CK), lambda i, f: (l, 0, f)),
            pl.BlockSpec((None, d, FFN_CK), lambda i, f: (l, 0, n_ck + f)),
            pl.BlockSpec((None, FFN_K, FFN_CK), lambda i, f: (l, 0, f)),
            pl.BlockSpec((None, FFN_K, FFN_CK), lambda i, f: (l, 0, n_ck + f)),
            pl.BlockSpec((None, 1, FFN_CK), lambda i, f: (l, 0, f)),
            pl.BlockSpec((None, 1, FFN_CK), lambda i, f: (l, 0, n_ck + f)),
            pl.BlockSpec((None, FFN_CK, d), lambda i, f: (l, f, 0)),
        ],
        out_specs=row,
        out_shape=jax.ShapeDtypeStruct((t_rows, d), F32),
        scratch_shapes=[pltpu.VMEM((rows + 2 * FFN_PAD, d), BF16)],
        compiler_params=pltpu.CompilerParams(
            dimension_semantics=("arbitrary", "arbitrary"),
            vmem_limit_bytes=_vmem_limit(blocks, scratch, 10 * _nbytes((FFN_SUB + 2 * FFN_PAD, 2 * FFN_CK), F32))),
        name="conv_ffn",
    )(x, mod, P["norm2_g"], P["ffn_up"], P["ffn_up"], P["ffn_dw"], P["ffn_dw"],
      P["ffn_dw_b"], P["ffn_dw_b"], P["ffn_down"])


def _dft_tables(n, scale):
    j = jnp.arange(n, dtype=jnp.int32)
    jk = (j[:, None] * j[None, :]) % n
    ang = jk.astype(F32) * (2.0 * math.pi / n)
    return jnp.cos(ang) * scale, jnp.sin(ang) * scale


def _rope_table(seq, ident_rows):
    rows = seq // GRID_W
    row = jnp.repeat(jnp.arange(rows), GRID_W, total_repeat_length=seq).astype(F32)
    col = jnp.tile(jnp.arange(GRID_W), rows).astype(F32)
    half = QK_ROPE // 2
    inv = ROPE_BASE ** (-jnp.arange(0, half, 2, dtype=F32) / half)
    ang = jnp.stack([row[:, None] * inv, col[:, None] * inv], axis=1)
    cos, sin = jnp.cos(ang), jnp.sin(ang)
    zero = jnp.zeros_like(sin)
    cos_r = jnp.stack([cos, cos], axis=2).reshape(seq, QK_ROPE)
    sin_a = jnp.stack([-sin, zero], axis=2).reshape(seq, QK_ROPE)
    sin_b = jnp.stack([zero, sin], axis=2).reshape(seq, QK_ROPE)

    def tile(mid, fill):
        left = jnp.full((seq, QK_NOPE), fill, F32)
        right = jnp.full((seq, HEAD_TILE - QK_DIM), fill, F32)
        return jnp.concatenate([left, mid, right], axis=1)

    tab = jnp.concatenate([tile(cos_r, 1.0), tile(sin_a, 0.0), tile(sin_b, 0.0)], axis=1)
    ident = jnp.concatenate([jnp.ones((ident_rows, HEAD_TILE), F32),
                             jnp.zeros((ident_rows, 2 * HEAD_TILE), F32)], axis=1)
    return jnp.concatenate([tab, ident], axis=0)


def _prepare_params(W):
    depth, d, _ = W["w_in"].shape
    P = {}
    split = F_W + 2 * CONV_W + Q_LORA + KV_LORA
    w_in = W["w_in"]
    P["w_in"] = jnp.concatenate(
        [w_in[:, :, :split], jnp.zeros((depth, d, QK_NOPE), F32), w_in[:, :, split:],
         jnp.zeros((depth, d, HEAD_TILE - QK_DIM), F32)], axis=2).astype(BF16)
    wq = W["w_q_up"].reshape(depth, Q_LORA, N_HEADS, QK_DIM)
    wq = jnp.pad(wq, ((0, 0), (0, 0), (0, 0), (0, HEAD_TILE - QK_DIM)))
    P["w_q"] = wq.reshape(depth, Q_LORA, HW).astype(BF16)
    wkv = W["w_kv_up"].reshape(depth, KV_LORA, N_HEADS, QK_NOPE + V_DIM)
    zk = jnp.zeros((depth, KV_LORA, N_HEADS, HEAD_TILE - QK_NOPE), F32)
    wk = jnp.concatenate([wkv[..., :QK_NOPE], zk], axis=-1)
    zv = jnp.zeros((depth, KV_LORA, N_HEADS, HEAD_TILE - V_DIM), F32)
    wv_even = jnp.concatenate([wkv[..., QK_NOPE:], zv], axis=-1)
    wv_odd = jnp.concatenate([zv, wkv[..., QK_NOPE:]], axis=-1)
    odd = (jnp.arange(N_HEADS) % 2 == 1)[None, None, :, None]
    wv = jnp.where(odd, wv_odd, wv_even)
    P["w_kv"] = jnp.concatenate([wk.reshape(depth, KV_LORA, HW), wv.reshape(depth, KV_LORA, HW)],
                                axis=2).astype(BF16)
    pad_g = lambda g: jnp.pad(g, ((0, 0), (0, HEAD_TILE - QK_DIM))).reshape(depth, 1, HEAD_TILE)
    P["gq"] = pad_g(W["qk_q_g"])
    P["gk"] = pad_g(W["qk_k_g"])
    vec = lambda a: a.reshape(depth, 1, a.shape[-1])
    for name in ("norm1_g", "norm2_g", "q_norm_g", "kv_norm_g", "b_gate", "conv_dw_b", "conv_ln_g",
                 "conv_ln_b", "ffn_dw_b"):
        P[name] = vec(W[name])
    P["conv_dw"] = W["conv_dw"]
    P["ffn_dw"] = W["ffn_dw"]
    for name in ("w_gate", "w_fourier", "w_conv_out", "w_mla_out", "w_out", "ffn_up", "ffn_down"):
        P[name] = W[name].astype(BF16)
    return P


def kernel(x_prompt, x_sample, cache_ckv, cache_krope, c, c_ctx, ada_w, ada_b, norm1_g, norm2_g, w_in, w_gate, b_gate, w_fourier, conv_dw, conv_dw_b, conv_ln_g, conv_ln_b, w_conv_out, q_norm_g, w_q_up, kv_norm_g, w_kv_up, qk_q_g, qk_k_g, w_mla_out, w_out, ffn_up, ffn_dw, ffn_dw_b, ffn_down):
    W = dict(norm1_g=norm1_g, norm2_g=norm2_g, w_in=w_in, w_gate=w_gate, b_gate=b_gate,
             w_fourier=w_fourier, conv_dw=conv_dw, conv_dw_b=conv_dw_b, conv_ln_g=conv_ln_g,
             conv_ln_b=conv_ln_b, w_conv_out=w_conv_out, q_norm_g=q_norm_g, w_q_up=w_q_up,
             kv_norm_g=kv_norm_g, w_kv_up=w_kv_up, qk_q_g=qk_q_g, qk_k_g=qk_k_g,
             w_mla_out=w_mla_out, w_out=w_out, ffn_up=ffn_up, ffn_dw=ffn_dw, ffn_dw_b=ffn_dw_b,
             ffn_down=ffn_down)
    n_bp, seq_p, d = x_prompt.shape
    n_bs, seq_s, _ = x_sample.shape
    depth = ada_w.shape[0]
    t_p = n_bp * seq_p
    t_s = n_bs * seq_s
    assert t_p % seq_s == 0 and seq_s % TM == 0 and TM % seq_p == 0 and seq_s % FFN_SUB == 0
    assert seq_p & (seq_p - 1) == 0 and seq_s & (seq_s - 1) == 0 and n_bs + 1 <= 8
    n_p = t_p // TM
    per_seq = seq_s // TM

    P = _prepare_params(W)
    c_cs, s_cs = _dft_tables(F_GROUP_W, 1.0 / math.sqrt(F_GROUP_W))
    P["dft_cs"] = jnp.concatenate([c_cs, s_cs], axis=1).astype(BF16)
    dft_p = [t.astype(BF16) for t in _dft_tables(seq_p, 1.0 / math.sqrt(seq_p))]
    dft_s = [t.astype(BF16) for t in _dft_tables(seq_s, 1.0 / math.sqrt(seq_s))]
    rope_tab = _rope_table(seq_s, TM)

    cvec = jnp.concatenate([c_ctx[None, :], c, jnp.zeros((8 - 1 - n_bs, d), F32)], axis=0)
    mod = _mod_table(cvec, ada_w, ada_b).reshape(depth, 8, 1, N_MOD * d)

    kr_tile = jnp.pad(cache_krope, ((0, 0), (0, 0), (0, 0), (QK_NOPE, HEAD_TILE - QK_DIM)))
    kc, vc = _cache_kv(cache_ckv, kr_tile, P)

    x = jnp.concatenate([x_prompt.reshape(t_p, d), x_sample.reshape(t_s, d)], axis=0)
    ckv_out, kr_out = [], []
    for l in range(depth):
        fa, ug, q, k, v, ckv, kr = _in_proj(x, mod, l, P, rope_tab, n_p=n_p, per_seq=per_seq)
        ckv_out.append(ckv[:t_p].reshape(n_bp, seq_p, KV_LORA))
        kr_out.append(kr[:t_p, QK_NOPE:QK_DIM].reshape(n_bp, seq_p, QK_ROPE))
        yf = (_fourier_prompt(fa, *dft_p, t_p=t_p, seq=seq_p),
              _fourier_sample(fa, *dft_s, t_p=t_p, n_b=n_bs, seq=seq_s))
        yc = (_conv_module(ug, l, P, first=0, n_seq=n_bp, seq=seq_p),
              _conv_module(ug, l, P, first=t_p // seq_s, n_seq=n_bs, seq=seq_s))
        ya = (_attn_prompt(q, k, v, n_seq=n_bp, seq=seq_p),
              _attn_sample(q, k, v, kc, vc, l, t_p=t_p, n_b=n_bs, seq=seq_s, tq=256))
        x = _merge(x, mod, l, P, yf, yc, ya, n_p=n_p, per_seq=per_seq)
        x = _ffn(x, mod, l, P, rows=seq_s, n_p=t_p // seq_s, seq_p=seq_p, seq_s=seq_s)
    y_prompt = x[:t_p].reshape(n_bp, seq_p, d)
    y_sample = x[t_p:].reshape(n_bs, seq_s, d)
    return (y_prompt, y_sample, jnp.stack(ckv_out, axis=1), jnp.stack(kr_out, axis=1))
```

```python
import functools
import math

import numpy as np
import jax
import jax.numpy as jnp
from jax import lax
from jax.experimental import pallas as pl
from jax.experimental.pallas import tpu as pltpu

F32 = jnp.float32
BF16 = jnp.bfloat16

EPS = 1e-6
N_HEADS = 8
QK_NOPE = 64
QK_ROPE = 32
V_DIM = 64
QK_DIM = QK_NOPE + QK_ROPE
F_GROUPS = 4
F_GROUP_W = 128
F_W = F_GROUPS * F_GROUP_W
CONV_W = 512
CONV_K = 31
Q_LORA = 384
KV_LORA = 256
GRID_W = 64
ROPE_BASE = 10000.0
FFN_K = 3
N_MOD = 6

LANES = 128
SUBLANES = 8
V7X_VMEM_BYTES = 64 * 2**20
VMEM_RESERVE_BYTES = 6 * 2**20

HEAD_TILE = LANES
HW = N_HEADS * HEAD_TILE
CONV_PAD = 16
FFN_PAD = 16

TM = 512
CONV_RB = 64
FFN_SUB = 512
FFN_CK = 256


def _vmem_limit(block_bytes, scratch_bytes, temp_bytes):
    need = 2 * block_bytes + scratch_bytes + temp_bytes
    return int(min(need, V7X_VMEM_BYTES - VMEM_RESERVE_BYTES))


def _nbytes(shape, dtype):
    return int(np.prod(shape)) * jnp.dtype(dtype).itemsize


def _dot(a, b):
    return jnp.dot(a, b, preferred_element_type=F32)


def _dot_nt(a, b):
    return lax.dot_general(a, b, (((1,), (1,)), ((), ())), preferred_element_type=F32)


def _rms(x):
    return x * lax.rsqrt(jnp.mean(x * x, axis=-1, keepdims=True) + EPS)


def _sigmoid(x):
    return 1.0 / (1.0 + jnp.exp(-x))


def _silu(x):
    return x * _sigmoid(x)


def _mod_kernel(c_ref, w_ref, b_ref, o_ref):
    s = _silu(c_ref[...]).astype(BF16)
    o_ref[...] = _dot(s, w_ref[...].astype(BF16)) + b_ref[...]


def _mod_table(cvec, ada_w, ada_b):
    depth, d, n = ada_w.shape
    rows = cvec.shape[0]
    tn = n // 4
    blocks = _nbytes((d, tn), F32) + _nbytes((rows, tn), F32) * 2
    return pl.pallas_call(
        _mod_kernel,
        grid=(depth, n // tn),
        in_specs=[
            pl.BlockSpec((rows, d), lambda l, j: (0, 0)),
            pl.BlockSpec((None, d, tn), lambda l, j: (l, 0, j)),
            pl.BlockSpec((None, 1, tn), lambda l, j: (l, 0, j)),
        ],
        out_specs=pl.BlockSpec((None, rows, tn), lambda l, j: (l, 0, j)),
        out_shape=jax.ShapeDtypeStruct((depth, rows, n), F32),
        compiler_params=pltpu.CompilerParams(
            dimension_semantics=("arbitrary", "arbitrary"),
            vmem_limit_bytes=_vmem_limit(blocks, 0, _nbytes((d, tn), BF16) + 2**20)),
        name="adaln_table",
    )(cvec, ada_w, ada_b.reshape(depth, 1, n))


def _head_inv_rms(t):
    ss = jnp.sum(t * t, axis=-1, keepdims=True)
    return lax.rsqrt(ss * (1.0 / QK_DIM) + EPS)


def _in_kernel(x_ref, mod_ref, g1_ref, win_ref, cs_ref, qg_ref, wq_ref, kvg_ref, wkv_ref,
               gq_ref, gk_ref, rope_ref,
               fa_ref, ug_ref, q_ref, k_ref, v_ref, ckv_ref, kr_ref, *, d):
    sh1 = mod_ref[:, 0:d]
    sc1 = mod_ref[:, d:2 * d]
    h = _rms(x_ref[...]) * g1_ref[...] * (1.0 + sc1) + sh1
    hb = h.astype(BF16)

    c0 = 0
    fb = _dot(hb, win_ref[:, c0:c0 + F_W]).astype(BF16)
    for g in range(F_GROUPS):
        a = _dot(fb[:, g * F_GROUP_W:(g + 1) * F_GROUP_W], cs_ref[...])
        fa_ref[:, g * F_GROUP_W:(g + 1) * F_GROUP_W] = a[:, :F_GROUP_W].astype(BF16)
        fa_ref[:, F_W + g * F_GROUP_W:F_W + (g + 1) * F_GROUP_W] = a[:, F_GROUP_W:].astype(BF16)
    c0 += F_W

    ga = _dot(hb, win_ref[:, c0:c0 + CONV_W])
    gg = _dot(hb, win_ref[:, c0 + CONV_W:c0 + 2 * CONV_W])
    ug_ref[...] = ga * _sigmoid(gg)
    c0 += 2 * CONV_W

    cos = rope_ref[:, 0:HEAD_TILE]
    sin = rope_ref[:, HEAD_TILE:2 * HEAD_TILE]

    qn = (_rms(_dot(hb, win_ref[:, c0:c0 + Q_LORA])) * qg_ref[...]).astype(BF16)
    c0 += Q_LORA
    scale = 1.0 / math.sqrt(QK_DIM)
    q_cos = cos * (gq_ref[0:1, :] * scale)
    q_sin = sin * (gq_ref[1:2, :] * scale)
    for hd in range(N_HEADS):
        sl = slice(hd * HEAD_TILE, (hd + 1) * HEAD_TILE)
        t = _dot(qn, wq_ref[:, sl])
        tp = _dot(qn, wq_ref[:, HW + hd * HEAD_TILE:HW + (hd + 1) * HEAD_TILE])
        q_ref[:, sl] = ((t * q_cos + tp * q_sin) * _head_inv_rms(t)).astype(BF16)

    ckv = _rms(_dot(hb, win_ref[:, c0:c0 + KV_LORA])) * kvg_ref[...]
    ckv_ref[...] = ckv
    c0 += KV_LORA
    kr = _dot(hb, win_ref[:, c0:c0 + HEAD_TILE])
    kr_ref[...] = kr
    k_cos = cos * gk_ref[0:1, :]
    kr_rot = _dot(hb, win_ref[:, c0 + HEAD_TILE:c0 + 2 * HEAD_TILE]) * (sin * gk_ref[1:2, :])
    cb = ckv.astype(BF16)
    for hd in range(N_HEADS):
        sl = slice(hd * HEAD_TILE, (hd + 1) * HEAD_TILE)
        t = _dot(cb, wkv_ref[:, sl]) + kr
        k_ref[:, sl] = ((t * k_cos + kr_rot) * _head_inv_rms(t)).astype(BF16)
        vs = slice(HW + hd * HEAD_TILE, HW + (hd + 1) * HEAD_TILE)
        v_ref[:, sl] = _dot(cb, wkv_ref[:, vs]).astype(BF16)


def _in_proj(x, mod, l, P, rope_tab, *, n_p, per_seq):
    t_rows, d = x.shape
    n_tiles = t_rows // TM
    win_cols = P["w_in"].shape[-1]

    def mod_idx(i):
        return (l, jnp.where(i < n_p, 0, 1 + (i - n_p) // per_seq), 0, 0)

    def rope_idx(i):
        return (jnp.where(i < n_p, per_seq, (i - n_p) % per_seq), 0)

    row = lambda w: pl.BlockSpec((TM, w), lambda i: (i, 0))
    lay = lambda *s: pl.BlockSpec((None,) + s, lambda i: (l,) + (0,) * len(s))
    out_shapes = [
        jax.ShapeDtypeStruct((t_rows, 2 * F_W), BF16),
        jax.ShapeDtypeStruct((t_rows, CONV_W), F32),
        jax.ShapeDtypeStruct((t_rows, HW), BF16),
        jax.ShapeDtypeStruct((t_rows, HW), BF16),
        jax.ShapeDtypeStruct((t_rows, HW), BF16),
        jax.ShapeDtypeStruct((t_rows, KV_LORA), F32),
        jax.ShapeDtypeStruct((t_rows, HEAD_TILE), F32),
    ]
    blocks = (_nbytes((TM, d), F32) + _nbytes((d, win_cols), BF16) + _nbytes((Q_LORA, 2 * HW), BF16)
              + _nbytes((KV_LORA, 2 * HW), BF16) + _nbytes((TM, 2 * HEAD_TILE), F32)
              + sum(_nbytes((TM, s.shape[1]), s.dtype) for s in out_shapes))
    return pl.pallas_call(
        functools.partial(_in_kernel, d=d),
        grid=(n_tiles,),
        in_specs=[
            row(d),
            pl.BlockSpec((None, None, 1, N_MOD * d), mod_idx),
            lay(1, d),
            lay(d, win_cols),
            pl.BlockSpec((F_GROUP_W, 2 * F_GROUP_W), lambda i: (0, 0)),
            lay(1, Q_LORA),
            lay(Q_LORA, 2 * HW),
            lay(1, KV_LORA),
            lay(KV_LORA, 2 * HW),
            lay(2, HEAD_TILE),
            lay(2, HEAD_TILE),
            pl.BlockSpec((TM, 2 * HEAD_TILE), rope_idx),
        ],
        out_specs=[row(s.shape[1]) for s in out_shapes],
        out_shape=out_shapes,
        compiler_params=pltpu.CompilerParams(
            dimension_semantics=("arbitrary",),
            vmem_limit_bytes=_vmem_limit(blocks, 0, 12 * _nbytes((TM, d), F32))),
        name="in_proj",
    )(x, mod, P["norm1_g"], P["w_in"], P["dft_cs"], P["q_norm_g"], P["w_q"], P["kv_norm_g"],
      P["w_kv"], P["gq"], P["gk"], rope_tab)


def _cache_kv_kernel(ckv_ref, kr_ref, wkv_ref, gk_ref, k_ref, v_ref):
    cb = ckv_ref[...].astype(BF16)
    kr = kr_ref[...]
    k_gain = gk_ref[0:1, :]
    for hd in range(N_HEADS):
        sl = slice(hd * HEAD_TILE, (hd + 1) * HEAD_TILE)
        t = _dot(cb, wkv_ref[:, sl]) + kr
        k_ref[:, sl] = (t * k_gain * _head_inv_rms(t)).astype(BF16)
        vs = slice(HW + hd * HEAD_TILE, HW + (hd + 1) * HEAD_TILE)
        v_ref[:, sl] = _dot(cb, wkv_ref[:, vs]).astype(BF16)


def _cache_kv(cache_ckv, cache_kr_tile, P):
    nb, depth, past, _ = cache_ckv.shape
    blk = lambda w: pl.BlockSpec((None, None, past, w), lambda b, l: (b, l, 0, 0))
    lay = lambda *s: pl.BlockSpec((None,) + s, lambda b, l: (l,) + (0,) * len(s))
    blocks = (_nbytes((past, KV_LORA + HEAD_TILE), F32) + _nbytes((KV_LORA, 2 * HW), BF16)
              + 2 * _nbytes((past, HW), BF16))
    return pl.pallas_call(
        _cache_kv_kernel,
        grid=(nb, depth),
        in_specs=[blk(KV_LORA), blk(HEAD_TILE), lay(KV_LORA, 2 * HW), lay(2, HEAD_TILE)],
        out_specs=[blk(HW), blk(HW)],
        out_shape=[jax.ShapeDtypeStruct((nb, depth, past, HW), BF16)] * 2,
        compiler_params=pltpu.CompilerParams(
            dimension_semantics=("arbitrary", "arbitrary"),
            vmem_limit_bytes=_vmem_limit(blocks, 0, 8 * _nbytes((past, HW), F32))),
        name="cache_kv",
    )(cache_ckv, cache_kr_tile, P["w_kv"], P["gk"])


def _fourier_kernel(fa_ref, c_ref, s_ref, o_ref, *, seq, n_seq):
    for s in range(n_seq):
        rows = slice(s * seq, (s + 1) * seq)
        y = _dot(c_ref[...], fa_ref[rows, 0:F_W]) - _dot(s_ref[...], fa_ref[rows, F_W:2 * F_W])
        o_ref[rows, :] = y.astype(BF16)


def _fourier_prompt(fa, dft_c, dft_s, *, t_p, seq):
    n_seq = TM // seq
    blocks = _nbytes((TM, 2 * F_W), BF16) + 2 * _nbytes((seq, seq), BF16) + _nbytes((TM, F_W), BF16)
    return pl.pallas_call(
        functools.partial(_fourier_kernel, seq=seq, n_seq=n_seq),
        grid=(t_p // TM,),
        in_specs=[
            pl.BlockSpec((TM, 2 * F_W), lambda i: (i, 0)),
            pl.BlockSpec((seq, seq), lambda i: (0, 0)),
            pl.BlockSpec((seq, seq), lambda i: (0, 0)),
        ],
        out_specs=pl.BlockSpec((TM, F_W), lambda i: (i, 0)),
        out_shape=jax.ShapeDtypeStruct((t_p, F_W), BF16),
        compiler_params=pltpu.CompilerParams(
            dimension_semantics=("arbitrary",),
            vmem_limit_bytes=_vmem_limit(blocks, 0, 4 * _nbytes((TM, F_W), F32))),
        name="fourier_prompt",
    )(fa, dft_c, dft_s)


def _fourier_sample_kernel(fa_ref, c_ref, s_ref, o_ref):
    y = _dot(c_ref[...], fa_ref[:, 0:F_W]) - _dot(s_ref[...], fa_ref[:, F_W:2 * F_W])
    o_ref[...] = y.astype(BF16)


def _fourier_sample(fa, dft_c, dft_s, *, t_p, n_b, seq):
    tr = TM
    per = seq // tr
    first = t_p // seq
    blocks = _nbytes((seq, 2 * F_W), BF16) + 2 * _nbytes((tr, seq), BF16) + _nbytes((tr, F_W), BF16)
    return pl.pallas_call(
        _fourier_sample_kernel,
        grid=(n_b, per),
        in_specs=[
            pl.BlockSpec((seq, 2 * F_W), lambda b, j: (first + b, 0)),
            pl.BlockSpec((tr, seq), lambda b, j: (j, 0)),
            pl.BlockSpec((tr, seq), lambda b, j: (j, 0)),
        ],
        out_specs=pl.BlockSpec((tr, F_W), lambda b, j: (b * per + j, 0)),
        out_shape=jax.ShapeDtypeStruct((n_b * seq, F_W), BF16),
        compiler_params=pltpu.CompilerParams(
            dimension_semantics=("arbitrary", "arbitrary"),
            vmem_limit_bytes=_vmem_limit(blocks, 0, 4 * _nbytes((tr, F_W), F32))),
        name="fourier_sample",
    )(fa, dft_c, dft_s)


def _conv_kernel(u_ref, dw_ref, dwb_ref, lng_ref, lnb_ref, o_ref, pad_ref, *, seq):
    zeros = jnp.zeros((CONV_PAD, CONV_W), F32)
    pad_ref[0:CONV_PAD, :] = zeros
    pad_ref[CONV_PAD + seq:2 * CONV_PAD + seq, :] = zeros
    pad_ref[CONV_PAD:CONV_PAD + seq, :] = u_ref[...]
    half = CONV_K // 2

    def body(rb, carry):
        r0 = pl.multiple_of(rb * CONV_RB, CONV_RB)
        win = pad_ref[pl.ds(r0, CONV_RB + 2 * CONV_PAD), :]
        acc = jnp.zeros((CONV_RB, CONV_W), F32)
        for phase in range(SUBLANES):
            shifted = win[phase:phase + CONV_RB + 2 * CONV_PAD - SUBLANES, :]
            for k in range(CONV_K):
                off = CONV_PAD - half + k
                if off % SUBLANES == phase:
                    base = off - phase
                    acc = acc + dw_ref[k:k + 1, :] * shifted[base:base + CONV_RB, :]
        acc = acc + dwb_ref[...]
        mu = jnp.mean(acc, axis=-1, keepdims=True)
        cen = acc - mu
        var = jnp.mean(cen * cen, axis=-1, keepdims=True)
        y = cen * lax.rsqrt(var + EPS) * lng_ref[...] + lnb_ref[...]
        o_ref[pl.ds(r0, CONV_RB), :] = _silu(y).astype(BF16)
        return carry

    lax.fori_loop(0, seq // CONV_RB, body, 0)


def _conv_module(ug, l, P, *, first, n_seq, seq):
    lay = lambda *s: pl.BlockSpec((None,) + s, lambda i: (l,) + (0,) * len(s))
    blocks = _nbytes((seq, CONV_W), F32) + _nbytes((seq, CONV_W), BF16) + _nbytes((CONV_K + 4, CONV_W), F32)
    scratch = _nbytes((seq + 2 * CONV_PAD, CONV_W), F32)
    return pl.pallas_call(
        functools.partial(_conv_kernel, seq=seq),
        grid=(n_seq,),
        in_specs=[
            pl.BlockSpec((seq, CONV_W), lambda i: (first + i, 0)),
            lay(CONV_K, CONV_W), lay(1, CONV_W), lay(1, CONV_W), lay(1, CONV_W),
        ],
        out_specs=pl.BlockSpec((seq, CONV_W), lambda i: (i, 0)),
        out_shape=jax.ShapeDtypeStruct((n_seq * seq, CONV_W), BF16),
        scratch_shapes=[pltpu.VMEM((seq + 2 * CONV_PAD, CONV_W), F32)],
        compiler_params=pltpu.CompilerParams(
            dimension_semantics=("arbitrary",),
            vmem_limit_bytes=_vmem_limit(blocks, scratch, 4 * 2**20)),
        name=f"conv_seq{seq}",
    )(ug, P["conv_dw"], P["conv_dw_b"], P["conv_ln_g"], P["conv_ln_b"])


def _attn_kernel(*refs, with_cache):
    if with_cache:
        q_ref, k_ref, v_ref, kc_ref, vc_ref, o_ref = refs
    else:
        q_ref, k_ref, v_ref, o_ref = refs
    tq = q_ref.shape[0]
    lane = lax.broadcasted_iota(jnp.int32, (tq, HEAD_TILE), 1)
    for hp in range(N_HEADS // 2):
        acc = None
        inv = []
        for hh in range(2):
            sl = slice((2 * hp + hh) * HEAD_TILE, (2 * hp + hh + 1) * HEAD_TILE)
            qh = q_ref[:, sl]
            s_m = _dot_nt(qh, k_ref[:, sl])
            m = jnp.max(s_m, axis=-1, keepdims=True)
            if with_cache:
                s_c = _dot_nt(qh, kc_ref[:, sl])
                m = jnp.maximum(m, jnp.max(s_c, axis=-1, keepdims=True))
            e_m = jnp.exp(s_m - m)
            den = jnp.sum(e_m, axis=-1, keepdims=True)
            pv = _dot(e_m.astype(BF16), v_ref[:, sl])
            if with_cache:
                e_c = jnp.exp(s_c - m)
                den = den + jnp.sum(e_c, axis=-1, keepdims=True)
                pv = pv + _dot(e_c.astype(BF16), vc_ref[:, sl])
            acc = pv if acc is None else acc + pv
            inv.append(1.0 / den)
        scale = jnp.where(lane < V_DIM, inv[0], inv[1])
        o_ref[:, hp * HEAD_TILE:(hp + 1) * HEAD_TILE] = (acc * scale).astype(BF16)


def _attn_prompt(q, k, v, *, n_seq, seq):
    blk = pl.BlockSpec((seq, HW), lambda i: (i, 0))
    blocks = 3 * _nbytes((seq, HW), BF16) + _nbytes((seq, N_HEADS * V_DIM), BF16)
    return pl.pallas_call(
        functools.partial(_attn_kernel, with_cache=False),
        grid=(n_seq,),
        in_specs=[blk, blk, blk],
        out_specs=pl.BlockSpec((seq, N_HEADS * V_DIM), lambda i: (i, 0)),
        out_shape=jax.ShapeDtypeStruct((n_seq * seq, N_HEADS * V_DIM), BF16),
        compiler_params=pltpu.CompilerParams(
            dimension_semantics=("arbitrary",),
            vmem_limit_bytes=_vmem_limit(blocks, 0, 8 * _nbytes((seq, seq), F32) + 2**21)),
        name="attn_prompt",
    )(q, k, v)


def _attn_sample(q, k, v, kc, vc, l, *, t_p, n_b, seq, tq):
    per = seq // tq
    first_q = t_p // tq
    first_s = t_p // seq
    past = kc.shape[2]
    kv_blk = pl.BlockSpec((seq, HW), lambda b, j: (first_s + b, 0))
    c_blk = pl.BlockSpec((None, None, past, HW), lambda b, j: (b, l, 0, 0))
    blocks = (_nbytes((tq, HW), BF16) + 2 * _nbytes((seq + past, HW), BF16)
              + _nbytes((tq, N_HEADS * V_DIM), BF16))
    return pl.pallas_call(
        functools.partial(_attn_kernel, with_cache=True),
        grid=(n_b, per),
        in_specs=[pl.BlockSpec((tq, HW), lambda b, j: (first_q + b * per + j, 0)),
                  kv_blk, kv_blk, c_blk, c_blk],
        out_specs=pl.BlockSpec((tq, N_HEADS * V_DIM), lambda b, j: (b * per + j, 0)),
        out_shape=jax.ShapeDtypeStruct((n_b * seq, N_HEADS * V_DIM), BF16),
        compiler_params=pltpu.CompilerParams(
            dimension_semantics=("arbitrary", "arbitrary"),
            vmem_limit_bytes=_vmem_limit(blocks, 0, 6 * _nbytes((tq, seq + past), F32))),
        name="attn_sample",
    )(q, k, v, kc, vc)


def _merge_kernel(x_ref, mod_ref, g1_ref, fp_ref, fs_ref, cp_ref, cs_ref, ap_ref, as_ref,
                  wg_ref, bg_ref, wf_ref, wc_ref, wa_ref, wo_ref, o_ref, *, d, n_p):
    is_prompt = pl.program_id(0) < n_p
    x = x_ref[...]
    sh1 = mod_ref[:, 0:d]
    sc1 = mod_ref[:, d:2 * d]
    g1 = mod_ref[:, 2 * d:3 * d]
    hb = (_rms(x) * g1_ref[...] * (1.0 + sc1) + sh1).astype(BF16)
    merged = None
    branches = ((fp_ref, fs_ref, wf_ref), (cp_ref, cs_ref, wc_ref), (ap_ref, as_ref, wa_ref))
    for b, (yp_ref, ys_ref, w_ref) in enumerate(branches):
        y_in = jnp.where(is_prompt, yp_ref[...], ys_ref[...])
        y = _dot(y_in, w_ref[...])
        gate = _sigmoid(_dot(hb, wg_ref[:, b * d:(b + 1) * d]) + bg_ref[:, b * d:(b + 1) * d])
        merged = gate * y if merged is None else merged + gate * y
    o_ref[...] = x + g1 * _dot(merged.astype(BF16), wo_ref[...])


def _merge(x, mod, l, P, yf, yc, ya, *, n_p, per_seq):
    t_rows, d = x.shape
    n_tiles = t_rows // TM
    n_s = n_tiles - n_p

    def mod_idx(i):
        return (l, jnp.where(i < n_p, 0, 1 + (i - n_p) // per_seq), 0, 0)

    row = pl.BlockSpec((TM, d), lambda i: (i, 0))
    lay = lambda *s: pl.BlockSpec((None,) + s, lambda i: (l,) + (0,) * len(s))
    p_blk = pl.BlockSpec((TM, F_W), lambda i: (jnp.minimum(i, n_p - 1), 0))
    s_blk = pl.BlockSpec((TM, F_W), lambda i: (jnp.clip(i - n_p, 0, n_s - 1), 0))
    blocks = (2 * _nbytes((TM, d), F32) + 6 * _nbytes((TM, F_W), BF16) + _nbytes((d, 3 * d), BF16)
              + 3 * _nbytes((F_W, d), BF16) + _nbytes((d, d), BF16))
    return pl.pallas_call(
        functools.partial(_merge_kernel, d=d, n_p=n_p),
        grid=(n_tiles,),
        in_specs=[row, pl.BlockSpec((None, None, 1, N_MOD * d), mod_idx), lay(1, d),
                  p_blk, s_blk, p_blk, s_blk, p_blk, s_blk,
                  lay(d, 3 * d), lay(1, 3 * d), lay(F_W, d), lay(CONV_W, d), lay(N_HEADS * V_DIM, d),
                  lay(d, d)],
        out_specs=row,
        out_shape=jax.ShapeDtypeStruct((t_rows, d), F32),
        compiler_params=pltpu.CompilerParams(
            dimension_semantics=("arbitrary",),
            vmem_limit_bytes=_vmem_limit(blocks, 0, 8 * _nbytes((TM, d), F32))),
        name="merge",
    )(x, mod, P["norm1_g"], yf[0], yf[1], yc[0], yc[1], ya[0], ya[1],
      P["w_gate"], P["b_gate"], P["w_fourier"], P["w_conv_out"], P["w_mla_out"], P["w_out"])


def _ffn_kernel(x_ref, mod_ref, g2_ref, ua_ref, ub_ref, dwa_ref, dwb_ref, ba_ref, bb_ref, dn_ref,
                o_ref, h_ref, *, d, rows, n_p, seq_p, seq_s):
    i = pl.program_id(0)
    f = pl.program_id(1)
    n_sub = rows // FFN_SUB

    @pl.when(f == 0)
    def _():
        sh2 = mod_ref[:, 3 * d:4 * d]
        sc2 = mod_ref[:, 4 * d:5 * d]
        zeros = jnp.zeros((FFN_PAD, d), BF16)
        h_ref[0:FFN_PAD, :] = zeros
        h_ref[FFN_PAD + rows:2 * FFN_PAD + rows, :] = zeros
        for sb in range(n_sub):
            r = slice(sb * FFN_SUB, (sb + 1) * FFN_SUB)
            h2 = _rms(x_ref[r, :]) * g2_ref[...] * (1.0 + sc2) + sh2
            h_ref[FFN_PAD + sb * FFN_SUB:FFN_PAD + (sb + 1) * FFN_SUB, :] = h2.astype(BF16)
            o_ref[r, :] = jnp.zeros((FFN_SUB, d), F32)

    period_mask = jnp.where(i < n_p, seq_p - 1, seq_s - 1)
    ext = FFN_SUB + 2 * FFN_PAD

    def up_proj(sb):
        hx = h_ref[sb * FFN_SUB:sb * FFN_SUB + ext, :]
        return _dot(hx, ua_ref[...]), _dot(hx, ub_ref[...])

    nxt = up_proj(0)
    for sb in range(n_sub):
        u_a, u_b = nxt
        if sb + 1 < n_sub:
            nxt = up_proj(sb + 1)
        pos = (sb * FFN_SUB + lax.broadcasted_iota(jnp.int32, (FFN_SUB, 1), 0)) & period_mask
        is_first = pos == 0
        is_last = pos == period_mask

        def conv(u, dw_ref, b_ref):
            prev = jnp.where(is_first, 0.0, u[FFN_PAD - 1:FFN_PAD - 1 + FFN_SUB, :])
            mid = u[FFN_PAD:FFN_PAD + FFN_SUB, :]
            nxt = jnp.where(is_last, 0.0, u[FFN_PAD + 1:FFN_PAD + 1 + FFN_SUB, :])
            return dw_ref[0:1, :] * prev + dw_ref[1:2, :] * mid + dw_ref[2:3, :] * nxt + b_ref[...]

        ca = conv(u_a, dwa_ref, ba_ref)
        cb = conv(u_b, dwb_ref, bb_ref)
        act = (_silu(ca) * cb).astype(BF16)
        r = slice(sb * FFN_SUB, (sb + 1) * FFN_SUB)
        o_ref[r, :] += _dot(act, dn_ref[...])

    @pl.when(f == pl.num_programs(1) - 1)
    def _():
        g2 = mod_ref[:, 5 * d:6 * d]
        for sb in range(n_sub):
            r = slice(sb * FFN_SUB, (sb + 1) * FFN_SUB)
            o_ref[r, :] = x_ref[r, :] + g2 * o_ref[r, :]


def _ffn(x, mod, l, P, *, rows, n_p, seq_p, seq_s):
    t_rows, d = x.shape
    d_ff = P["ffn_down"].shape[1]
    n_ck = d_ff // FFN_CK
    n_tiles = t_rows // rows

    def mod_idx(i, f):
        return (l, jnp.where(i < n_p, 0, 1 + (i - n_p)), 0, 0)

    row = pl.BlockSpec((rows, d), lambda i, f: (i, 0))
    blocks = (2 * _nbytes((rows, d), F32) + 2 * _nbytes((d, FFN_CK), BF16)
              + _nbytes((FFN_CK, d), BF16) + 4 * _nbytes((8, FFN_CK), F32))
    scratch = _nbytes((rows + 2 * FFN_PAD, d), BF16)
    return pl.pallas_call(
        functools.partial(_ffn_kernel, d=d, rows=rows, n_p=n_p, seq_p=seq_p, seq_s=seq_s),
        grid=(n_tiles, n_ck),
        in_specs=[
            row,
            pl.BlockSpec((None, None, 1, N_MOD * d), mod_idx),
            pl.BlockSpec((None, 1, d), lambda i, f: (l, 0, 0)),
            pl.BlockSpec((None, d, FFN_CK), lambda i, f: (l, 0, f)),
            pl.BlockSpec((None, d, FFN_CK), lambda i, f: (l, 0, n_ck + f)),
            pl.BlockSpec((None, FFN_K, FFN_CK), lambda i, f: (l, 0, f)),
            pl.BlockSpec((None, FFN_K, FFN_CK), lambda i, f: (l, 0, n_ck + f)),
            pl.BlockSpec((None, 1, FFN_CK), lambda i, f: (l, 0, f)),
            pl.BlockSpec((None, 1, FFN_CK), lambda i, f: (l, 0, n_ck + f)),
            pl.BlockSpec((None, FFN_CK, d), lambda i, f: (l, f, 0)),
        ],
        out_specs=row,
        out_shape=jax.ShapeDtypeStruct((t_rows, d), F32),
        scratch_shapes=[pltpu.VMEM((rows + 2 * FFN_PAD, d), BF16)],
        compiler_params=pltpu.CompilerParams(
            dimension_semantics=("arbitrary", "arbitrary"),
            vmem_limit_bytes=_vmem_limit(blocks, scratch, 10 * _nbytes((FFN_SUB + 2 * FFN_PAD, 2 * FFN_CK), F32))),
        name="conv_ffn",
    )(x, mod, P["norm2_g"], P["ffn_up"], P["ffn_up"], P["ffn_dw"], P["ffn_dw"],
      P["ffn_dw_b"], P["ffn_dw_b"], P["ffn_down"])


def _dft_tables(n, scale):
    j = np.arange(n, dtype=np.int64)
    ang = ((j[:, None] * j[None, :]) % n).astype(np.float64) * (2.0 * math.pi / n)
    return (np.cos(ang) * scale).astype(np.float32), (np.sin(ang) * scale).astype(np.float32)


def _rope_partner():
    lane = np.arange(HEAD_TILE)
    dim = lane - QK_NOPE
    in_rope = (dim >= 0) & (dim < QK_ROPE)
    quarter = QK_ROPE // 4
    first_half = (dim % (2 * quarter)) // quarter == 0
    perm = np.where(in_rope, np.where(first_half, lane + quarter, lane - quarter), lane)
    return perm, in_rope.astype(np.float32)


def _rope_table(seq, ident_rows):
    rows = seq // GRID_W
    row = np.repeat(np.arange(rows), GRID_W).astype(np.float32)
    col = np.tile(np.arange(GRID_W), rows).astype(np.float32)
    half = QK_ROPE // 2
    inv = (np.float32(ROPE_BASE) ** (-np.arange(0, half, 2, dtype=np.float32) / half)).astype(np.float32)
    ang = np.stack([row[:, None] * inv, col[:, None] * inv], axis=1)
    cos, sin = np.cos(ang), np.sin(ang)
    cos_r = np.stack([cos, cos], axis=2).reshape(seq, QK_ROPE)
    sin_r = np.stack([-sin, sin], axis=2).reshape(seq, QK_ROPE)

    def tile(mid, fill):
        left = np.full((seq, QK_NOPE), fill, np.float32)
        right = np.full((seq, HEAD_TILE - QK_DIM), fill, np.float32)
        return np.concatenate([left, mid, right], axis=1)

    tab = np.concatenate([tile(cos_r, 1.0), tile(sin_r, 0.0)], axis=1)
    ident = np.concatenate([np.ones((ident_rows, HEAD_TILE), np.float32),
                            np.zeros((ident_rows, HEAD_TILE), np.float32)], axis=1)
    return np.concatenate([tab, ident], axis=0).astype(np.float32)


def _prepare_params(W):
    depth, d, _ = W["w_in"].shape
    P = {}
    perm, rope_mask = _rope_partner()
    split = F_W + 2 * CONV_W + Q_LORA + KV_LORA
    w_in = W["w_in"]
    kr_tile = jnp.concatenate([jnp.zeros((depth, d, QK_NOPE), F32), w_in[:, :, split:],
                               jnp.zeros((depth, d, HEAD_TILE - QK_DIM), F32)], axis=2)
    P["w_in"] = jnp.concatenate([w_in[:, :, :split], kr_tile, kr_tile[:, :, perm] * rope_mask],
                                axis=2).astype(BF16)
    wq = W["w_q_up"].reshape(depth, Q_LORA, N_HEADS, QK_DIM)
    wq = jnp.pad(wq, ((0, 0), (0, 0), (0, 0), (0, HEAD_TILE - QK_DIM)))
    wq_rot = wq[..., perm] * rope_mask
    P["w_q"] = jnp.concatenate([wq.reshape(depth, Q_LORA, HW), wq_rot.reshape(depth, Q_LORA, HW)],
                               axis=2).astype(BF16)
    wkv = W["w_kv_up"].reshape(depth, KV_LORA, N_HEADS, QK_NOPE + V_DIM)
    zk = jnp.zeros((depth, KV_LORA, N_HEADS, HEAD_TILE - QK_NOPE), F32)
    wk = jnp.concatenate([wkv[..., :QK_NOPE], zk], axis=-1)
    zv = jnp.zeros((depth, KV_LORA, N_HEADS, HEAD_TILE - V_DIM), F32)
    wv_even = jnp.concatenate([wkv[..., QK_NOPE:], zv], axis=-1)
    wv_odd = jnp.concatenate([zv, wkv[..., QK_NOPE:]], axis=-1)
    odd = (jnp.arange(N_HEADS) % 2 == 1)[None, None, :, None]
    wv = jnp.where(odd, wv_odd, wv_even)
    P["w_kv"] = jnp.concatenate([wk.reshape(depth, KV_LORA, HW), wv.reshape(depth, KV_LORA, HW)],
                                axis=2).astype(BF16)
    def head_gains(g):
        g = jnp.pad(g, ((0, 0), (0, HEAD_TILE - QK_DIM)))
        return jnp.stack([g, g[:, perm] * rope_mask], axis=1)

    P["gq"] = head_gains(W["qk_q_g"])
    P["gk"] = head_gains(W["qk_k_g"])
    vec = lambda a: a.reshape(depth, 1, a.shape[-1])
    for name in ("norm1_g", "norm2_g", "q_norm_g", "kv_norm_g", "b_gate", "conv_dw_b", "conv_ln_g",
                 "conv_ln_b", "ffn_dw_b"):
        P[name] = vec(W[name])
    P["conv_dw"] = W["conv_dw"]
    P["ffn_dw"] = W["ffn_dw"]
    for name in ("w_gate", "w_fourier", "w_conv_out", "w_mla_out", "w_out", "ffn_up", "ffn_down"):
        P[name] = W[name].astype(BF16)
    return P


def kernel(x_prompt, x_sample, cache_ckv, cache_krope, c, c_ctx, ada_w, ada_b, norm1_g, norm2_g, w_in, w_gate, b_gate, w_fourier, conv_dw, conv_dw_b, conv_ln_g, conv_ln_b, w_conv_out, q_norm_g, w_q_up, kv_norm_g, w_kv_up, qk_q_g, qk_k_g, w_mla_out, w_out, ffn_up, ffn_dw, ffn_dw_b, ffn_down):
    W = dict(norm1_g=norm1_g, norm2_g=norm2_g, w_in=w_in, w_gate=w_gate, b_gate=b_gate,
             w_fourier=w_fourier, conv_dw=conv_dw, conv_dw_b=conv_dw_b, conv_ln_g=conv_ln_g,
             conv_ln_b=conv_ln_b, w_conv_out=w_conv_out, q_norm_g=q_norm_g, w_q_up=w_q_up,
             kv_norm_g=kv_norm_g, w_kv_up=w_kv_up, qk_q_g=qk_q_g, qk_k_g=qk_k_g,
             w_mla_out=w_mla_out, w_out=w_out, ffn_up=ffn_up, ffn_dw=ffn_dw, ffn_dw_b=ffn_dw_b,
             ffn_down=ffn_down)
    n_bp, seq_p, d = x_prompt.shape
    n_bs, seq_s, _ = x_sample.shape
    depth = ada_w.shape[0]
    t_p = n_bp * seq_p
    t_s = n_bs * seq_s
    assert t_p % seq_s == 0 and seq_s % TM == 0 and TM % seq_p == 0 and seq_s % FFN_SUB == 0
    assert seq_p & (seq_p - 1) == 0 and seq_s & (seq_s - 1) == 0 and n_bs + 1 <= 8
    n_p = t_p // TM
    per_seq = seq_s // TM

    P = _prepare_params(W)
    c_cs, s_cs = _dft_tables(F_GROUP_W, 1.0 / math.sqrt(F_GROUP_W))
    P["dft_cs"] = jnp.asarray(np.concatenate([c_cs, s_cs], axis=1), dtype=BF16)
    dft_p = [jnp.asarray(t, dtype=BF16) for t in _dft_tables(seq_p, 1.0 / math.sqrt(seq_p))]
    dft_s = [jnp.asarray(t, dtype=BF16) for t in _dft_tables(seq_s, 1.0 / math.sqrt(seq_s))]
    rope_tab = jnp.asarray(_rope_table(seq_s, TM))

    cvec = jnp.concatenate([c_ctx[None, :], c, jnp.zeros((8 - 1 - n_bs, d), F32)], axis=0)
    mod = _mod_table(cvec, ada_w, ada_b).reshape(depth, 8, 1, N_MOD * d)

    kr_tile = jnp.pad(cache_krope, ((0, 0), (0, 0), (0, 0), (QK_NOPE, HEAD_TILE - QK_DIM)))
    kc, vc = _cache_kv(cache_ckv, kr_tile, P)

    x = jnp.concatenate([x_prompt.reshape(t_p, d), x_sample.reshape(t_s, d)], axis=0)
    ckv_out, kr_out = [], []
    for l in range(depth):
        fa, ug, q, k, v, ckv, kr = _in_proj(x, mod, l, P, rope_tab, n_p=n_p, per_seq=per_seq)
        ckv_out.append(ckv[:t_p].reshape(n_bp, seq_p, KV_LORA))
        kr_out.append(kr[:t_p, QK_NOPE:QK_DIM].reshape(n_bp, seq_p, QK_ROPE))
        yf = (_fourier_prompt(fa, *dft_p, t_p=t_p, seq=seq_p),
              _fourier_sample(fa, *dft_s, t_p=t_p, n_b=n_bs, seq=seq_s))
        yc = (_conv_module(ug, l, P, first=0, n_seq=n_bp, seq=seq_p),
              _conv_module(ug, l, P, first=t_p // seq_s, n_seq=n_bs, seq=seq_s))
        ya = (_attn_prompt(q, k, v, n_seq=n_bp, seq=seq_p),
              _attn_sample(q, k, v, kc, vc, l, t_p=t_p, n_b=n_bs, seq=seq_s, tq=256))
        x = _merge(x, mod, l, P, yf, yc, ya, n_p=n_p, per_seq=per_seq)
        x = _ffn(x, mod, l, P, rows=seq_s, n_p=t_p // seq_s, seq_p=seq_p, seq_s=seq_s)
    y_prompt = x[:t_p].reshape(n_bp, seq_p, d)
    y_sample = x[t_p:].reshape(n_bs, seq_s, d)
    return (y_prompt, y_sample, jnp.stack(ckv_out, axis=1), jnp.stack(kr_out, axis=1))
```

```python
import functools
import math

import numpy as np
import jax
import jax.numpy as jnp
from jax import lax
from jax.experimental import pallas as pl
from jax.experimental.pallas import tpu as pltpu

F32 = jnp.float32
BF16 = jnp.bfloat16

EPS = 1e-6
N_HEADS = 8
QK_NOPE = 64
QK_ROPE = 32
V_DIM = 64
QK_DIM = QK_NOPE + QK_ROPE
F_GROUPS = 4
F_GROUP_W = 128
F_W = F_GROUPS * F_GROUP_W
CONV_W = 512
CONV_K = 31
Q_LORA = 384
KV_LORA = 256
GRID_W = 64
ROPE_BASE = 10000.0
FFN_K = 3
N_MOD = 6

LANES = 128
SUBLANES = 8
V7X_VMEM_BYTES = 64 * 2**20
VMEM_RESERVE_BYTES = 6 * 2**20

HEAD_TILE = LANES
HW = N_HEADS * HEAD_TILE
CONV_PAD = 16

TM = 512
ATTN_TQ = 512
CONV_MAX_BLOCK_GROUPS = 16
FFN_SUB = 512
FFN_WIN = FFN_SUB + 2 * SUBLANES
FFN_CK = 256


def _vmem_limit(block_bytes, scratch_bytes, temp_bytes):
    need = 2 * block_bytes + scratch_bytes + temp_bytes
    return int(min(need, V7X_VMEM_BYTES - VMEM_RESERVE_BYTES))


def _nbytes(shape, dtype):
    return int(np.prod(shape)) * jnp.dtype(dtype).itemsize


def _dot(a, b):
    return jnp.dot(a, b, preferred_element_type=F32)


def _dot_nt(a, b):
    return lax.dot_general(a, b, (((1,), (1,)), ((), ())), preferred_element_type=F32)


def _rms(x):
    return x * lax.rsqrt(jnp.mean(x * x, axis=-1, keepdims=True) + EPS)


def _sigmoid(x):
    return 1.0 / (1.0 + jnp.exp(-x))


def _silu(x):
    return x * _sigmoid(x)


def _mod_kernel(c_ref, w_ref, b_ref, o_ref):
    s = _silu(c_ref[...]).astype(BF16)
    o_ref[...] = _dot(s, w_ref[...].astype(BF16)) + b_ref[...]


def _mod_table(cvec, ada_w, ada_b):
    depth, d, n = ada_w.shape
    rows = cvec.shape[0]
    tn = n // 4
    blocks = _nbytes((d, tn), F32) + _nbytes((rows, tn), F32) * 2
    return pl.pallas_call(
        _mod_kernel,
        grid=(depth, n // tn),
        in_specs=[
            pl.BlockSpec((rows, d), lambda l, j: (0, 0)),
            pl.BlockSpec((None, d, tn), lambda l, j: (l, 0, j)),
            pl.BlockSpec((None, 1, tn), lambda l, j: (l, 0, j)),
        ],
        out_specs=pl.BlockSpec((None, rows, tn), lambda l, j: (l, 0, j)),
        out_shape=jax.ShapeDtypeStruct((depth, rows, n), F32),
        compiler_params=pltpu.CompilerParams(
            dimension_semantics=("arbitrary", "arbitrary"),
            vmem_limit_bytes=_vmem_limit(blocks, 0, _nbytes((d, tn), BF16) + 2**20)),
        name="adaln_table",
    )(cvec, ada_w, ada_b.reshape(depth, 1, n))


def _head_inv_rms(t):
    ss = jnp.sum(t * t, axis=-1, keepdims=True)
    return lax.rsqrt(ss * (1.0 / QK_DIM) + EPS)


def _in_kernel(x_ref, mod_ref, g1_ref, win_ref, cs_ref, qg_ref, wq_ref, kvg_ref, wkv_ref,
               gq_ref, gk_ref, rope_ref,
               fa_ref, ug_ref, q_ref, k_ref, v_ref, ckv_ref, kr_ref, *, d):
    sh1 = mod_ref[:, 0:d]
    sc1 = mod_ref[:, d:2 * d]
    h = _rms(x_ref[...]) * g1_ref[...] * (1.0 + sc1) + sh1
    hb = h.astype(BF16)

    c0 = 0
    fb = _dot(hb, win_ref[:, c0:c0 + F_W]).astype(BF16)
    for g in range(F_GROUPS):
        a = _dot(fb[:, g * F_GROUP_W:(g + 1) * F_GROUP_W], cs_ref[...])
        fa_ref[:, g * F_GROUP_W:(g + 1) * F_GROUP_W] = a[:, :F_GROUP_W].astype(BF16)
        fa_ref[:, F_W + g * F_GROUP_W:F_W + (g + 1) * F_GROUP_W] = a[:, F_GROUP_W:].astype(BF16)
    c0 += F_W

    ga = _dot(hb, win_ref[:, c0:c0 + CONV_W])
    gg = _dot(hb, win_ref[:, c0 + CONV_W:c0 + 2 * CONV_W])
    ug_ref[...] = ga * _sigmoid(gg)
    c0 += 2 * CONV_W

    cos = rope_ref[:, 0:HEAD_TILE]
    sin = rope_ref[:, HEAD_TILE:2 * HEAD_TILE]

    qn = (_rms(_dot(hb, win_ref[:, c0:c0 + Q_LORA])) * qg_ref[...]).astype(BF16)
    c0 += Q_LORA
    scale = 1.0 / math.sqrt(QK_DIM)
    q_cos = cos * (gq_ref[0:1, :] * scale)
    q_sin = sin * (gq_ref[1:2, :] * scale)
    for hd in range(N_HEADS):
        sl = slice(hd * HEAD_TILE, (hd + 1) * HEAD_TILE)
        t = _dot(qn, wq_ref[:, sl])
        tp = _dot(qn, wq_ref[:, HW + hd * HEAD_TILE:HW + (hd + 1) * HEAD_TILE])
        q_ref[:, sl] = ((t * q_cos + tp * q_sin) * _head_inv_rms(t)).astype(BF16)

    ckv = _rms(_dot(hb, win_ref[:, c0:c0 + KV_LORA])) * kvg_ref[...]
    ckv_ref[...] = ckv
    c0 += KV_LORA
    kr = _dot(hb, win_ref[:, c0:c0 + HEAD_TILE])
    kr_ref[...] = kr
    k_cos = cos * gk_ref[0:1, :]
    kr_rot = _dot(hb, win_ref[:, c0 + HEAD_TILE:c0 + 2 * HEAD_TILE]) * (sin * gk_ref[1:2, :])
    cb = ckv.astype(BF16)
    for hd in range(N_HEADS):
        sl = slice(hd * HEAD_TILE, (hd + 1) * HEAD_TILE)
        t = _dot(cb, wkv_ref[:, sl]) + kr
        k_ref[:, sl] = ((t * k_cos + kr_rot) * _head_inv_rms(t)).astype(BF16)
        vs = slice(HW + hd * HEAD_TILE, HW + (hd + 1) * HEAD_TILE)
        v_ref[:, sl] = _dot(cb, wkv_ref[:, vs]).astype(BF16)


def _in_proj(x, mod, l, P, rope_tab, *, n_p, per_seq):
    t_rows, d = x.shape
    n_tiles = t_rows // TM
    win_cols = P["w_in"].shape[-1]

    def mod_idx(i):
        return (l, jnp.where(i < n_p, 0, 1 + (i - n_p) // per_seq), 0, 0)

    def rope_idx(i):
        return (jnp.where(i < n_p, per_seq, (i - n_p) % per_seq), 0)

    row = lambda w: pl.BlockSpec((TM, w), lambda i: (i, 0))
    lay = lambda *s: pl.BlockSpec((None,) + s, lambda i: (l,) + (0,) * len(s))
    out_shapes = [
        jax.ShapeDtypeStruct((t_rows, 2 * F_W), BF16),
        jax.ShapeDtypeStruct((t_rows, CONV_W), F32),
        jax.ShapeDtypeStruct((t_rows, HW), BF16),
        jax.ShapeDtypeStruct((t_rows, HW), BF16),
        jax.ShapeDtypeStruct((t_rows, HW), BF16),
        jax.ShapeDtypeStruct((t_rows, KV_LORA), F32),
        jax.ShapeDtypeStruct((t_rows, HEAD_TILE), F32),
    ]
    blocks = (_nbytes((TM, d), F32) + _nbytes((d, win_cols), BF16) + _nbytes((Q_LORA, 2 * HW), BF16)
              + _nbytes((KV_LORA, 2 * HW), BF16) + _nbytes((TM, 2 * HEAD_TILE), F32)
              + sum(_nbytes((TM, s.shape[1]), s.dtype) for s in out_shapes))
    return pl.pallas_call(
        functools.partial(_in_kernel, d=d),
        grid=(n_tiles,),
        in_specs=[
            row(d),
            pl.BlockSpec((None, None, 1, N_MOD * d), mod_idx),
            lay(1, d),
            lay(d, win_cols),
            pl.BlockSpec((F_GROUP_W, 2 * F_GROUP_W), lambda i: (0, 0)),
            lay(1, Q_LORA),
            lay(Q_LORA, 2 * HW),
            lay(1, KV_LORA),
            lay(KV_LORA, 2 * HW),
            lay(2, HEAD_TILE),
            lay(2, HEAD_TILE),
            pl.BlockSpec((TM, 2 * HEAD_TILE), rope_idx),
        ],
        out_specs=[row(s.shape[1]) for s in out_shapes],
        out_shape=out_shapes,
        compiler_params=pltpu.CompilerParams(
            dimension_semantics=("arbitrary",),
            vmem_limit_bytes=_vmem_limit(blocks, 0, 12 * _nbytes((TM, d), F32))),
        name="in_proj",
    )(x, mod, P["norm1_g"], P["w_in"], P["dft_cs"], P["q_norm_g"], P["w_q"], P["kv_norm_g"],
      P["w_kv"], P["gq"], P["gk"], rope_tab)


def _cache_kv_kernel(ckv_ref, kr_ref, wkv_ref, gk_ref, k_ref, v_ref):
    cb = ckv_ref[...].astype(BF16)
    kr = kr_ref[...]
    k_gain = gk_ref[0:1, :]
    for hd in range(N_HEADS):
        sl = slice(hd * HEAD_TILE, (hd + 1) * HEAD_TILE)
        t = _dot(cb, wkv_ref[:, sl]) + kr
        k_ref[:, sl] = (t * k_gain * _head_inv_rms(t)).astype(BF16)
        vs = slice(HW + hd * HEAD_TILE, HW + (hd + 1) * HEAD_TILE)
        v_ref[:, sl] = _dot(cb, wkv_ref[:, vs]).astype(BF16)


def _cache_kv(cache_ckv, cache_kr_tile, P):
    nb, depth, past, _ = cache_ckv.shape
    blk = lambda w: pl.BlockSpec((None, None, past, w), lambda b, l: (b, l, 0, 0))
    lay = lambda *s: pl.BlockSpec((None,) + s, lambda b, l: (l,) + (0,) * len(s))
    blocks = (_nbytes((past, KV_LORA + HEAD_TILE), F32) + _nbytes((KV_LORA, 2 * HW), BF16)
              + 2 * _nbytes((past, HW), BF16))
    return pl.pallas_call(
        _cache_kv_kernel,
        grid=(nb, depth),
        in_specs=[blk(KV_LORA), blk(HEAD_TILE), lay(KV_LORA, 2 * HW), lay(2, HEAD_TILE)],
        out_specs=[blk(HW), blk(HW)],
        out_shape=[jax.ShapeDtypeStruct((nb, depth, past, HW), BF16)] * 2,
        compiler_params=pltpu.CompilerParams(
            dimension_semantics=("arbitrary", "arbitrary"),
            vmem_limit_bytes=_vmem_limit(blocks, 0, 8 * _nbytes((past, HW), F32))),
        name="cache_kv",
    )(cache_ckv, cache_kr_tile, P["w_kv"], P["gk"])


def _fourier_kernel(fa_ref, c_ref, s_ref, o_ref, *, seq, n_seq):
    for s in range(n_seq):
        rows = slice(s * seq, (s + 1) * seq)
        y = _dot(c_ref[...], fa_ref[rows, 0:F_W]) - _dot(s_ref[...], fa_ref[rows, F_W:2 * F_W])
        o_ref[rows, :] = y.astype(BF16)


def _fourier_prompt(fa, dft_c, dft_s, *, t_p, seq):
    n_seq = TM // seq
    blocks = _nbytes((TM, 2 * F_W), BF16) + 2 * _nbytes((seq, seq), BF16) + _nbytes((TM, F_W), BF16)
    return pl.pallas_call(
        functools.partial(_fourier_kernel, seq=seq, n_seq=n_seq),
        grid=(t_p // TM,),
        in_specs=[
            pl.BlockSpec((TM, 2 * F_W), lambda i: (i, 0)),
            pl.BlockSpec((seq, seq), lambda i: (0, 0)),
            pl.BlockSpec((seq, seq), lambda i: (0, 0)),
        ],
        out_specs=pl.BlockSpec((TM, F_W), lambda i: (i, 0)),
        out_shape=jax.ShapeDtypeStruct((t_p, F_W), BF16),
        compiler_params=pltpu.CompilerParams(
            dimension_semantics=("arbitrary",),
            vmem_limit_bytes=_vmem_limit(blocks, 0, 4 * _nbytes((TM, F_W), F32))),
        name="fourier_prompt",
    )(fa, dft_c, dft_s)


def _fourier_sample_kernel(fa_ref, c_ref, s_ref, o_ref):
    y = _dot(c_ref[...], fa_ref[:, 0:F_W]) - _dot(s_ref[...], fa_ref[:, F_W:2 * F_W])
    o_ref[...] = y.astype(BF16)


def _fourier_sample(fa, dft_c, dft_s, *, t_p, n_b, seq):
    tr = TM
    per = seq // tr
    first = t_p // seq
    blocks = _nbytes((seq, 2 * F_W), BF16) + 2 * _nbytes((tr, seq), BF16) + _nbytes((tr, F_W), BF16)
    return pl.pallas_call(
        _fourier_sample_kernel,
        grid=(n_b, per),
        in_specs=[
            pl.BlockSpec((seq, 2 * F_W), lambda b, j: (first + b, 0)),
            pl.BlockSpec((tr, seq), lambda b, j: (j, 0)),
            pl.BlockSpec((tr, seq), lambda b, j: (j, 0)),
        ],
        out_specs=pl.BlockSpec((tr, F_W), lambda b, j: (b * per + j, 0)),
        out_shape=jax.ShapeDtypeStruct((n_b * seq, F_W), BF16),
        compiler_params=pltpu.CompilerParams(
            dimension_semantics=("arbitrary", "arbitrary"),
            vmem_limit_bytes=_vmem_limit(blocks, 0, 4 * _nbytes((tr, F_W), F32))),
        name="fourier_sample",
    )(fa, dft_c, dft_s)


def _conv_block_groups(n_grp):
    return max(g for g in range(1, CONV_MAX_BLOCK_GROUPS + 1) if n_grp % g == 0)


def _conv_kernel(u_ref, dw_ref, dwb_ref, lng_ref, lnb_ref, o_ref, nat_ref, ext_ref, *, seq):
    half = CONV_K // 2
    win = seq + 2 * CONV_PAD
    n_grp = win // SUBLANES
    n_slab = CONV_W // LANES
    lead = half * SUBLANES
    zeros = jnp.zeros((CONV_PAD, LANES), F32)
    for k in range(n_slab):
        nat_ref[k, 0:CONV_PAD, :] = zeros
        nat_ref[k, CONV_PAD + seq:win, :] = zeros
        nat_ref[k, CONV_PAD:CONV_PAD + seq, :] = u_ref[:, k * LANES:(k + 1) * LANES]
    for k in range(n_slab):
        for v in range(n_grp):
            g = nat_ref[k, pl.ds(v, SUBLANES, stride=n_grp), :]
            ext_ref[k, lead + v * SUBLANES:lead + (v + 1) * SUBLANES, :] = g
            if v >= n_grp - half:
                j = v - (n_grp - half)
                ext_ref[k, j * SUBLANES:(j + 1) * SUBLANES, :] = pltpu.roll(g, 1, axis=0)
            if v < half:
                j = half + n_grp + v
                ext_ref[k, j * SUBLANES:(j + 1) * SUBLANES, :] = pltpu.roll(g, SUBLANES - 1, axis=0)

    blk_grp = _conv_block_groups(n_grp)
    blk = blk_grp * SUBLANES

    for k in range(n_slab):
        lanes = slice(k * LANES, (k + 1) * LANES)

        def conv_body(b, carry, k=k, lanes=lanes):
            r0 = pl.multiple_of(b * blk, SUBLANES)
            acc = jnp.zeros((blk, LANES), F32)
            for t in range(CONV_K):
                acc = acc + dw_ref[t:t + 1, lanes] * ext_ref[k, pl.ds(r0 + t * SUBLANES, blk), :]
            ext_ref[k, pl.ds(r0, blk), :] = acc + dwb_ref[:, lanes]
            return carry

        lax.fori_loop(0, n_grp // blk_grp, conv_body, 0)

    def norm_body(b, carry):
        r0 = pl.multiple_of(b * blk, SUBLANES)
        acc = jnp.concatenate([ext_ref[k, pl.ds(r0, blk), :] for k in range(n_slab)], axis=1)
        mu = jnp.mean(acc, axis=-1, keepdims=True)
        cen = acc - mu
        var = jnp.mean(cen * cen, axis=-1, keepdims=True)
        y = _silu(cen * lax.rsqrt(var + EPS) * lng_ref[...] + lnb_ref[...])
        for k in range(n_slab):
            ext_ref[k, pl.ds(r0, blk), :] = y[:, k * LANES:(k + 1) * LANES]
        return carry

    lax.fori_loop(0, n_grp // blk_grp, norm_body, 0)

    for k in range(n_slab):
        for v in range(n_grp):
            nat_ref[k, pl.ds(v, SUBLANES, stride=n_grp), :] = ext_ref[k, v * SUBLANES:(v + 1) * SUBLANES, :]
        o_ref[:, k * LANES:(k + 1) * LANES] = nat_ref[k, CONV_PAD:CONV_PAD + seq, :].astype(BF16)


def _conv_module(ug, l, P, *, first, n_seq, seq):
    lay = lambda *s: pl.BlockSpec((None,) + s, lambda i: (l,) + (0,) * len(s))
    blocks = _nbytes((seq, CONV_W), F32) + _nbytes((seq, CONV_W), BF16) + _nbytes((CONV_K + 4, CONV_W), F32)
    win = seq + 2 * CONV_PAD
    ext = win + 2 * (CONV_K // 2) * SUBLANES
    scratch = _nbytes((win + ext, CONV_W), F32)
    return pl.pallas_call(
        functools.partial(_conv_kernel, seq=seq),
        grid=(n_seq,),
        in_specs=[
            pl.BlockSpec((seq, CONV_W), lambda i: (first + i, 0)),
            lay(CONV_K, CONV_W), lay(1, CONV_W), lay(1, CONV_W), lay(1, CONV_W),
        ],
        out_specs=pl.BlockSpec((seq, CONV_W), lambda i: (i, 0)),
        out_shape=jax.ShapeDtypeStruct((n_seq * seq, CONV_W), BF16),
        scratch_shapes=[pltpu.VMEM((CONV_W // LANES, win, LANES), F32),
                        pltpu.VMEM((CONV_W // LANES, ext, LANES), F32)],
        compiler_params=pltpu.CompilerParams(
            dimension_semantics=("arbitrary",),
            vmem_limit_bytes=_vmem_limit(blocks, scratch, 4 * 2**20)),
        name=f"conv_seq{seq}",
    )(ug, P["conv_dw"], P["conv_dw_b"], P["conv_ln_g"], P["conv_ln_b"])


def _attn_kernel(*refs, with_cache):
    if with_cache:
        q_ref, k_ref, v_ref, kc_ref, vc_ref, o_ref = refs
    else:
        q_ref, k_ref, v_ref, o_ref = refs
    tq = q_ref.shape[0]
    lane = lax.broadcasted_iota(jnp.int32, (tq, HEAD_TILE), 1)

    def scores(hd):
        sl = slice(hd * HEAD_TILE, (hd + 1) * HEAD_TILE)
        qh = q_ref[:, sl]
        return _dot_nt(qh, k_ref[:, sl]), (_dot_nt(qh, kc_ref[:, sl]) if with_cache else None)

    nxt = scores(0)
    acc = None
    inv = []
    for hd in range(N_HEADS):
        sl = slice(hd * HEAD_TILE, (hd + 1) * HEAD_TILE)
        s_m, s_c = nxt
        if hd + 1 < N_HEADS:
            nxt = scores(hd + 1)
        m = jnp.max(s_m, axis=-1, keepdims=True)
        if with_cache:
            m = jnp.maximum(m, jnp.max(s_c, axis=-1, keepdims=True))
        e_m = jnp.exp(s_m - m)
        den = jnp.sum(e_m, axis=-1, keepdims=True)
        pv = _dot(e_m.astype(BF16), v_ref[:, sl])
        if with_cache:
            e_c = jnp.exp(s_c - m)
            den = den + jnp.sum(e_c, axis=-1, keepdims=True)
            pv = pv + _dot(e_c.astype(BF16), vc_ref[:, sl])
        acc = pv if hd % 2 == 0 else acc + pv
        inv.append(1.0 / den)
        if hd % 2 == 1:
            scale = jnp.where(lane < V_DIM, inv[hd - 1], inv[hd])
            o_ref[:, (hd // 2) * HEAD_TILE:(hd // 2 + 1) * HEAD_TILE] = (acc * scale).astype(BF16)


def _attn_prompt(q, k, v, *, n_seq, seq):
    blk = pl.BlockSpec((seq, HW), lambda i: (i, 0))
    blocks = 3 * _nbytes((seq, HW), BF16) + _nbytes((seq, N_HEADS * V_DIM), BF16)
    return pl.pallas_call(
        functools.partial(_attn_kernel, with_cache=False),
        grid=(n_seq,),
        in_specs=[blk, blk, blk],
        out_specs=pl.BlockSpec((seq, N_HEADS * V_DIM), lambda i: (i, 0)),
        out_shape=jax.ShapeDtypeStruct((n_seq * seq, N_HEADS * V_DIM), BF16),
        compiler_params=pltpu.CompilerParams(
            dimension_semantics=("arbitrary",),
            vmem_limit_bytes=_vmem_limit(blocks, 0, 8 * _nbytes((seq, seq), F32) + 2**21)),
        name="attn_prompt",
    )(q, k, v)


def _attn_sample(q, k, v, kc, vc, l, *, t_p, n_b, seq, tq):
    per = seq // tq
    first_q = t_p // tq
    first_s = t_p // seq
    past = kc.shape[2]
    once = pl.Buffered(1)
    kv_blk = pl.BlockSpec((seq, HW), lambda b, j: (first_s + b, 0), pipeline_mode=once)
    c_blk = pl.BlockSpec((None, None, past, HW), lambda b, j: (b, l, 0, 0), pipeline_mode=once)
    blocks = _nbytes((tq, HW), BF16) + _nbytes((tq, N_HEADS * V_DIM), BF16)
    resident = 2 * _nbytes((seq + past, HW), BF16)
    return pl.pallas_call(
        functools.partial(_attn_kernel, with_cache=True),
        grid=(n_b, per),
        in_specs=[pl.BlockSpec((tq, HW), lambda b, j: (first_q + b * per + j, 0)),
                  kv_blk, kv_blk, c_blk, c_blk],
        out_specs=pl.BlockSpec((tq, N_HEADS * V_DIM), lambda b, j: (b * per + j, 0)),
        out_shape=jax.ShapeDtypeStruct((n_b * seq, N_HEADS * V_DIM), BF16),
        compiler_params=pltpu.CompilerParams(
            dimension_semantics=("arbitrary", "arbitrary"),
            vmem_limit_bytes=_vmem_limit(blocks, resident, 8 * _nbytes((tq, seq + past), F32))),
        name="attn_sample",
    )(q, k, v, kc, vc)


def _merge_kernel(x_ref, mod_ref, g1_ref, fp_ref, fs_ref, cp_ref, cs_ref, ap_ref, as_ref,
                  wg_ref, bg_ref, wf_ref, wc_ref, wa_ref, wo_ref, o_ref, *, d, n_p):
    is_prompt = pl.program_id(0) < n_p
    x = x_ref[...]
    sh1 = mod_ref[:, 0:d]
    sc1 = mod_ref[:, d:2 * d]
    g1 = mod_ref[:, 2 * d:3 * d]
    hb = (_rms(x) * g1_ref[...] * (1.0 + sc1) + sh1).astype(BF16)
    merged = None
    branches = ((fp_ref, fs_ref, wf_ref), (cp_ref, cs_ref, wc_ref), (ap_ref, as_ref, wa_ref))
    for b, (yp_ref, ys_ref, w_ref) in enumerate(branches):
        y_in = jnp.where(is_prompt, yp_ref[...], ys_ref[...])
        y = _dot(y_in, w_ref[...])
        gate = _sigmoid(_dot(hb, wg_ref[:, b * d:(b + 1) * d]) + bg_ref[:, b * d:(b + 1) * d])
        merged = gate * y if merged is None else merged + gate * y
    o_ref[...] = x + g1 * _dot(merged.astype(BF16), wo_ref[...])


def _merge(x, mod, l, P, yf, yc, ya, *, n_p, per_seq):
    t_rows, d = x.shape
    n_tiles = t_rows // TM
    n_s = n_tiles - n_p

    def mod_idx(i):
        return (l, jnp.where(i < n_p, 0, 1 + (i - n_p) // per_seq), 0, 0)

    row = pl.BlockSpec((TM, d), lambda i: (i, 0))
    lay = lambda *s: pl.BlockSpec((None,) + s, lambda i: (l,) + (0,) * len(s))
    p_blk = pl.BlockSpec((TM, F_W), lambda i: (jnp.minimum(i, n_p - 1), 0))
    s_blk = pl.BlockSpec((TM, F_W), lambda i: (jnp.clip(i - n_p, 0, n_s - 1), 0))
    blocks = (2 * _nbytes((TM, d), F32) + 6 * _nbytes((TM, F_W), BF16) + _nbytes((d, 3 * d), BF16)
              + 3 * _nbytes((F_W, d), BF16) + _nbytes((d, d), BF16))
    return pl.pallas_call(
        functools.partial(_merge_kernel, d=d, n_p=n_p),
        grid=(n_tiles,),
        in_specs=[row, pl.BlockSpec((None, None, 1, N_MOD * d), mod_idx), lay(1, d),
                  p_blk, s_blk, p_blk, s_blk, p_blk, s_blk,
                  lay(d, 3 * d), lay(1, 3 * d), lay(F_W, d), lay(CONV_W, d), lay(N_HEADS * V_DIM, d),
                  lay(d, d)],
        out_specs=row,
        out_shape=jax.ShapeDtypeStruct((t_rows, d), F32),
        compiler_params=pltpu.CompilerParams(
            dimension_semantics=("arbitrary",),
            vmem_limit_bytes=_vmem_limit(blocks, 0, 8 * _nbytes((TM, d), F32))),
        name="merge",
    )(x, mod, P["norm1_g"], yf[0], yf[1], yc[0], yc[1], ya[0], ya[1],
      P["w_gate"], P["b_gate"], P["w_fourier"], P["w_conv_out"], P["w_mla_out"], P["w_out"])


def _ffn_kernel(x_ref, xp_ref, xn_ref, mod_ref, g2_ref, up_ref, dw_ref, b_ref, dn_ref,
                o_ref, nat_ref, h_ref, act_ref, *, d, d_ff, n_p, seq_p, seq_s):
    i = pl.program_id(0)
    sh2 = mod_ref[:, 3 * d:4 * d]
    sc2 = mod_ref[:, 4 * d:5 * d]
    g2 = mod_ref[:, 5 * d:6 * d]

    def norm_mod(x):
        return _rms(x) * g2_ref[...] * (1.0 + sc2) + sh2

    is_prompt = i < n_p
    period_mask = jnp.where(is_prompt, seq_p - 1, seq_s - 1)
    has_prev = ((i * FFN_SUB) & period_mask) != 0
    has_next = (((i + 1) * FFN_SUB) & period_mask) != 0
    h_prev = jnp.where(has_prev, norm_mod(xp_ref[...]), 0.0)
    h_next = jnp.where(has_next, norm_mod(xn_ref[...]), 0.0)
    h_main = norm_mod(x_ref[...])

    n_grp = FFN_WIN // SUBLANES
    for k in range(d // LANES):
        lanes = slice(k * LANES, (k + 1) * LANES)
        nat_ref[k, 0:SUBLANES, :] = h_prev[:, lanes]
        nat_ref[k, SUBLANES:SUBLANES + FFN_SUB, :] = h_main[:, lanes]
        nat_ref[k, SUBLANES + FFN_SUB:FFN_WIN, :] = h_next[:, lanes]
    for k in range(d // LANES):
        groups = [nat_ref[k, pl.ds(v, SUBLANES, stride=n_grp), :] for v in range(n_grp)]
        h_ref[:, k * LANES:(k + 1) * LANES] = jnp.concatenate(groups, axis=0).astype(BF16)

    sub = lax.broadcasted_iota(jnp.int32, (SUBLANES, 1), 0)
    no_prev = jnp.zeros((SUBLANES, 1), jnp.bool_)
    no_next = jnp.zeros((SUBLANES, 1), jnp.bool_)
    for edge in range(seq_p, FFN_SUB, seq_p):
        assert (edge + SUBLANES) % n_grp == 0
        no_prev = jnp.logical_or(no_prev, sub == (edge + SUBLANES) // n_grp)
        no_next = jnp.logical_or(no_next, sub == (edge + SUBLANES - 1) // n_grp)
    no_prev = jnp.logical_and(is_prompt, no_prev)
    no_next = jnp.logical_and(is_prompt, no_next)

    def conv(u, c0):
        last = u[FFN_WIN - SUBLANES:FFN_WIN, :]
        first = u[0:SUBLANES, :]
        wrap_prev = jnp.where(no_prev, 0.0, pltpu.roll(last, 1, axis=0))
        wrap_next = jnp.where(no_next, 0.0, pltpu.roll(first, SUBLANES - 1, axis=0))
        prev = jnp.concatenate([wrap_prev, u[0:FFN_WIN - SUBLANES, :]], axis=0)
        nxt = jnp.concatenate([u[SUBLANES:FFN_WIN, :], wrap_next], axis=0)
        w = dw_ref[:, c0:c0 + FFN_CK]
        return w[0:1, :] * prev + w[1:2, :] * u + w[2:3, :] * nxt + b_ref[:, c0:c0 + FFN_CK]

    hx = h_ref[...]

    def up_proj(c):
        return (_dot(hx, up_ref[:, c * FFN_CK:(c + 1) * FFN_CK]),
                _dot(hx, up_ref[:, d_ff + c * FFN_CK:d_ff + (c + 1) * FFN_CK]))

    n_ck = d_ff // FFN_CK
    half_ck = (n_ck + 1) // 2
    part = None
    nxt = up_proj(0)
    for c in range(n_ck):
        u_a, u_b = nxt
        if c + 1 < n_ck:
            nxt = up_proj(c + 1)
        if c == half_ck:
            part = _dot(act_ref[:, 0:half_ck * FFN_CK], dn_ref[0:half_ck * FFN_CK, :])
        act = _silu(conv(u_a, c * FFN_CK)) * conv(u_b, d_ff + c * FFN_CK)
        act_ref[:, c * FFN_CK:(c + 1) * FFN_CK] = act.astype(BF16)
    y = part + _dot(act_ref[:, half_ck * FFN_CK:d_ff], dn_ref[half_ck * FFN_CK:d_ff, :])
    for k in range(d // LANES):
        lanes = slice(k * LANES, (k + 1) * LANES)
        for v in range(n_grp):
            nat_ref[k, pl.ds(v, SUBLANES, stride=n_grp), :] = y[v * SUBLANES:(v + 1) * SUBLANES, lanes]
        o_ref[:, lanes] = x_ref[:, lanes] + g2[:, lanes] * nat_ref[k, SUBLANES:SUBLANES + FFN_SUB, :]


def _ffn(x, mod, l, P, *, n_p, per_seq, seq_p, seq_s):
    t_rows, d = x.shape
    d_ff = P["ffn_down"].shape[1]
    n_tiles = t_rows // FFN_SUB
    halo_per_tile = FFN_SUB // SUBLANES
    n_halo = t_rows // SUBLANES

    def mod_idx(i):
        return (l, jnp.where(i < n_p, 0, 1 + (i - n_p) // per_seq), 0, 0)

    row = pl.BlockSpec((FFN_SUB, d), lambda i: (i, 0))
    once = pl.Buffered(1)
    lay = lambda *s: pl.BlockSpec((None,) + s, lambda i: (l,) + (0,) * len(s))
    res = lambda *s: pl.BlockSpec((None,) + s, lambda i: (l,) + (0,) * len(s), pipeline_mode=once)
    weights = _nbytes((d, 2 * d_ff), BF16) + _nbytes((d_ff, d), BF16)
    blocks = 2 * _nbytes((FFN_SUB + 2 * SUBLANES, d), F32) + 4 * _nbytes((8, 2 * d_ff), F32)
    scratch = (_nbytes((FFN_WIN, d), F32) + _nbytes((FFN_WIN, d), BF16) + _nbytes((FFN_WIN, d_ff), BF16))
    temps = 4 * _nbytes((FFN_WIN, d), F32) + 8 * _nbytes((FFN_WIN, FFN_CK), F32)
    return pl.pallas_call(
        functools.partial(_ffn_kernel, d=d, d_ff=d_ff, n_p=n_p, seq_p=seq_p, seq_s=seq_s),
        grid=(n_tiles,),
        in_specs=[
            row,
            pl.BlockSpec((SUBLANES, d), lambda i: (jnp.maximum(i * halo_per_tile - 1, 0), 0)),
            pl.BlockSpec((SUBLANES, d), lambda i: (jnp.minimum((i + 1) * halo_per_tile, n_halo - 1), 0)),
            pl.BlockSpec((None, None, 1, N_MOD * d), mod_idx),
            lay(1, d),
            res(d, 2 * d_ff),
            lay(FFN_K, 2 * d_ff),
            lay(1, 2 * d_ff),
            res(d_ff, d),
        ],
        out_specs=row,
        out_shape=jax.ShapeDtypeStruct((t_rows, d), F32),
        scratch_shapes=[pltpu.VMEM((d // LANES, FFN_WIN, LANES), F32), pltpu.VMEM((FFN_WIN, d), BF16),
                        pltpu.VMEM((FFN_WIN, d_ff), BF16)],
        compiler_params=pltpu.CompilerParams(
            dimension_semantics=("arbitrary",),
            vmem_limit_bytes=_vmem_limit(blocks, scratch + weights, temps)),
        name="conv_ffn",
    )(x, x, x, mod, P["norm2_g"], P["ffn_up"], P["ffn_dw"], P["ffn_dw_b"], P["ffn_down"])


def _dft_tables(n, scale):
    j = np.arange(n, dtype=np.int64)
    ang = ((j[:, None] * j[None, :]) % n).astype(np.float64) * (2.0 * math.pi / n)
    return (np.cos(ang) * scale).astype(np.float32), (np.sin(ang) * scale).astype(np.float32)


def _rope_partner():
    lane = np.arange(HEAD_TILE)
    dim = lane - QK_NOPE
    in_rope = (dim >= 0) & (dim < QK_ROPE)
    quarter = QK_ROPE // 4
    first_half = (dim % (2 * quarter)) // quarter == 0
    perm = np.where(in_rope, np.where(first_half, lane + quarter, lane - quarter), lane)
    return perm, in_rope.astype(np.float32)


def _rope_table(seq, ident_rows):
    rows = seq // GRID_W
    row = np.repeat(np.arange(rows), GRID_W).astype(np.float32)
    col = np.tile(np.arange(GRID_W), rows).astype(np.float32)
    half = QK_ROPE // 2
    inv = (np.float32(ROPE_BASE) ** (-np.arange(0, half, 2, dtype=np.float32) / half)).astype(np.float32)
    ang = np.stack([row[:, None] * inv, col[:, None] * inv], axis=1)
    cos, sin = np.cos(ang), np.sin(ang)
    cos_r = np.stack([cos, cos], axis=2).reshape(seq, QK_ROPE)
    sin_r = np.stack([-sin, sin], axis=2).reshape(seq, QK_ROPE)

    def tile(mid, fill):
        left = np.full((seq, QK_NOPE), fill, np.float32)
        right = np.full((seq, HEAD_TILE - QK_DIM), fill, np.float32)
        return np.concatenate([left, mid, right], axis=1)

    tab = np.concatenate([tile(cos_r, 1.0), tile(sin_r, 0.0)], axis=1)
    ident = np.concatenate([np.ones((ident_rows, HEAD_TILE), np.float32),
                            np.zeros((ident_rows, HEAD_TILE), np.float32)], axis=1)
    return np.concatenate([tab, ident], axis=0).astype(np.float32)


def _prepare_params(W):
    depth, d, _ = W["w_in"].shape
    P = {}
    perm, rope_mask = _rope_partner()
    split = F_W + 2 * CONV_W + Q_LORA + KV_LORA
    w_in = W["w_in"]
    kr_tile = jnp.concatenate([jnp.zeros((depth, d, QK_NOPE), F32), w_in[:, :, split:],
                               jnp.zeros((depth, d, HEAD_TILE - QK_DIM), F32)], axis=2)
    P["w_in"] = jnp.concatenate([w_in[:, :, :split], kr_tile, kr_tile[:, :, perm] * rope_mask],
                                axis=2).astype(BF16)
    wq = W["w_q_up"].reshape(depth, Q_LORA, N_HEADS, QK_DIM)
    wq = jnp.pad(wq, ((0, 0), (0, 0), (0, 0), (0, HEAD_TILE - QK_DIM)))
    wq_rot = wq[..., perm] * rope_mask
    P["w_q"] = jnp.concatenate([wq.reshape(depth, Q_LORA, HW), wq_rot.reshape(depth, Q_LORA, HW)],
                               axis=2).astype(BF16)
    wkv = W["w_kv_up"].reshape(depth, KV_LORA, N_HEADS, QK_NOPE + V_DIM)
    zk = jnp.zeros((depth, KV_LORA, N_HEADS, HEAD_TILE - QK_NOPE), F32)
    wk = jnp.concatenate([wkv[..., :QK_NOPE], zk], axis=-1)
    zv = jnp.zeros((depth, KV_LORA, N_HEADS, HEAD_TILE - V_DIM), F32)
    wv_even = jnp.concatenate([wkv[..., QK_NOPE:], zv], axis=-1)
    wv_odd = jnp.concatenate([zv, wkv[..., QK_NOPE:]], axis=-1)
    odd = (jnp.arange(N_HEADS) % 2 == 1)[None, None, :, None]
    wv = jnp.where(odd, wv_odd, wv_even)
    P["w_kv"] = jnp.concatenate([wk.reshape(depth, KV_LORA, HW), wv.reshape(depth, KV_LORA, HW)],
                                axis=2).astype(BF16)
    def head_gains(g):
        g = jnp.pad(g, ((0, 0), (0, HEAD_TILE - QK_DIM)))
        return jnp.stack([g, g[:, perm] * rope_mask], axis=1)

    P["gq"] = head_gains(W["qk_q_g"])
    P["gk"] = head_gains(W["qk_k_g"])
    vec = lambda a: a.reshape(depth, 1, a.shape[-1])
    for name in ("norm1_g", "norm2_g", "q_norm_g", "kv_norm_g", "b_gate", "conv_dw_b", "conv_ln_g",
                 "conv_ln_b", "ffn_dw_b"):
        P[name] = vec(W[name])
    P["conv_dw"] = W["conv_dw"]
    P["ffn_dw"] = W["ffn_dw"]
    for name in ("w_gate", "w_fourier", "w_conv_out", "w_mla_out", "w_out", "ffn_up", "ffn_down"):
        P[name] = W[name].astype(BF16)
    return P


def kernel(x_prompt, x_sample, cache_ckv, cache_krope, c, c_ctx, ada_w, ada_b, norm1_g, norm2_g, w_in, w_gate, b_gate, w_fourier, conv_dw, conv_dw_b, conv_ln_g, conv_ln_b, w_conv_out, q_norm_g, w_q_up, kv_norm_g, w_kv_up, qk_q_g, qk_k_g, w_mla_out, w_out, ffn_up, ffn_dw, ffn_dw_b, ffn_down):
    W = dict(norm1_g=norm1_g, norm2_g=norm2_g, w_in=w_in, w_gate=w_gate, b_gate=b_gate,
             w_fourier=w_fourier, conv_dw=conv_dw, conv_dw_b=conv_dw_b, conv_ln_g=conv_ln_g,
             conv_ln_b=conv_ln_b, w_conv_out=w_conv_out, q_norm_g=q_norm_g, w_q_up=w_q_up,
             kv_norm_g=kv_norm_g, w_kv_up=w_kv_up, qk_q_g=qk_q_g, qk_k_g=qk_k_g,
             w_mla_out=w_mla_out, w_out=w_out, ffn_up=ffn_up, ffn_dw=ffn_dw, ffn_dw_b=ffn_dw_b,
             ffn_down=ffn_down)
    n_bp, seq_p, d = x_prompt.shape
    n_bs, seq_s, _ = x_sample.shape
    depth = ada_w.shape[0]
    t_p = n_bp * seq_p
    t_s = n_bs * seq_s
    assert t_p % seq_s == 0 and seq_s % TM == 0 and TM % seq_p == 0 and seq_s % FFN_SUB == 0
    assert seq_p & (seq_p - 1) == 0 and seq_s & (seq_s - 1) == 0 and n_bs + 1 <= 8
    n_p = t_p // TM
    per_seq = seq_s // TM

    P = _prepare_params(W)
    c_cs, s_cs = _dft_tables(F_GROUP_W, 1.0 / math.sqrt(F_GROUP_W))
    P["dft_cs"] = jnp.asarray(np.concatenate([c_cs, s_cs], axis=1), dtype=BF16)
    dft_p = [jnp.asarray(t, dtype=BF16) for t in _dft_tables(seq_p, 1.0 / math.sqrt(seq_p))]
    dft_s = [jnp.asarray(t, dtype=BF16) for t in _dft_tables(seq_s, 1.0 / math.sqrt(seq_s))]
    rope_tab = jnp.asarray(_rope_table(seq_s, TM))

    cvec = jnp.concatenate([c_ctx[None, :], c, jnp.zeros((8 - 1 - n_bs, d), F32)], axis=0)
    mod = _mod_table(cvec, ada_w, ada_b).reshape(depth, 8, 1, N_MOD * d)

    kr_tile = jnp.pad(cache_krope, ((0, 0), (0, 0), (0, 0), (QK_NOPE, HEAD_TILE - QK_DIM)))
    kc, vc = _cache_kv(cache_ckv, kr_tile, P)

    x = jnp.concatenate([x_prompt.reshape(t_p, d), x_sample.reshape(t_s, d)], axis=0)
    ckv_out, kr_out = [], []
    for l in range(depth):
        fa, ug, q, k, v, ckv, kr = _in_proj(x, mod, l, P, rope_tab, n_p=n_p, per_seq=per_seq)
        ckv_out.append(ckv[:t_p].reshape(n_bp, seq_p, KV_LORA))
        kr_out.append(kr[:t_p, QK_NOPE:QK_DIM].reshape(n_bp, seq_p, QK_ROPE))
        yf = (_fourier_prompt(fa, *dft_p, t_p=t_p, seq=seq_p),
              _fourier_sample(fa, *dft_s, t_p=t_p, n_b=n_bs, seq=seq_s))
        yc = (_conv_module(ug, l, P, first=0, n_seq=n_bp, seq=seq_p),
              _conv_module(ug, l, P, first=t_p // seq_s, n_seq=n_bs, seq=seq_s))
        ya = (_attn_prompt(q, k, v, n_seq=n_bp, seq=seq_p),
              _attn_sample(q, k, v, kc, vc, l, t_p=t_p, n_b=n_bs, seq=seq_s, tq=ATTN_TQ))
        x = _merge(x, mod, l, P, yf, yc, ya, n_p=n_p, per_seq=per_seq)
        x = _ffn(x, mod, l, P, n_p=t_p // FFN_SUB, per_seq=seq_s // FFN_SUB, seq_p=seq_p, seq_s=seq_s)
    y_prompt = x[:t_p].reshape(n_bp, seq_p, d)
    y_sample = x[t_p:].reshape(n_bs, seq_s, d)
    return (y_prompt, y_sample, jnp.stack(ckv_out, axis=1), jnp.stack(kr_out, axis=1))
```

```python
import functools
import math

import numpy as np
import jax
import jax.numpy as jnp
from jax import lax
from jax.experimental import pallas as pl
from jax.experimental.pallas import tpu as pltpu

F32 = jnp.float32
BF16 = jnp.bfloat16

EPS = 1e-6
N_HEADS = 8
QK_NOPE = 64
QK_ROPE = 32
V_DIM = 64
QK_DIM = QK_NOPE + QK_ROPE
F_GROUPS = 4
F_GROUP_W = 128
F_W = F_GROUPS * F_GROUP_W
CONV_W = 512
CONV_K = 31
Q_LORA = 384
KV_LORA = 256
GRID_W = 64
ROPE_BASE = 10000.0
FFN_K = 3
N_MOD = 6

LANES = 128
SUBLANES = 8
V7X_VMEM_BYTES = 64 * 2**20
VMEM_RESERVE_BYTES = 6 * 2**20

HEAD_TILE = LANES
HW = N_HEADS * HEAD_TILE
PAIR_W = 2 * HEAD_TILE
CONV_PAD = 16

TM = 512
ATTN_TQ = 512
CONV_MAX_BLOCK_GROUPS = 16
FFN_SUB = 512
FFN_WIN = FFN_SUB + 2 * SUBLANES
FFN_CK = 256


def _vmem_limit(block_bytes, scratch_bytes, temp_bytes):
    need = 2 * block_bytes + scratch_bytes + temp_bytes
    return int(min(need, V7X_VMEM_BYTES - VMEM_RESERVE_BYTES))


def _nbytes(shape, dtype):
    return int(np.prod(shape)) * jnp.dtype(dtype).itemsize


def _dot(a, b):
    return jnp.dot(a, b, preferred_element_type=F32)


def _dot_nt(a, b):
    return lax.dot_general(a, b, (((1,), (1,)), ((), ())), preferred_element_type=F32)


def _rms(x):
    return x * lax.rsqrt(jnp.mean(x * x, axis=-1, keepdims=True) + EPS)


def _sigmoid(x):
    return 1.0 / (1.0 + jnp.exp(-x))


def _silu(x):
    return x * _sigmoid(x)


def _mod_kernel(c_ref, w_ref, b_ref, o_ref):
    s = _silu(c_ref[...]).astype(BF16)
    o_ref[...] = _dot(s, w_ref[...].astype(BF16)) + b_ref[...]


def _mod_table(cvec, ada_w, ada_b):
    depth, d, n = ada_w.shape
    rows = cvec.shape[0]
    tn = n // 4
    blocks = _nbytes((d, tn), F32) + _nbytes((rows, tn), F32) * 2
    return pl.pallas_call(
        _mod_kernel,
        grid=(depth, n // tn),
        in_specs=[
            pl.BlockSpec((rows, d), lambda l, j: (0, 0)),
            pl.BlockSpec((None, d, tn), lambda l, j: (l, 0, j)),
            pl.BlockSpec((None, 1, tn), lambda l, j: (l, 0, j)),
        ],
        out_specs=pl.BlockSpec((None, rows, tn), lambda l, j: (l, 0, j)),
        out_shape=jax.ShapeDtypeStruct((depth, rows, n), F32),
        compiler_params=pltpu.CompilerParams(
            dimension_semantics=("arbitrary", "arbitrary"),
            vmem_limit_bytes=_vmem_limit(blocks, 0, _nbytes((d, tn), BF16) + 2**20)),
        name="adaln_table",
    )(cvec, ada_w, ada_b.reshape(depth, 1, n))


def _head_inv_rms(t):
    ss = jnp.sum(t * t, axis=-1, keepdims=True)
    return lax.rsqrt(ss * (1.0 / QK_DIM) + EPS)


def _keys_values(cb, kr, wkv_ref, k_ref, vt_ref, finish_key):
    lane = lax.broadcasted_iota(jnp.int32, (1, PAIR_W), 1)
    ones_row = jnp.logical_or(lane == V_DIM, lane == HEAD_TILE).astype(F32)
    for pair in range(N_HEADS // 2):
        cols = slice(pair * PAIR_W, (pair + 1) * PAIR_W)
        t2 = _dot(cb, wkv_ref[:, cols])
        for hh in range(2):
            sl = slice((2 * pair + hh) * HEAD_TILE, (2 * pair + hh + 1) * HEAD_TILE)
            k_ref[:, sl] = finish_key(t2[:, hh * HEAD_TILE:(hh + 1) * HEAD_TILE] + kr).astype(BF16)
        v2 = _dot(cb, wkv_ref[:, HW + pair * PAIR_W:HW + (pair + 1) * PAIR_W]) + ones_row
        vt_ref[cols, :] = v2.T.astype(BF16)


def _in_kernel(x_ref, mod_ref, g1_ref, win_ref, cs_ref, qg_ref, wq_ref, kvg_ref, wkv_ref,
               gq_ref, gk_ref, rope_ref,
               fa_ref, ug_ref, q_ref, k_ref, vt_ref, ckv_ref, kr_ref, *, d):
    sh1 = mod_ref[:, 0:d]
    sc1 = mod_ref[:, d:2 * d]
    h = _rms(x_ref[...]) * g1_ref[...] * (1.0 + sc1) + sh1
    hb = h.astype(BF16)
    c_conv = F_W
    c_q = c_conv + 2 * CONV_W
    c0 = c_q

    cos = rope_ref[:, 0:HEAD_TILE]
    sin = rope_ref[:, HEAD_TILE:2 * HEAD_TILE]

    qn = (_rms(_dot(hb, win_ref[:, c0:c0 + Q_LORA])) * qg_ref[...]).astype(BF16)
    c0 += Q_LORA
    scale = 1.0 / math.sqrt(QK_DIM)
    q_cos = cos * (gq_ref[0:1, :] * scale)
    q_sin = sin * (gq_ref[1:2, :] * scale)
    partners = _dot(qn, wq_ref[:, HW:HW + N_HEADS * QK_ROPE])
    per_tile = HEAD_TILE // QK_ROPE
    for pair in range(N_HEADS // 2):
        t2 = _dot(qn, wq_ref[:, pair * PAIR_W:(pair + 1) * PAIR_W])
        for hd in (2 * pair, 2 * pair + 1):
            sl = slice(hd * HEAD_TILE, (hd + 1) * HEAD_TILE)
            t = t2[:, (hd % 2) * HEAD_TILE:(hd % 2 + 1) * HEAD_TILE]
            tp = partners[:, (hd // per_tile) * HEAD_TILE:(hd // per_tile + 1) * HEAD_TILE]
            shift = (QK_NOPE - (hd % per_tile) * QK_ROPE) % HEAD_TILE
            if shift:
                tp = pltpu.roll(tp, shift, axis=1)
            q_ref[:, sl] = ((t * q_cos + tp * q_sin) * _head_inv_rms(t)).astype(BF16)

    ckv = _rms(_dot(hb, win_ref[:, c0:c0 + KV_LORA])) * kvg_ref[...]
    ckv_ref[...] = ckv
    c0 += KV_LORA
    kr = _dot(hb, win_ref[:, c0:c0 + HEAD_TILE])
    kr_ref[...] = kr
    k_cos = cos * gk_ref[0:1, :]
    kr_rot = _dot(hb, win_ref[:, c0 + HEAD_TILE:c0 + 2 * HEAD_TILE]) * (sin * gk_ref[1:2, :])
    _keys_values(ckv.astype(BF16), kr, wkv_ref, k_ref, vt_ref,
                 lambda t: (t * k_cos + kr_rot) * _head_inv_rms(t))

    fb = _dot(hb, win_ref[:, 0:F_W]).astype(BF16)
    for g in range(F_GROUPS):
        a = _dot(fb[:, g * F_GROUP_W:(g + 1) * F_GROUP_W], cs_ref[...])
        fa_ref[:, g * F_GROUP_W:(g + 1) * F_GROUP_W] = a[:, :F_GROUP_W].astype(BF16)
        fa_ref[:, F_W + g * F_GROUP_W:F_W + (g + 1) * F_GROUP_W] = a[:, F_GROUP_W:].astype(BF16)

    ga = _dot(hb, win_ref[:, c_conv:c_conv + CONV_W])
    gg = _dot(hb, win_ref[:, c_conv + CONV_W:c_conv + 2 * CONV_W])
    ug_ref[...] = ga * _sigmoid(gg)


def _in_proj(x, mod, l, P, rope_tab, *, n_p, per_seq):
    t_rows, d = x.shape
    n_tiles = t_rows // TM
    win_cols = P["w_in"].shape[-1]

    def mod_idx(i):
        return (l, jnp.where(i < n_p, 0, 1 + (i - n_p) // per_seq), 0, 0)

    def rope_idx(i):
        return (jnp.where(i < n_p, per_seq, (i - n_p) % per_seq), 0)

    row = lambda w: pl.BlockSpec((TM, w), lambda i: (i, 0))
    lay = lambda *s: pl.BlockSpec((None,) + s, lambda i: (l,) + (0,) * len(s))
    wq_cols = P["w_q"].shape[-1]
    out_shapes = [
        jax.ShapeDtypeStruct((t_rows, 2 * F_W), BF16),
        jax.ShapeDtypeStruct((t_rows, CONV_W), F32),
        jax.ShapeDtypeStruct((t_rows, HW), BF16),
        jax.ShapeDtypeStruct((t_rows, HW), BF16),
        jax.ShapeDtypeStruct((HW, t_rows), BF16),
        jax.ShapeDtypeStruct((t_rows, KV_LORA), F32),
        jax.ShapeDtypeStruct((t_rows, HEAD_TILE), F32),
    ]
    out_specs = [pl.BlockSpec((HW, TM), lambda i: (0, i)) if s.shape[0] == HW else row(s.shape[1])
                 for s in out_shapes]
    blocks = (_nbytes((TM, d), F32) + _nbytes((d, win_cols), BF16) + _nbytes((Q_LORA, wq_cols), BF16)
              + _nbytes((KV_LORA, 2 * HW), BF16) + _nbytes((TM, 2 * HEAD_TILE), F32)
              + sum(_nbytes(s.shape, s.dtype) // (t_rows // TM) for s in out_shapes))
    return pl.pallas_call(
        functools.partial(_in_kernel, d=d),
        grid=(n_tiles,),
        in_specs=[
            row(d),
            pl.BlockSpec((None, None, 1, N_MOD * d), mod_idx),
            lay(1, d),
            lay(d, win_cols),
            pl.BlockSpec((F_GROUP_W, 2 * F_GROUP_W), lambda i: (0, 0)),
            lay(1, Q_LORA),
            lay(Q_LORA, wq_cols),
            lay(1, KV_LORA),
            lay(KV_LORA, 2 * HW),
            lay(2, HEAD_TILE),
            lay(2, HEAD_TILE),
            pl.BlockSpec((TM, 2 * HEAD_TILE), rope_idx),
        ],
        out_specs=out_specs,
        out_shape=out_shapes,
        compiler_params=pltpu.CompilerParams(
            dimension_semantics=("arbitrary",),
            vmem_limit_bytes=_vmem_limit(blocks, 0, 12 * _nbytes((TM, d), F32))),
        name="in_proj",
    )(x, mod, P["norm1_g"], P["w_in"], P["dft_cs"], P["q_norm_g"], P["w_q"], P["kv_norm_g"],
      P["w_kv"], P["gq"], P["gk"], rope_tab)


def _cache_kv_kernel(ckv_ref, kr_ref, wkv_ref, gk_ref, k_ref, vt_ref):
    k_gain = gk_ref[0:1, :]
    _keys_values(ckv_ref[...].astype(BF16), kr_ref[...], wkv_ref, k_ref, vt_ref,
                 lambda t: t * k_gain * _head_inv_rms(t))


def _cache_kv(cache_ckv, cache_kr_tile, P):
    nb, depth, past, _ = cache_ckv.shape
    blk = lambda w: pl.BlockSpec((None, None, past, w), lambda b, l: (b, l, 0, 0))
    lay = lambda *s: pl.BlockSpec((None,) + s, lambda b, l: (l,) + (0,) * len(s))
    blocks = (_nbytes((past, KV_LORA + HEAD_TILE), F32) + _nbytes((KV_LORA, 2 * HW), BF16)
              + 2 * _nbytes((past, HW), BF16))
    return pl.pallas_call(
        _cache_kv_kernel,
        grid=(nb, depth),
        in_specs=[blk(KV_LORA), blk(HEAD_TILE), lay(KV_LORA, 2 * HW), lay(2, HEAD_TILE)],
        out_specs=[blk(HW), pl.BlockSpec((None, None, HW, past), lambda b, l: (b, l, 0, 0))],
        out_shape=[jax.ShapeDtypeStruct((nb, depth, past, HW), BF16),
                   jax.ShapeDtypeStruct((nb, depth, HW, past), BF16)],
        compiler_params=pltpu.CompilerParams(
            dimension_semantics=("arbitrary", "arbitrary"),
            vmem_limit_bytes=_vmem_limit(blocks, 0, 8 * _nbytes((past, HW), F32))),
        name="cache_kv",
    )(cache_ckv, cache_kr_tile, P["w_kv"], P["gk"])


def _fourier_kernel(fa_ref, c_ref, s_ref, o_ref, *, seq, n_seq):
    for s in range(n_seq):
        rows = slice(s * seq, (s + 1) * seq)
        y = _dot(c_ref[...], fa_ref[rows, 0:F_W]) - _dot(s_ref[...], fa_ref[rows, F_W:2 * F_W])
        o_ref[rows, :] = y.astype(BF16)


def _fourier_prompt(fa, dft_c, dft_s, *, t_p, seq):
    n_seq = TM // seq
    blocks = _nbytes((TM, 2 * F_W), BF16) + 2 * _nbytes((seq, seq), BF16) + _nbytes((TM, F_W), BF16)
    return pl.pallas_call(
        functools.partial(_fourier_kernel, seq=seq, n_seq=n_seq),
        grid=(t_p // TM,),
        in_specs=[
            pl.BlockSpec((TM, 2 * F_W), lambda i: (i, 0)),
            pl.BlockSpec((seq, seq), lambda i: (0, 0)),
            pl.BlockSpec((seq, seq), lambda i: (0, 0)),
        ],
        out_specs=pl.BlockSpec((TM, F_W), lambda i: (i, 0)),
        out_shape=jax.ShapeDtypeStruct((t_p, F_W), BF16),
        compiler_params=pltpu.CompilerParams(
            dimension_semantics=("arbitrary",),
            vmem_limit_bytes=_vmem_limit(blocks, 0, 4 * _nbytes((TM, F_W), F32))),
        name="fourier_prompt",
    )(fa, dft_c, dft_s)


def _fourier_sample_kernel(fa_ref, c_ref, s_ref, o_ref):
    y = _dot(c_ref[...], fa_ref[:, 0:F_W]) - _dot(s_ref[...], fa_ref[:, F_W:2 * F_W])
    o_ref[...] = y.astype(BF16)


def _fourier_sample(fa, dft_c, dft_s, *, t_p, n_b, seq):
    tr = TM
    per = seq // tr
    first = t_p // seq
    blocks = _nbytes((seq, 2 * F_W), BF16) + 2 * _nbytes((tr, seq), BF16) + _nbytes((tr, F_W), BF16)
    return pl.pallas_call(
        _fourier_sample_kernel,
        grid=(n_b, per),
        in_specs=[
            pl.BlockSpec((seq, 2 * F_W), lambda b, j: (first + b, 0)),
            pl.BlockSpec((tr, seq), lambda b, j: (j, 0)),
            pl.BlockSpec((tr, seq), lambda b, j: (j, 0)),
        ],
        out_specs=pl.BlockSpec((tr, F_W), lambda b, j: (b * per + j, 0)),
        out_shape=jax.ShapeDtypeStruct((n_b * seq, F_W), BF16),
        compiler_params=pltpu.CompilerParams(
            dimension_semantics=("arbitrary", "arbitrary"),
            vmem_limit_bytes=_vmem_limit(blocks, 0, 4 * _nbytes((tr, F_W), F32))),
        name="fourier_sample",
    )(fa, dft_c, dft_s)


def _conv_block_groups(n_grp):
    return max(g for g in range(1, CONV_MAX_BLOCK_GROUPS + 1) if n_grp % g == 0)


def _conv_kernel(u_ref, dw_ref, dwb_ref, lng_ref, lnb_ref, o_ref, nat_ref, ext_ref, *, seq):
    half = CONV_K // 2
    win = seq + 2 * CONV_PAD
    n_grp = win // SUBLANES
    n_slab = CONV_W // LANES
    lead = half * SUBLANES
    zeros = jnp.zeros((CONV_PAD, LANES), F32)
    for k in range(n_slab):
        nat_ref[k, 0:CONV_PAD, :] = zeros
        nat_ref[k, CONV_PAD + seq:win, :] = zeros
        nat_ref[k, CONV_PAD:CONV_PAD + seq, :] = u_ref[:, k * LANES:(k + 1) * LANES]
    for k in range(n_slab):
        for v in range(n_grp):
            g = nat_ref[k, pl.ds(v, SUBLANES, stride=n_grp), :]
            ext_ref[k, lead + v * SUBLANES:lead + (v + 1) * SUBLANES, :] = g
            if v >= n_grp - half:
                j = v - (n_grp - half)
                ext_ref[k, j * SUBLANES:(j + 1) * SUBLANES, :] = pltpu.roll(g, 1, axis=0)
            if v < half:
                j = half + n_grp + v
                ext_ref[k, j * SUBLANES:(j + 1) * SUBLANES, :] = pltpu.roll(g, SUBLANES - 1, axis=0)

    blk_grp = _conv_block_groups(n_grp)
    blk = blk_grp * SUBLANES

    for k in range(n_slab):
        lanes = slice(k * LANES, (k + 1) * LANES)

        def conv_body(b, carry, k=k, lanes=lanes):
            r0 = pl.multiple_of(b * blk, SUBLANES)
            acc = jnp.zeros((blk, LANES), F32)
            for t in range(CONV_K):
                acc = acc + dw_ref[t:t + 1, lanes] * ext_ref[k, pl.ds(r0 + t * SUBLANES, blk), :]
            ext_ref[k, pl.ds(r0, blk), :] = acc + dwb_ref[:, lanes]
            return carry

        lax.fori_loop(0, n_grp // blk_grp, conv_body, 0)

    def norm_body(b, carry):
        r0 = pl.multiple_of(b * blk, SUBLANES)
        acc = jnp.concatenate([ext_ref[k, pl.ds(r0, blk), :] for k in range(n_slab)], axis=1)
        mu = jnp.mean(acc, axis=-1, keepdims=True)
        cen = acc - mu
        var = jnp.mean(cen * cen, axis=-1, keepdims=True)
        y = _silu(cen * lax.rsqrt(var + EPS) * lng_ref[...] + lnb_ref[...])
        for k in range(n_slab):
            ext_ref[k, pl.ds(r0, blk), :] = y[:, k * LANES:(k + 1) * LANES]
        return carry

    lax.fori_loop(0, n_grp // blk_grp, norm_body, 0)

    for k in range(n_slab):
        for v in range(n_grp):
            nat_ref[k, pl.ds(v, SUBLANES, stride=n_grp), :] = ext_ref[k, v * SUBLANES:(v + 1) * SUBLANES, :]
        o_ref[:, k * LANES:(k + 1) * LANES] = nat_ref[k, CONV_PAD:CONV_PAD + seq, :].astype(BF16)


def _conv_module(ug, l, P, *, first, n_seq, seq):
    lay = lambda *s: pl.BlockSpec((None,) + s, lambda i: (l,) + (0,) * len(s))
    blocks = _nbytes((seq, CONV_W), F32) + _nbytes((seq, CONV_W), BF16) + _nbytes((CONV_K + 4, CONV_W), F32)
    win = seq + 2 * CONV_PAD
    ext = win + 2 * (CONV_K // 2) * SUBLANES
    scratch = _nbytes((win + ext, CONV_W), F32)
    return pl.pallas_call(
        functools.partial(_conv_kernel, seq=seq),
        grid=(n_seq,),
        in_specs=[
            pl.BlockSpec((seq, CONV_W), lambda i: (first + i, 0)),
            lay(CONV_K, CONV_W), lay(1, CONV_W), lay(1, CONV_W), lay(1, CONV_W),
        ],
        out_specs=pl.BlockSpec((seq, CONV_W), lambda i: (i, 0)),
        out_shape=jax.ShapeDtypeStruct((n_seq * seq, CONV_W), BF16),
        scratch_shapes=[pltpu.VMEM((CONV_W // LANES, win, LANES), F32),
                        pltpu.VMEM((CONV_W // LANES, ext, LANES), F32)],
        compiler_params=pltpu.CompilerParams(
            dimension_semantics=("arbitrary",),
            vmem_limit_bytes=_vmem_limit(blocks, scratch, 4 * 2**20)),
        name=f"conv_seq{seq}",
    )(ug, P["conv_dw"], P["conv_dw_b"], P["conv_ln_g"], P["conv_ln_b"])


def _attn_kernel(*refs, with_cache):
    if with_cache:
        q_ref, k_ref, vt_ref, kc_ref, vct_ref, o_ref = refs
    else:
        q_ref, k_ref, vt_ref, o_ref = refs
    tq = q_ref.shape[0]
    row = lax.broadcasted_iota(jnp.int32, (HEAD_TILE, tq), 0)

    def scores(hd):
        sl = slice(hd * HEAD_TILE, (hd + 1) * HEAD_TILE)
        qh = q_ref[:, sl]
        return _dot_nt(k_ref[:, sl], qh), (_dot_nt(kc_ref[:, sl], qh) if with_cache else None)

    nxt = scores(0)
    even = None
    for hd in range(N_HEADS):
        sl = slice(hd * HEAD_TILE, (hd + 1) * HEAD_TILE)
        s_m, s_c = nxt
        if hd + 1 < N_HEADS:
            nxt = scores(hd + 1)
        m = jnp.max(s_m, axis=0, keepdims=True)
        if with_cache:
            m = jnp.maximum(m, jnp.max(s_c, axis=0, keepdims=True))
        out_t = _dot(vt_ref[sl, :], jnp.exp(s_m - m).astype(BF16))
        if with_cache:
            out_t = out_t + _dot(vct_ref[sl, :], jnp.exp(s_c - m).astype(BF16))
        den_row = V_DIM if hd % 2 == 0 else 0
        out_t = out_t * (1.0 / out_t[den_row:den_row + 1, :])
        if hd % 2 == 0:
            even = out_t
        else:
            pair_t = jnp.where(row < V_DIM, even, out_t)
            o_ref[:, (hd // 2) * HEAD_TILE:(hd // 2 + 1) * HEAD_TILE] = pair_t.T.astype(BF16)


def _attn_prompt(q, k, vt, *, n_seq, seq):
    blk = pl.BlockSpec((seq, HW), lambda i: (i, 0))
    blocks = 3 * _nbytes((seq, HW), BF16) + _nbytes((seq, N_HEADS * V_DIM), BF16)
    return pl.pallas_call(
        functools.partial(_attn_kernel, with_cache=False),
        grid=(n_seq,),
        in_specs=[blk, blk, pl.BlockSpec((HW, seq), lambda i: (0, i))],
        out_specs=pl.BlockSpec((seq, N_HEADS * V_DIM), lambda i: (i, 0)),
        out_shape=jax.ShapeDtypeStruct((n_seq * seq, N_HEADS * V_DIM), BF16),
        compiler_params=pltpu.CompilerParams(
            dimension_semantics=("arbitrary",),
            vmem_limit_bytes=_vmem_limit(blocks, 0, 8 * _nbytes((seq, seq), F32) + 2**21)),
        name="attn_prompt",
    )(q, k, vt)


def _attn_sample(q, k, vt, kc, vct, l, *, t_p, n_b, seq, tq):
    per = seq // tq
    first_q = t_p // tq
    first_s = t_p // seq
    past = kc.shape[2]
    once = pl.Buffered(1)
    k_blk = pl.BlockSpec((seq, HW), lambda b, j: (first_s + b, 0), pipeline_mode=once)
    vt_blk = pl.BlockSpec((HW, seq), lambda b, j: (0, first_s + b), pipeline_mode=once)
    kc_blk = pl.BlockSpec((None, None, past, HW), lambda b, j: (b, l, 0, 0), pipeline_mode=once)
    vct_blk = pl.BlockSpec((None, None, HW, past), lambda b, j: (b, l, 0, 0), pipeline_mode=once)
    blocks = _nbytes((tq, HW), BF16) + _nbytes((tq, N_HEADS * V_DIM), BF16)
    resident = 2 * _nbytes((seq + past, HW), BF16)
    return pl.pallas_call(
        functools.partial(_attn_kernel, with_cache=True),
        grid=(n_b, per),
        in_specs=[pl.BlockSpec((tq, HW), lambda b, j: (first_q + b * per + j, 0)),
                  k_blk, vt_blk, kc_blk, vct_blk],
        out_specs=pl.BlockSpec((tq, N_HEADS * V_DIM), lambda b, j: (b * per + j, 0)),
        out_shape=jax.ShapeDtypeStruct((n_b * seq, N_HEADS * V_DIM), BF16),
        compiler_params=pltpu.CompilerParams(
            dimension_semantics=("arbitrary", "arbitrary"),
            vmem_limit_bytes=_vmem_limit(blocks, resident, 8 * _nbytes((tq, seq + past), F32))),
        name="attn_sample",
    )(q, k, vt, kc, vct)


def _merge_kernel(x_ref, mod_ref, g1_ref, fp_ref, fs_ref, cp_ref, cs_ref, ap_ref, as_ref,
                  wg_ref, bg_ref, wf_ref, wc_ref, wa_ref, wo_ref, o_ref, *, d, n_p):
    is_prompt = pl.program_id(0) < n_p
    x = x_ref[...]
    sh1 = mod_ref[:, 0:d]
    sc1 = mod_ref[:, d:2 * d]
    g1 = mod_ref[:, 2 * d:3 * d]
    hb = (_rms(x) * g1_ref[...] * (1.0 + sc1) + sh1).astype(BF16)
    merged = None
    branches = ((fp_ref, fs_ref, wf_ref), (cp_ref, cs_ref, wc_ref), (ap_ref, as_ref, wa_ref))
    for b, (yp_ref, ys_ref, w_ref) in enumerate(branches):
        y_in = jnp.where(is_prompt, yp_ref[...], ys_ref[...])
        y = _dot(y_in, w_ref[...])
        gate = _sigmoid(_dot(hb, wg_ref[:, b * d:(b + 1) * d]) + bg_ref[:, b * d:(b + 1) * d])
        merged = gate * y if merged is None else merged + gate * y
    o_ref[...] = x + g1 * _dot(merged.astype(BF16), wo_ref[...])


def _merge(x, mod, l, P, yf, yc, ya, *, n_p, per_seq):
    t_rows, d = x.shape
    n_tiles = t_rows // TM
    n_s = n_tiles - n_p

    def mod_idx(i):
        return (l, jnp.where(i < n_p, 0, 1 + (i - n_p) // per_seq), 0, 0)

    row = pl.BlockSpec((TM, d), lambda i: (i, 0))
    lay = lambda *s: pl.BlockSpec((None,) + s, lambda i: (l,) + (0,) * len(s))
    p_blk = pl.BlockSpec((TM, F_W), lambda i: (jnp.minimum(i, n_p - 1), 0))
    s_blk = pl.BlockSpec((TM, F_W), lambda i: (jnp.clip(i - n_p, 0, n_s - 1), 0))
    blocks = (2 * _nbytes((TM, d), F32) + 6 * _nbytes((TM, F_W), BF16) + _nbytes((d, 3 * d), BF16)
              + 3 * _nbytes((F_W, d), BF16) + _nbytes((d, d), BF16))
    return pl.pallas_call(
        functools.partial(_merge_kernel, d=d, n_p=n_p),
        grid=(n_tiles,),
        in_specs=[row, pl.BlockSpec((None, None, 1, N_MOD * d), mod_idx), lay(1, d),
                  p_blk, s_blk, p_blk, s_blk, p_blk, s_blk,
                  lay(d, 3 * d), lay(1, 3 * d), lay(F_W, d), lay(CONV_W, d), lay(N_HEADS * V_DIM, d),
                  lay(d, d)],
        out_specs=row,
        out_shape=jax.ShapeDtypeStruct((t_rows, d), F32),
        compiler_params=pltpu.CompilerParams(
            dimension_semantics=("arbitrary",),
            vmem_limit_bytes=_vmem_limit(blocks, 0, 8 * _nbytes((TM, d), F32))),
        name="merge",
    )(x, mod, P["norm1_g"], yf[0], yf[1], yc[0], yc[1], ya[0], ya[1],
      P["w_gate"], P["b_gate"], P["w_fourier"], P["w_conv_out"], P["w_mla_out"], P["w_out"])


def _ffn_kernel(x_ref, xp_ref, xn_ref, mod_ref, g2_ref, up_ref, dw_ref, b_ref, dn_ref,
                o_ref, nat_ref, h_ref, act_ref, *, d, d_ff, n_p, seq_p, seq_s):
    i = pl.program_id(0)
    sh2 = mod_ref[:, 3 * d:4 * d]
    sc2 = mod_ref[:, 4 * d:5 * d]
    g2 = mod_ref[:, 5 * d:6 * d]

    def norm_mod(x):
        return _rms(x) * g2_ref[...] * (1.0 + sc2) + sh2

    is_prompt = i < n_p
    period_mask = jnp.where(is_prompt, seq_p - 1, seq_s - 1)
    has_prev = ((i * FFN_SUB) & period_mask) != 0
    has_next = (((i + 1) * FFN_SUB) & period_mask) != 0
    h_prev = jnp.where(has_prev, norm_mod(xp_ref[...]), 0.0)
    h_next = jnp.where(has_next, norm_mod(xn_ref[...]), 0.0)
    h_main = norm_mod(x_ref[...])

    n_grp = FFN_WIN // SUBLANES
    for k in range(d // LANES):
        lanes = slice(k * LANES, (k + 1) * LANES)
        nat_ref[k, 0:SUBLANES, :] = h_prev[:, lanes]
        nat_ref[k, SUBLANES:SUBLANES + FFN_SUB, :] = h_main[:, lanes]
        nat_ref[k, SUBLANES + FFN_SUB:FFN_WIN, :] = h_next[:, lanes]
    for k in range(d // LANES):
        groups = [nat_ref[k, pl.ds(v, SUBLANES, stride=n_grp), :] for v in range(n_grp)]
        h_ref[:, k * LANES:(k + 1) * LANES] = jnp.concatenate(groups, axis=0).astype(BF16)

    sub = lax.broadcasted_iota(jnp.int32, (SUBLANES, 1), 0)
    no_prev = jnp.zeros((SUBLANES, 1), jnp.bool_)
    no_next = jnp.zeros((SUBLANES, 1), jnp.bool_)
    for edge in range(seq_p, FFN_SUB, seq_p):
        assert (edge + SUBLANES) % n_grp == 0
        no_prev = jnp.logical_or(no_prev, sub == (edge + SUBLANES) // n_grp)
        no_next = jnp.logical_or(no_next, sub == (edge + SUBLANES - 1) // n_grp)
    no_prev = jnp.logical_and(is_prompt, no_prev)
    no_next = jnp.logical_and(is_prompt, no_next)

    def conv(u, c0):
        last = u[FFN_WIN - SUBLANES:FFN_WIN, :]
        first = u[0:SUBLANES, :]
        wrap_prev = jnp.where(no_prev, 0.0, pltpu.roll(last, 1, axis=0))
        wrap_next = jnp.where(no_next, 0.0, pltpu.roll(first, SUBLANES - 1, axis=0))
        prev = jnp.concatenate([wrap_prev, u[0:FFN_WIN - SUBLANES, :]], axis=0)
        nxt = jnp.concatenate([u[SUBLANES:FFN_WIN, :], wrap_next], axis=0)
        w = dw_ref[:, c0:c0 + FFN_CK]
        return w[0:1, :] * prev + w[1:2, :] * u + w[2:3, :] * nxt + b_ref[:, c0:c0 + FFN_CK]

    hx = h_ref[...]

    def up_proj(c):
        return (_dot(hx, up_ref[:, c * FFN_CK:(c + 1) * FFN_CK]),
                _dot(hx, up_ref[:, d_ff + c * FFN_CK:d_ff + (c + 1) * FFN_CK]))

    n_ck = d_ff // FFN_CK
    half_ck = (n_ck + 1) // 2
    part = None
    nxt = up_proj(0)
    for c in range(n_ck):
        u_a, u_b = nxt
        if c + 1 < n_ck:
            nxt = up_proj(c + 1)
        if c == half_ck:
            part = _dot(act_ref[:, 0:half_ck * FFN_CK], dn_ref[0:half_ck * FFN_CK, :])
        act = _silu(conv(u_a, c * FFN_CK)) * conv(u_b, d_ff + c * FFN_CK)
        act_ref[:, c * FFN_CK:(c + 1) * FFN_CK] = act.astype(BF16)
    y = part + _dot(act_ref[:, half_ck * FFN_CK:d_ff], dn_ref[half_ck * FFN_CK:d_ff, :])
    for k in range(d // LANES):
        lanes = slice(k * LANES, (k + 1) * LANES)
        for v in range(n_grp):
            nat_ref[k, pl.ds(v, SUBLANES, stride=n_grp), :] = y[v * SUBLANES:(v + 1) * SUBLANES, lanes]
        o_ref[:, lanes] = x_ref[:, lanes] + g2[:, lanes] * nat_ref[k, SUBLANES:SUBLANES + FFN_SUB, :]


def _ffn(x, mod, l, P, *, n_p, per_seq, seq_p, seq_s):
    t_rows, d = x.shape
    d_ff = P["ffn_down"].shape[1]
    n_tiles = t_rows // FFN_SUB
    halo_per_tile = FFN_SUB // SUBLANES
    n_halo = t_rows // SUBLANES

    def mod_idx(i):
        return (l, jnp.where(i < n_p, 0, 1 + (i - n_p) // per_seq), 0, 0)

    row = pl.BlockSpec((FFN_SUB, d), lambda i: (i, 0))
    once = pl.Buffered(1)
    lay = lambda *s: pl.BlockSpec((None,) + s, lambda i: (l,) + (0,) * len(s))
    res = lambda *s: pl.BlockSpec((None,) + s, lambda i: (l,) + (0,) * len(s), pipeline_mode=once)
    weights = _nbytes((d, 2 * d_ff), BF16) + _nbytes((d_ff, d), BF16)
    blocks = 2 * _nbytes((FFN_SUB + 2 * SUBLANES, d), F32) + 4 * _nbytes((8, 2 * d_ff), F32)
    scratch = (_nbytes((FFN_WIN, d), F32) + _nbytes((FFN_WIN, d), BF16) + _nbytes((FFN_WIN, d_ff), BF16))
    temps = 4 * _nbytes((FFN_WIN, d), F32) + 8 * _nbytes((FFN_WIN, FFN_CK), F32)
    return pl.pallas_call(
        functools.partial(_ffn_kernel, d=d, d_ff=d_ff, n_p=n_p, seq_p=seq_p, seq_s=seq_s),
        grid=(n_tiles,),
        in_specs=[
            row,
            pl.BlockSpec((SUBLANES, d), lambda i: (jnp.maximum(i * halo_per_tile - 1, 0), 0)),
            pl.BlockSpec((SUBLANES, d), lambda i: (jnp.minimum((i + 1) * halo_per_tile, n_halo - 1), 0)),
            pl.BlockSpec((None, None, 1, N_MOD * d), mod_idx),
            lay(1, d),
            res(d, 2 * d_ff),
            lay(FFN_K, 2 * d_ff),
            lay(1, 2 * d_ff),
            res(d_ff, d),
        ],
        out_specs=row,
        out_shape=jax.ShapeDtypeStruct((t_rows, d), F32),
        scratch_shapes=[pltpu.VMEM((d // LANES, FFN_WIN, LANES), F32), pltpu.VMEM((FFN_WIN, d), BF16),
                        pltpu.VMEM((FFN_WIN, d_ff), BF16)],
        compiler_params=pltpu.CompilerParams(
            dimension_semantics=("arbitrary",),
            vmem_limit_bytes=_vmem_limit(blocks, scratch + weights, temps)),
        name="conv_ffn",
    )(x, x, x, mod, P["norm2_g"], P["ffn_up"], P["ffn_dw"], P["ffn_dw_b"], P["ffn_down"])


def _dft_tables(n, scale):
    j = np.arange(n, dtype=np.int64)
    ang = ((j[:, None] * j[None, :]) % n).astype(np.float64) * (2.0 * math.pi / n)
    return (np.cos(ang) * scale).astype(np.float32), (np.sin(ang) * scale).astype(np.float32)


def _rope_partner():
    lane = np.arange(HEAD_TILE)
    dim = lane - QK_NOPE
    in_rope = (dim >= 0) & (dim < QK_ROPE)
    quarter = QK_ROPE // 4
    first_half = (dim % (2 * quarter)) // quarter == 0
    perm = np.where(in_rope, np.where(first_half, lane + quarter, lane - quarter), lane)
    return perm, in_rope.astype(np.float32)


def _rope_table(seq, ident_rows):
    rows = seq // GRID_W
    row = np.repeat(np.arange(rows), GRID_W).astype(np.float32)
    col = np.tile(np.arange(GRID_W), rows).astype(np.float32)
    half = QK_ROPE // 2
    inv = (np.float32(ROPE_BASE) ** (-np.arange(0, half, 2, dtype=np.float32) / half)).astype(np.float32)
    ang = np.stack([row[:, None] * inv, col[:, None] * inv], axis=1)
    cos, sin = np.cos(ang), np.sin(ang)
    cos_r = np.stack([cos, cos], axis=2).reshape(seq, QK_ROPE)
    sin_r = np.stack([-sin, sin], axis=2).reshape(seq, QK_ROPE)

    def tile(mid, fill):
        left = np.full((seq, QK_NOPE), fill, np.float32)
        right = np.full((seq, HEAD_TILE - QK_DIM), fill, np.float32)
        return np.concatenate([left, mid, right], axis=1)

    tab = np.concatenate([tile(cos_r, 1.0), tile(sin_r, 0.0)], axis=1)
    ident = np.concatenate([np.ones((ident_rows, HEAD_TILE), np.float32),
                            np.zeros((ident_rows, HEAD_TILE), np.float32)], axis=1)
    return np.concatenate([tab, ident], axis=0).astype(np.float32)


def _prepare_params(W):
    depth, d, _ = W["w_in"].shape
    P = {}
    perm, rope_mask = _rope_partner()
    split = F_W + 2 * CONV_W + Q_LORA + KV_LORA
    w_in = W["w_in"]
    kr_tile = jnp.concatenate([jnp.zeros((depth, d, QK_NOPE), F32), w_in[:, :, split:],
                               jnp.zeros((depth, d, HEAD_TILE - QK_DIM), F32)], axis=2)
    P["w_in"] = jnp.concatenate([w_in[:, :, :split], kr_tile, kr_tile[:, :, perm] * rope_mask],
                                axis=2).astype(BF16)
    wq = W["w_q_up"].reshape(depth, Q_LORA, N_HEADS, QK_DIM)
    wq = jnp.pad(wq, ((0, 0), (0, 0), (0, 0), (0, HEAD_TILE - QK_DIM)))
    wq_rot = wq[..., perm][..., QK_NOPE:QK_DIM]
    P["w_q"] = jnp.concatenate([wq.reshape(depth, Q_LORA, HW),
                                wq_rot.reshape(depth, Q_LORA, N_HEADS * QK_ROPE)], axis=2).astype(BF16)
    wkv = W["w_kv_up"].reshape(depth, KV_LORA, N_HEADS, QK_NOPE + V_DIM)
    zk = jnp.zeros((depth, KV_LORA, N_HEADS, HEAD_TILE - QK_NOPE), F32)
    wk = jnp.concatenate([wkv[..., :QK_NOPE], zk], axis=-1)
    zv = jnp.zeros((depth, KV_LORA, N_HEADS, HEAD_TILE - V_DIM), F32)
    wv_even = jnp.concatenate([wkv[..., QK_NOPE:], zv], axis=-1)
    wv_odd = jnp.concatenate([zv, wkv[..., QK_NOPE:]], axis=-1)
    odd = (jnp.arange(N_HEADS) % 2 == 1)[None, None, :, None]
    wv = jnp.where(odd, wv_odd, wv_even)
    P["w_kv"] = jnp.concatenate([wk.reshape(depth, KV_LORA, HW), wv.reshape(depth, KV_LORA, HW)],
                                axis=2).astype(BF16)
    def head_gains(g):
        g = jnp.pad(g, ((0, 0), (0, HEAD_TILE - QK_DIM)))
        return jnp.stack([g, g[:, perm] * rope_mask], axis=1)

    P["gq"] = head_gains(W["qk_q_g"])
    P["gk"] = head_gains(W["qk_k_g"])
    vec = lambda a: a.reshape(depth, 1, a.shape[-1])
    for name in ("norm1_g", "norm2_g", "q_norm_g", "kv_norm_g", "b_gate", "conv_dw_b", "conv_ln_g",
                 "conv_ln_b", "ffn_dw_b"):
        P[name] = vec(W[name])
    P["conv_dw"] = W["conv_dw"]
    P["ffn_dw"] = W["ffn_dw"]
    for name in ("w_gate", "w_fourier", "w_conv_out", "w_mla_out", "w_out", "ffn_up", "ffn_down"):
        P[name] = W[name].astype(BF16)
    return P


def kernel(x_prompt, x_sample, cache_ckv, cache_krope, c, c_ctx, ada_w, ada_b, norm1_g, norm2_g, w_in, w_gate, b_gate, w_fourier, conv_dw, conv_dw_b, conv_ln_g, conv_ln_b, w_conv_out, q_norm_g, w_q_up, kv_norm_g, w_kv_up, qk_q_g, qk_k_g, w_mla_out, w_out, ffn_up, ffn_dw, ffn_dw_b, ffn_down):
    W = dict(norm1_g=norm1_g, norm2_g=norm2_g, w_in=w_in, w_gate=w_gate, b_gate=b_gate,
             w_fourier=w_fourier, conv_dw=conv_dw, conv_dw_b=conv_dw_b, conv_ln_g=conv_ln_g,
             conv_ln_b=conv_ln_b, w_conv_out=w_conv_out, q_norm_g=q_norm_g, w_q_up=w_q_up,
             kv_norm_g=kv_norm_g, w_kv_up=w_kv_up, qk_q_g=qk_q_g, qk_k_g=qk_k_g,
             w_mla_out=w_mla_out, w_out=w_out, ffn_up=ffn_up, ffn_dw=ffn_dw, ffn_dw_b=ffn_dw_b,
             ffn_down=ffn_down)
    n_bp, seq_p, d = x_prompt.shape
    n_bs, seq_s, _ = x_sample.shape
    depth = ada_w.shape[0]
    t_p = n_bp * seq_p
    t_s = n_bs * seq_s
    assert t_p % seq_s == 0 and seq_s % TM == 0 and TM % seq_p == 0 and seq_s % FFN_SUB == 0
    assert seq_p & (seq_p - 1) == 0 and seq_s & (seq_s - 1) == 0 and n_bs + 1 <= 8
    n_p = t_p // TM
    per_seq = seq_s // TM

    P = _prepare_params(W)
    c_cs, s_cs = _dft_tables(F_GROUP_W, 1.0 / math.sqrt(F_GROUP_W))
    P["dft_cs"] = jnp.asarray(np.concatenate([c_cs, s_cs], axis=1), dtype=BF16)
    dft_p = [jnp.asarray(t, dtype=BF16) for t in _dft_tables(seq_p, 1.0 / math.sqrt(seq_p))]
    dft_s = [jnp.asarray(t, dtype=BF16) for t in _dft_tables(seq_s, 1.0 / math.sqrt(seq_s))]
    rope_tab = jnp.asarray(_rope_table(seq_s, TM))

    cvec = jnp.concatenate([c_ctx[None, :], c, jnp.zeros((8 - 1 - n_bs, d), F32)], axis=0)
    mod = _mod_table(cvec, ada_w, ada_b).reshape(depth, 8, 1, N_MOD * d)

    kr_tile = jnp.pad(cache_krope, ((0, 0), (0, 0), (0, 0), (QK_NOPE, HEAD_TILE - QK_DIM)))
    kc, vc = _cache_kv(cache_ckv, kr_tile, P)

    x = jnp.concatenate([x_prompt.reshape(t_p, d), x_sample.reshape(t_s, d)], axis=0)
    ckv_out, kr_out = [], []
    for l in range(depth):
        fa, ug, q, k, v, ckv, kr = _in_proj(x, mod, l, P, rope_tab, n_p=n_p, per_seq=per_seq)
        ckv_out.append(ckv[:t_p].reshape(n_bp, seq_p, KV_LORA))
        kr_out.append(kr[:t_p, QK_NOPE:QK_DIM].reshape(n_bp, seq_p, QK_ROPE))
        yf = (_fourier_prompt(fa, *dft_p, t_p=t_p, seq=seq_p),
              _fourier_sample(fa, *dft_s, t_p=t_p, n_b=n_bs, seq=seq_s))
        yc = (_conv_module(ug, l, P, first=0, n_seq=n_bp, seq=seq_p),
              _conv_module(ug, l, P, first=t_p // seq_s, n_seq=n_bs, seq=seq_s))
        ya = (_attn_prompt(q, k, v, n_seq=n_bp, seq=seq_p),
              _attn_sample(q, k, v, kc, vc, l, t_p=t_p, n_b=n_bs, seq=seq_s, tq=ATTN_TQ))
        x = _merge(x, mod, l, P, yf, yc, ya, n_p=n_p, per_seq=per_seq)
        x = _ffn(x, mod, l, P, n_p=t_p // FFN_SUB, per_seq=seq_s // FFN_SUB, seq_p=seq_p, seq_s=seq_s)
    y_prompt = x[:t_p].reshape(n_bp, seq_p, d)
    y_sample = x[t_p:].reshape(n_bs, seq_s, d)
    return (y_prompt, y_sample, jnp.stack(ckv_out, axis=1), jnp.stack(kr_out, axis=1))
```

```python
import functools
import math

import numpy as np
import jax
import jax.numpy as jnp
from jax import lax
from jax.experimental import pallas as pl
from jax.experimental.pallas import tpu as pltpu

F32 = jnp.float32
BF16 = jnp.bfloat16

EPS = 1e-6
N_HEADS = 8
QK_NOPE = 64
QK_ROPE = 32
V_DIM = 64
QK_DIM = QK_NOPE + QK_ROPE
F_GROUPS = 4
F_GROUP_W = 128
F_W = F_GROUPS * F_GROUP_W
CONV_W = 512
CONV_K = 31
Q_LORA = 384
KV_LORA = 256
GRID_W = 64
ROPE_BASE = 10000.0
FFN_K = 3
N_MOD = 6

LANES = 128
SUBLANES = 8
V7X_VMEM_BYTES = 64 * 2**20
VMEM_RESERVE_BYTES = 6 * 2**20

HEAD_TILE = LANES
HW = N_HEADS * HEAD_TILE
PAIR_W = 2 * HEAD_TILE
CONV_PAD = 16

TM = 512
ATTN_TQ = 512
CONV_MAX_BLOCK_GROUPS = 16
FFN_SUB = 512
FFN_WIN = FFN_SUB + 2 * SUBLANES
FFN_CK = 256


def _in_hbm(a):
    if isinstance(a, jax.core.Tracer):
        return pltpu.with_memory_space_constraint(a, pltpu.HBM)
    return a


def _vmem_limit(block_bytes, scratch_bytes, temp_bytes):
    need = 2 * block_bytes + scratch_bytes + temp_bytes
    return int(min(need, V7X_VMEM_BYTES - VMEM_RESERVE_BYTES))


def _nbytes(shape, dtype):
    return int(np.prod(shape)) * jnp.dtype(dtype).itemsize


def _dot(a, b):
    return jnp.dot(a, b, preferred_element_type=F32)


def _dot_nt(a, b):
    return lax.dot_general(a, b, (((1,), (1,)), ((), ())), preferred_element_type=F32)


def _rms(x):
    return x * lax.rsqrt(jnp.mean(x * x, axis=-1, keepdims=True) + EPS)


def _sigmoid(x):
    return 1.0 / (1.0 + jnp.exp(-x))


def _silu(x):
    return x * _sigmoid(x)


def _mod_kernel(c_ref, w_ref, b_ref, o_ref):
    s = _silu(c_ref[...]).astype(BF16)
    o_ref[...] = _dot(s, w_ref[...].astype(BF16)) + b_ref[...]


def _mod_table(cvec, ada_w, ada_b):
    depth, d, n = ada_w.shape
    rows = cvec.shape[0]
    tn = n // 4
    blocks = _nbytes((d, tn), F32) + _nbytes((rows, tn), F32) * 2
    return pl.pallas_call(
        _mod_kernel,
        grid=(depth, n // tn),
        in_specs=[
            pl.BlockSpec((rows, d), lambda l, j: (0, 0)),
            pl.BlockSpec((None, d, tn), lambda l, j: (l, 0, j)),
            pl.BlockSpec((None, 1, tn), lambda l, j: (l, 0, j)),
        ],
        out_specs=pl.BlockSpec((None, rows, tn), lambda l, j: (l, 0, j)),
        out_shape=jax.ShapeDtypeStruct((depth, rows, n), F32),
        compiler_params=pltpu.CompilerParams(
            dimension_semantics=("arbitrary", "arbitrary"),
            vmem_limit_bytes=_vmem_limit(blocks, 0, _nbytes((d, tn), BF16) + 2**20)),
        name="adaln_table",
    )(cvec, ada_w, ada_b.reshape(depth, 1, n))


def _head_inv_rms(t):
    ss = jnp.sum(t * t, axis=-1, keepdims=True)
    return lax.rsqrt(ss * (1.0 / QK_DIM) + EPS)


def _keys_values(cb, kr, wkv_ref, k_ref, vt_ref, finish_key):
    lane = lax.broadcasted_iota(jnp.int32, (1, PAIR_W), 1)
    ones_row = jnp.logical_or(lane == V_DIM, lane == HEAD_TILE).astype(F32)
    for pair in range(N_HEADS // 2):
        cols = slice(pair * PAIR_W, (pair + 1) * PAIR_W)
        t2 = _dot(cb, wkv_ref[:, cols])
        for hh in range(2):
            sl = slice((2 * pair + hh) * HEAD_TILE, (2 * pair + hh + 1) * HEAD_TILE)
            k_ref[:, sl] = finish_key(t2[:, hh * HEAD_TILE:(hh + 1) * HEAD_TILE] + kr).astype(BF16)
        v2 = _dot(cb, wkv_ref[:, HW + pair * PAIR_W:HW + (pair + 1) * PAIR_W]) + ones_row
        vt_ref[cols, :] = v2.T.astype(BF16)


def _in_kernel(x_ref, mod_ref, g1_ref, win_ref, cs_ref, qg_ref, wq_ref, kvg_ref, wkv_ref,
               gq_ref, gk_ref, rope_ref,
               fa_ref, ug_ref, q_ref, k_ref, vt_ref, ckv_ref, kr_ref, *, d):
    sh1 = mod_ref[:, 0:d]
    sc1 = mod_ref[:, d:2 * d]
    h = _rms(x_ref[...]) * g1_ref[...] * (1.0 + sc1) + sh1
    hb = h.astype(BF16)
    c_conv = F_W
    c_q = c_conv + 2 * CONV_W
    c0 = c_q

    cos = rope_ref[:, 0:HEAD_TILE]
    sin = rope_ref[:, HEAD_TILE:2 * HEAD_TILE]

    qn = (_rms(_dot(hb, win_ref[:, c0:c0 + Q_LORA])) * qg_ref[...]).astype(BF16)
    c0 += Q_LORA
    scale = 1.0 / math.sqrt(QK_DIM)
    q_cos = cos * (gq_ref[0:1, :] * scale)
    q_sin = sin * (gq_ref[1:2, :] * scale)
    partners = _dot(qn, wq_ref[:, HW:HW + N_HEADS * QK_ROPE])
    per_tile = HEAD_TILE // QK_ROPE
    for pair in range(N_HEADS // 2):
        t2 = _dot(qn, wq_ref[:, pair * PAIR_W:(pair + 1) * PAIR_W])
        for hd in (2 * pair, 2 * pair + 1):
            sl = slice(hd * HEAD_TILE, (hd + 1) * HEAD_TILE)
            t = t2[:, (hd % 2) * HEAD_TILE:(hd % 2 + 1) * HEAD_TILE]
            tp = partners[:, (hd // per_tile) * HEAD_TILE:(hd // per_tile + 1) * HEAD_TILE]
            shift = (QK_NOPE - (hd % per_tile) * QK_ROPE) % HEAD_TILE
            if shift:
                tp = pltpu.roll(tp, shift, axis=1)
            q_ref[:, sl] = ((t * q_cos + tp * q_sin) * _head_inv_rms(t)).astype(BF16)

    ckv = _rms(_dot(hb, win_ref[:, c0:c0 + KV_LORA])) * kvg_ref[...]
    ckv_ref[...] = ckv
    c0 += KV_LORA
    kr = _dot(hb, win_ref[:, c0:c0 + HEAD_TILE])
    kr_ref[...] = kr
    k_cos = cos * gk_ref[0:1, :]
    kr_rot = _dot(hb, win_ref[:, c0 + HEAD_TILE:c0 + 2 * HEAD_TILE]) * (sin * gk_ref[1:2, :])
    _keys_values(ckv.astype(BF16), kr, wkv_ref, k_ref, vt_ref,
                 lambda t: (t * k_cos + kr_rot) * _head_inv_rms(t))

    fb = _dot(hb, win_ref[:, 0:F_W]).astype(BF16)
    for g in range(F_GROUPS):
        a = _dot(fb[:, g * F_GROUP_W:(g + 1) * F_GROUP_W], cs_ref[...])
        fa_ref[:, g * F_GROUP_W:(g + 1) * F_GROUP_W] = a[:, :F_GROUP_W].astype(BF16)
        fa_ref[:, F_W + g * F_GROUP_W:F_W + (g + 1) * F_GROUP_W] = a[:, F_GROUP_W:].astype(BF16)

    ga = _dot(hb, win_ref[:, c_conv:c_conv + CONV_W])
    gg = _dot(hb, win_ref[:, c_conv + CONV_W:c_conv + 2 * CONV_W])
    ug_ref[...] = ga * _sigmoid(gg)


def _in_proj(x, mod, l, P, rope_tab, *, n_p, per_seq):
    t_rows, d = x.shape
    n_tiles = t_rows // TM
    win_cols = P["w_in"].shape[-1]

    def mod_idx(i):
        return (l, jnp.where(i < n_p, 0, 1 + (i - n_p) // per_seq), 0, 0)

    def rope_idx(i):
        return (jnp.where(i < n_p, per_seq, (i - n_p) % per_seq), 0)

    row = lambda w: pl.BlockSpec((TM, w), lambda i: (i, 0))
    lay = lambda *s: pl.BlockSpec((None,) + s, lambda i: (l,) + (0,) * len(s))
    wq_cols = P["w_q"].shape[-1]
    out_shapes = [
        jax.ShapeDtypeStruct((t_rows, 2 * F_W), BF16),
        jax.ShapeDtypeStruct((t_rows, CONV_W), F32),
        jax.ShapeDtypeStruct((t_rows, HW), BF16),
        jax.ShapeDtypeStruct((t_rows, HW), BF16),
        jax.ShapeDtypeStruct((HW, t_rows), BF16),
        jax.ShapeDtypeStruct((t_rows, KV_LORA), F32),
        jax.ShapeDtypeStruct((t_rows, HEAD_TILE), F32),
    ]
    out_specs = [pl.BlockSpec((HW, TM), lambda i: (0, i)) if s.shape[0] == HW else row(s.shape[1])
                 for s in out_shapes]
    blocks = (_nbytes((TM, d), F32) + _nbytes((d, win_cols), BF16) + _nbytes((Q_LORA, wq_cols), BF16)
              + _nbytes((KV_LORA, 2 * HW), BF16) + _nbytes((TM, 2 * HEAD_TILE), F32)
              + sum(_nbytes(s.shape, s.dtype) // (t_rows // TM) for s in out_shapes))
    return pl.pallas_call(
        functools.partial(_in_kernel, d=d),
        grid=(n_tiles,),
        in_specs=[
            row(d),
            pl.BlockSpec((None, None, 1, N_MOD * d), mod_idx),
            lay(1, d),
            lay(d, win_cols),
            pl.BlockSpec((F_GROUP_W, 2 * F_GROUP_W), lambda i: (0, 0)),
            lay(1, Q_LORA),
            lay(Q_LORA, wq_cols),
            lay(1, KV_LORA),
            lay(KV_LORA, 2 * HW),
            lay(2, HEAD_TILE),
            lay(2, HEAD_TILE),
            pl.BlockSpec((TM, 2 * HEAD_TILE), rope_idx),
        ],
        out_specs=out_specs,
        out_shape=out_shapes,
        compiler_params=pltpu.CompilerParams(
            dimension_semantics=("arbitrary",),
            vmem_limit_bytes=_vmem_limit(blocks, 0, 12 * _nbytes((TM, d), F32))),
        name="in_proj",
    )(x, mod, P["norm1_g"], P["w_in"], P["dft_cs"], P["q_norm_g"], P["w_q"], P["kv_norm_g"],
      P["w_kv"], P["gq"], P["gk"], rope_tab)


def _cache_kv_kernel(ckv_ref, kr_ref, wkv_ref, gk_ref, k_ref, vt_ref):
    k_gain = gk_ref[0:1, :]
    _keys_values(ckv_ref[...].astype(BF16), kr_ref[...], wkv_ref, k_ref, vt_ref,
                 lambda t: t * k_gain * _head_inv_rms(t))


def _cache_kv(cache_ckv, cache_kr_tile, P):
    nb, depth, past, _ = cache_ckv.shape
    blk = lambda w: pl.BlockSpec((None, None, past, w), lambda b, l: (b, l, 0, 0))
    lay = lambda *s: pl.BlockSpec((None,) + s, lambda b, l: (l,) + (0,) * len(s))
    blocks = (_nbytes((past, KV_LORA + HEAD_TILE), F32) + _nbytes((KV_LORA, 2 * HW), BF16)
              + 2 * _nbytes((past, HW), BF16))
    return pl.pallas_call(
        _cache_kv_kernel,
        grid=(nb, depth),
        in_specs=[blk(KV_LORA), blk(HEAD_TILE), lay(KV_LORA, 2 * HW), lay(2, HEAD_TILE)],
        out_specs=[blk(HW), pl.BlockSpec((None, None, HW, past), lambda b, l: (b, l, 0, 0))],
        out_shape=[jax.ShapeDtypeStruct((nb, depth, past, HW), BF16),
                   jax.ShapeDtypeStruct((nb, depth, HW, past), BF16)],
        compiler_params=pltpu.CompilerParams(
            dimension_semantics=("arbitrary", "arbitrary"),
            vmem_limit_bytes=_vmem_limit(blocks, 0, 8 * _nbytes((past, HW), F32))),
        name="cache_kv",
    )(cache_ckv, cache_kr_tile, P["w_kv"], P["gk"])


def _fourier_kernel(fa_ref, c_ref, s_ref, o_ref, *, seq, n_seq):
    for s in range(n_seq):
        rows = slice(s * seq, (s + 1) * seq)
        y = _dot(c_ref[...], fa_ref[rows, 0:F_W]) - _dot(s_ref[...], fa_ref[rows, F_W:2 * F_W])
        o_ref[rows, :] = y.astype(BF16)


def _fourier_prompt(fa, dft_c, dft_s, *, t_p, seq):
    n_seq = TM // seq
    blocks = _nbytes((TM, 2 * F_W), BF16) + 2 * _nbytes((seq, seq), BF16) + _nbytes((TM, F_W), BF16)
    return pl.pallas_call(
        functools.partial(_fourier_kernel, seq=seq, n_seq=n_seq),
        grid=(t_p // TM,),
        in_specs=[
            pl.BlockSpec((TM, 2 * F_W), lambda i: (i, 0)),
            pl.BlockSpec((seq, seq), lambda i: (0, 0)),
            pl.BlockSpec((seq, seq), lambda i: (0, 0)),
        ],
        out_specs=pl.BlockSpec((TM, F_W), lambda i: (i, 0)),
        out_shape=jax.ShapeDtypeStruct((t_p, F_W), BF16),
        compiler_params=pltpu.CompilerParams(
            dimension_semantics=("arbitrary",),
            vmem_limit_bytes=_vmem_limit(blocks, 0, 4 * _nbytes((TM, F_W), F32))),
        name="fourier_prompt",
    )(fa, dft_c, dft_s)


def _fourier_sample_kernel(fa_ref, c_ref, s_ref, o_ref):
    y = _dot(c_ref[...], fa_ref[:, 0:F_W]) - _dot(s_ref[...], fa_ref[:, F_W:2 * F_W])
    o_ref[...] = y.astype(BF16)


def _fourier_sample(fa, dft_c, dft_s, *, t_p, n_b, seq):
    tr = TM
    per = seq // tr
    first = t_p // seq
    blocks = _nbytes((seq, 2 * F_W), BF16) + 2 * _nbytes((tr, seq), BF16) + _nbytes((tr, F_W), BF16)
    return pl.pallas_call(
        _fourier_sample_kernel,
        grid=(n_b, per),
        in_specs=[
            pl.BlockSpec((seq, 2 * F_W), lambda b, j: (first + b, 0)),
            pl.BlockSpec((tr, seq), lambda b, j: (j, 0)),
            pl.BlockSpec((tr, seq), lambda b, j: (j, 0)),
        ],
        out_specs=pl.BlockSpec((tr, F_W), lambda b, j: (b * per + j, 0)),
        out_shape=jax.ShapeDtypeStruct((n_b * seq, F_W), BF16),
        compiler_params=pltpu.CompilerParams(
            dimension_semantics=("arbitrary", "arbitrary"),
            vmem_limit_bytes=_vmem_limit(blocks, 0, 4 * _nbytes((tr, F_W), F32))),
        name="fourier_sample",
    )(fa, dft_c, dft_s)


def _conv_block_groups(n_grp):
    return max(g for g in range(1, CONV_MAX_BLOCK_GROUPS + 1) if n_grp % (2 * g) == 0)


def _conv_kernel(u_ref, dw_ref, dwb_ref, lng_ref, lnb_ref, o_ref, nat_ref, ext_ref, *, seq):
    half = CONV_K // 2
    win = seq + 2 * CONV_PAD
    n_grp = win // SUBLANES
    n_slab = CONV_W // LANES
    lead = half * SUBLANES
    zeros = jnp.zeros((CONV_PAD, LANES), F32)
    for k in range(n_slab):
        nat_ref[k, 0:CONV_PAD, :] = zeros
        nat_ref[k, CONV_PAD + seq:win, :] = zeros
        nat_ref[k, CONV_PAD:CONV_PAD + seq, :] = u_ref[:, k * LANES:(k + 1) * LANES]
    for k in range(n_slab):
        for v in range(n_grp):
            g = nat_ref[k, pl.ds(v, SUBLANES, stride=n_grp), :]
            ext_ref[k, lead + v * SUBLANES:lead + (v + 1) * SUBLANES, :] = g
            if v >= n_grp - half:
                j = v - (n_grp - half)
                ext_ref[k, j * SUBLANES:(j + 1) * SUBLANES, :] = pltpu.roll(g, 1, axis=0)
            if v < half:
                j = half + n_grp + v
                ext_ref[k, j * SUBLANES:(j + 1) * SUBLANES, :] = pltpu.roll(g, SUBLANES - 1, axis=0)

    blk_grp = _conv_block_groups(n_grp)
    blk = blk_grp * SUBLANES
    n_blk = n_grp // blk_grp

    for k in range(n_slab):
        lanes = slice(k * LANES, (k + 1) * LANES)

        def conv_body(b, carry, k=k, lanes=lanes):
            r0 = pl.multiple_of(b * blk, SUBLANES)
            acc = jnp.zeros((blk, LANES), F32)
            for t in range(CONV_K):
                acc = acc + dw_ref[t:t + 1, lanes] * ext_ref[k, pl.ds(r0 + t * SUBLANES, blk), :]
            ext_ref[k, pl.ds(r0, blk), :] = acc + dwb_ref[:, lanes]
            return carry

        lax.fori_loop(0, n_blk, conv_body, 0)

    def norm_block(r0):
        acc = jnp.concatenate([ext_ref[k, pl.ds(r0, blk), :] for k in range(n_slab)], axis=1)
        mu = jnp.mean(acc, axis=-1, keepdims=True)
        cen = acc - mu
        var = jnp.mean(cen * cen, axis=-1, keepdims=True)
        y = _silu(cen * lax.rsqrt(var + EPS) * lng_ref[...] + lnb_ref[...])
        for k in range(n_slab):
            ext_ref[k, pl.ds(r0, blk), :] = y[:, k * LANES:(k + 1) * LANES]

    def norm_body(b, carry):
        norm_block(pl.multiple_of(b * blk, SUBLANES))
        norm_block(pl.multiple_of((b + n_blk // 2) * blk, SUBLANES))
        return carry

    lax.fori_loop(0, n_blk // 2, norm_body, 0)

    for k in range(n_slab):
        for v in range(n_grp):
            nat_ref[k, pl.ds(v, SUBLANES, stride=n_grp), :] = ext_ref[k, v * SUBLANES:(v + 1) * SUBLANES, :]
        o_ref[:, k * LANES:(k + 1) * LANES] = nat_ref[k, CONV_PAD:CONV_PAD + seq, :].astype(BF16)


def _conv_module(ug, l, P, *, first, n_seq, seq):
    lay = lambda *s: pl.BlockSpec((None,) + s, lambda i: (l,) + (0,) * len(s))
    blocks = _nbytes((seq, CONV_W), F32) + _nbytes((seq, CONV_W), BF16) + _nbytes((CONV_K + 4, CONV_W), F32)
    win = seq + 2 * CONV_PAD
    ext = win + 2 * (CONV_K // 2) * SUBLANES
    scratch = _nbytes((win + ext, CONV_W), F32)
    return pl.pallas_call(
        functools.partial(_conv_kernel, seq=seq),
        grid=(n_seq,),
        in_specs=[
            pl.BlockSpec((seq, CONV_W), lambda i: (first + i, 0)),
            lay(CONV_K, CONV_W), lay(1, CONV_W), lay(1, CONV_W), lay(1, CONV_W),
        ],
        out_specs=pl.BlockSpec((seq, CONV_W), lambda i: (i, 0)),
        out_shape=jax.ShapeDtypeStruct((n_seq * seq, CONV_W), BF16),
        scratch_shapes=[pltpu.VMEM((CONV_W // LANES, win, LANES), F32),
                        pltpu.VMEM((CONV_W // LANES, ext, LANES), F32)],
        compiler_params=pltpu.CompilerParams(
            dimension_semantics=("arbitrary",),
            vmem_limit_bytes=_vmem_limit(blocks, scratch, 4 * 2**20)),
        name=f"conv_seq{seq}",
    )(ug, P["conv_dw"], P["conv_dw_b"], P["conv_ln_g"], P["conv_ln_b"])


def _attn_kernel(*refs, with_cache):
    if with_cache:
        q_ref, k_ref, vt_ref, kc_ref, vct_ref, o_ref = refs
    else:
        q_ref, k_ref, vt_ref, o_ref = refs
    tq = q_ref.shape[0]
    row = lax.broadcasted_iota(jnp.int32, (HEAD_TILE, tq), 0)

    def scores(hd):
        sl = slice(hd * HEAD_TILE, (hd + 1) * HEAD_TILE)
        qh = q_ref[:, sl]
        return _dot_nt(k_ref[:, sl], qh), (_dot_nt(kc_ref[:, sl], qh) if with_cache else None)

    nxt = scores(0)
    even = None
    for hd in range(N_HEADS):
        sl = slice(hd * HEAD_TILE, (hd + 1) * HEAD_TILE)
        s_m, s_c = nxt
        if hd + 1 < N_HEADS:
            nxt = scores(hd + 1)
        m = jnp.max(s_m, axis=0, keepdims=True)
        if with_cache:
            m = jnp.maximum(m, jnp.max(s_c, axis=0, keepdims=True))
        out_t = _dot(vt_ref[sl, :], jnp.exp(s_m - m).astype(BF16))
        if with_cache:
            out_t = out_t + _dot(vct_ref[sl, :], jnp.exp(s_c - m).astype(BF16))
        den_row = V_DIM if hd % 2 == 0 else 0
        out_t = out_t * (1.0 / out_t[den_row:den_row + 1, :])
        if hd % 2 == 0:
            even = out_t
        else:
            pair_t = jnp.where(row < V_DIM, even, out_t)
            o_ref[:, (hd // 2) * HEAD_TILE:(hd // 2 + 1) * HEAD_TILE] = pair_t.T.astype(BF16)


def _attn_prompt(q, k, vt, *, n_seq, seq):
    blk = pl.BlockSpec((seq, HW), lambda i: (i, 0))
    blocks = 3 * _nbytes((seq, HW), BF16) + _nbytes((seq, N_HEADS * V_DIM), BF16)
    return pl.pallas_call(
        functools.partial(_attn_kernel, with_cache=False),
        grid=(n_seq,),
        in_specs=[blk, blk, pl.BlockSpec((HW, seq), lambda i: (0, i))],
        out_specs=pl.BlockSpec((seq, N_HEADS * V_DIM), lambda i: (i, 0)),
        out_shape=jax.ShapeDtypeStruct((n_seq * seq, N_HEADS * V_DIM), BF16),
        compiler_params=pltpu.CompilerParams(
            dimension_semantics=("arbitrary",),
            vmem_limit_bytes=_vmem_limit(blocks, 0, 8 * _nbytes((seq, seq), F32) + 2**21)),
        name="attn_prompt",
    )(q, k, vt)


def _attn_sample(q, k, vt, kc, vct, l, *, t_p, n_b, seq, tq):
    per = seq // tq
    first_q = t_p // tq
    first_s = t_p // seq
    past = kc.shape[2]
    once = pl.Buffered(1)
    k_blk = pl.BlockSpec((seq, HW), lambda b, j: (first_s + b, 0), pipeline_mode=once)
    vt_blk = pl.BlockSpec((HW, seq), lambda b, j: (0, first_s + b), pipeline_mode=once)
    kc_blk = pl.BlockSpec((None, None, past, HW), lambda b, j: (b, l, 0, 0), pipeline_mode=once)
    vct_blk = pl.BlockSpec((None, None, HW, past), lambda b, j: (b, l, 0, 0), pipeline_mode=once)
    blocks = _nbytes((tq, HW), BF16) + _nbytes((tq, N_HEADS * V_DIM), BF16)
    resident = 2 * _nbytes((seq + past, HW), BF16)
    return pl.pallas_call(
        functools.partial(_attn_kernel, with_cache=True),
        grid=(n_b, per),
        in_specs=[pl.BlockSpec((tq, HW), lambda b, j: (first_q + b * per + j, 0)),
                  k_blk, vt_blk, kc_blk, vct_blk],
        out_specs=pl.BlockSpec((tq, N_HEADS * V_DIM), lambda b, j: (b * per + j, 0)),
        out_shape=jax.ShapeDtypeStruct((n_b * seq, N_HEADS * V_DIM), BF16),
        compiler_params=pltpu.CompilerParams(
            dimension_semantics=("arbitrary", "arbitrary"),
            vmem_limit_bytes=_vmem_limit(blocks, resident, 8 * _nbytes((tq, seq + past), F32))),
        name="attn_sample",
    )(q, k, vt, kc, vct)


def _merge_kernel(x_ref, mod_ref, g1_ref, fp_ref, fs_ref, cp_ref, cs_ref, ap_ref, as_ref,
                  wg_ref, bg_ref, wf_ref, wc_ref, wa_ref, wo_ref, o_ref, *, d, n_p):
    is_prompt = pl.program_id(0) < n_p
    x = x_ref[...]
    sh1 = mod_ref[:, 0:d]
    sc1 = mod_ref[:, d:2 * d]
    g1 = mod_ref[:, 2 * d:3 * d]
    hb = (_rms(x) * g1_ref[...] * (1.0 + sc1) + sh1).astype(BF16)
    merged = None
    branches = ((fp_ref, fs_ref, wf_ref), (cp_ref, cs_ref, wc_ref), (ap_ref, as_ref, wa_ref))
    for b, (yp_ref, ys_ref, w_ref) in enumerate(branches):
        y_in = jnp.where(is_prompt, yp_ref[...], ys_ref[...])
        y = _dot(y_in, w_ref[...])
        gate = _sigmoid(_dot(hb, wg_ref[:, b * d:(b + 1) * d]) + bg_ref[:, b * d:(b + 1) * d])
        merged = gate * y if merged is None else merged + gate * y
    o_ref[...] = x + g1 * _dot(merged.astype(BF16), wo_ref[...])


def _merge(x, mod, l, P, yf, yc, ya, *, n_p, per_seq):
    t_rows, d = x.shape
    n_tiles = t_rows // TM
    n_s = n_tiles - n_p

    def mod_idx(i):
        return (l, jnp.where(i < n_p, 0, 1 + (i - n_p) // per_seq), 0, 0)

    row = pl.BlockSpec((TM, d), lambda i: (i, 0))
    lay = lambda *s: pl.BlockSpec((None,) + s, lambda i: (l,) + (0,) * len(s))
    p_blk = pl.BlockSpec((TM, F_W), lambda i: (jnp.minimum(i, n_p - 1), 0))
    s_blk = pl.BlockSpec((TM, F_W), lambda i: (jnp.clip(i - n_p, 0, n_s - 1), 0))
    blocks = (2 * _nbytes((TM, d), F32) + 6 * _nbytes((TM, F_W), BF16) + _nbytes((d, 3 * d), BF16)
              + 3 * _nbytes((F_W, d), BF16) + _nbytes((d, d), BF16))
    return pl.pallas_call(
        functools.partial(_merge_kernel, d=d, n_p=n_p),
        grid=(n_tiles,),
        in_specs=[row, pl.BlockSpec((None, None, 1, N_MOD * d), mod_idx), lay(1, d),
                  p_blk, s_blk, p_blk, s_blk, p_blk, s_blk,
                  lay(d, 3 * d), lay(1, 3 * d), lay(F_W, d), lay(CONV_W, d), lay(N_HEADS * V_DIM, d),
                  lay(d, d)],
        out_specs=row,
        out_shape=jax.ShapeDtypeStruct((t_rows, d), F32),
        compiler_params=pltpu.CompilerParams(
            dimension_semantics=("arbitrary",),
            vmem_limit_bytes=_vmem_limit(blocks, 0, 8 * _nbytes((TM, d), F32))),
        name="merge",
    )(x, mod, P["norm1_g"], yf[0], yf[1], yc[0], yc[1], ya[0], ya[1],
      P["w_gate"], P["b_gate"], P["w_fourier"], P["w_conv_out"], P["w_mla_out"], P["w_out"])


def _ffn_kernel(x_ref, xp_ref, xn_ref, mod_ref, g2_ref, up_ref, dw_ref, b_ref, dn_ref,
                o_ref, nat_ref, h_ref, act_ref, *, d, d_ff, n_p, seq_p, seq_s):
    i = pl.program_id(0)
    sh2 = mod_ref[:, 3 * d:4 * d]
    sc2 = mod_ref[:, 4 * d:5 * d]
    g2 = mod_ref[:, 5 * d:6 * d]

    def norm_mod(x):
        return _rms(x) * g2_ref[...] * (1.0 + sc2) + sh2

    is_prompt = i < n_p
    period_mask = jnp.where(is_prompt, seq_p - 1, seq_s - 1)
    has_prev = ((i * FFN_SUB) & period_mask) != 0
    has_next = (((i + 1) * FFN_SUB) & period_mask) != 0
    h_prev = jnp.where(has_prev, norm_mod(xp_ref[...]), 0.0)
    h_next = jnp.where(has_next, norm_mod(xn_ref[...]), 0.0)
    h_main = norm_mod(x_ref[...])

    n_grp = FFN_WIN // SUBLANES
    for k in range(d // LANES):
        lanes = slice(k * LANES, (k + 1) * LANES)
        nat_ref[k, 0:SUBLANES, :] = h_prev[:, lanes]
        nat_ref[k, SUBLANES:SUBLANES + FFN_SUB, :] = h_main[:, lanes]
        nat_ref[k, SUBLANES + FFN_SUB:FFN_WIN, :] = h_next[:, lanes]
    for k in range(d // LANES):
        groups = [nat_ref[k, pl.ds(v, SUBLANES, stride=n_grp), :] for v in range(n_grp)]
        h_ref[:, k * LANES:(k + 1) * LANES] = jnp.concatenate(groups, axis=0).astype(BF16)

    sub = lax.broadcasted_iota(jnp.int32, (SUBLANES, 1), 0)
    no_prev = jnp.zeros((SUBLANES, 1), jnp.bool_)
    no_next = jnp.zeros((SUBLANES, 1), jnp.bool_)
    for edge in range(seq_p, FFN_SUB, seq_p):
        assert (edge + SUBLANES) % n_grp == 0
        no_prev = jnp.logical_or(no_prev, sub == (edge + SUBLANES) // n_grp)
        no_next = jnp.logical_or(no_next, sub == (edge + SUBLANES - 1) // n_grp)
    no_prev = jnp.logical_and(is_prompt, no_prev)
    no_next = jnp.logical_and(is_prompt, no_next)

    def conv(u, c0):
        last = u[FFN_WIN - SUBLANES:FFN_WIN, :]
        first = u[0:SUBLANES, :]
        wrap_prev = jnp.where(no_prev, 0.0, pltpu.roll(last, 1, axis=0))
        wrap_next = jnp.where(no_next, 0.0, pltpu.roll(first, SUBLANES - 1, axis=0))
        prev = jnp.concatenate([wrap_prev, u[0:FFN_WIN - SUBLANES, :]], axis=0)
        nxt = jnp.concatenate([u[SUBLANES:FFN_WIN, :], wrap_next], axis=0)
        w = dw_ref[:, c0:c0 + FFN_CK]
        return w[0:1, :] * prev + w[1:2, :] * u + w[2:3, :] * nxt + b_ref[:, c0:c0 + FFN_CK]

    hx = h_ref[...]

    def up_proj(c):
        return (_dot(hx, up_ref[:, c * FFN_CK:(c + 1) * FFN_CK]),
                _dot(hx, up_ref[:, d_ff + c * FFN_CK:d_ff + (c + 1) * FFN_CK]))

    n_ck = d_ff // FFN_CK
    half_ck = (n_ck + 1) // 2
    part = None
    nxt = up_proj(0)
    for c in range(n_ck):
        u_a, u_b = nxt
        if c + 1 < n_ck:
            nxt = up_proj(c + 1)
        if c == half_ck:
            part = _dot(act_ref[:, 0:half_ck * FFN_CK], dn_ref[0:half_ck * FFN_CK, :])
        act = _silu(conv(u_a, c * FFN_CK)) * conv(u_b, d_ff + c * FFN_CK)
        act_ref[:, c * FFN_CK:(c + 1) * FFN_CK] = act.astype(BF16)
    y = part + _dot(act_ref[:, half_ck * FFN_CK:d_ff], dn_ref[half_ck * FFN_CK:d_ff, :])
    for k in range(d // LANES):
        lanes = slice(k * LANES, (k + 1) * LANES)
        for v in range(n_grp):
            nat_ref[k, pl.ds(v, SUBLANES, stride=n_grp), :] = y[v * SUBLANES:(v + 1) * SUBLANES, lanes]
        o_ref[:, lanes] = x_ref[:, lanes] + g2[:, lanes] * nat_ref[k, SUBLANES:SUBLANES + FFN_SUB, :]


def _ffn(x, mod, l, P, *, n_p, per_seq, seq_p, seq_s):
    t_rows, d = x.shape
    d_ff = P["ffn_down"].shape[1]
    n_tiles = t_rows // FFN_SUB
    halo_per_tile = FFN_SUB // SUBLANES
    n_halo = t_rows // SUBLANES

    def mod_idx(i):
        return (l, jnp.where(i < n_p, 0, 1 + (i - n_p) // per_seq), 0, 0)

    row = pl.BlockSpec((FFN_SUB, d), lambda i: (i, 0))
    once = pl.Buffered(1)
    lay = lambda *s: pl.BlockSpec((None,) + s, lambda i: (l,) + (0,) * len(s))
    res = lambda *s: pl.BlockSpec((None,) + s, lambda i: (l,) + (0,) * len(s), pipeline_mode=once)
    weights = _nbytes((d, 2 * d_ff), BF16) + _nbytes((d_ff, d), BF16)
    blocks = 2 * _nbytes((FFN_SUB + 2 * SUBLANES, d), F32) + 4 * _nbytes((8, 2 * d_ff), F32)
    scratch = (_nbytes((FFN_WIN, d), F32) + _nbytes((FFN_WIN, d), BF16) + _nbytes((FFN_WIN, d_ff), BF16))
    temps = 4 * _nbytes((FFN_WIN, d), F32) + 8 * _nbytes((FFN_WIN, FFN_CK), F32)
    return pl.pallas_call(
        functools.partial(_ffn_kernel, d=d, d_ff=d_ff, n_p=n_p, seq_p=seq_p, seq_s=seq_s),
        grid=(n_tiles,),
        in_specs=[
            row,
            pl.BlockSpec((SUBLANES, d), lambda i: (jnp.maximum(i * halo_per_tile - 1, 0), 0)),
            pl.BlockSpec((SUBLANES, d), lambda i: (jnp.minimum((i + 1) * halo_per_tile, n_halo - 1), 0)),
            pl.BlockSpec((None, None, 1, N_MOD * d), mod_idx),
            lay(1, d),
            res(d, 2 * d_ff),
            lay(FFN_K, 2 * d_ff),
            lay(1, 2 * d_ff),
            res(d_ff, d),
        ],
        out_specs=row,
        out_shape=jax.ShapeDtypeStruct((t_rows, d), F32),
        scratch_shapes=[pltpu.VMEM((d // LANES, FFN_WIN, LANES), F32), pltpu.VMEM((FFN_WIN, d), BF16),
                        pltpu.VMEM((FFN_WIN, d_ff), BF16)],
        compiler_params=pltpu.CompilerParams(
            dimension_semantics=("arbitrary",),
            vmem_limit_bytes=_vmem_limit(blocks, scratch + weights, temps)),
        name="conv_ffn",
    )(x, x, x, mod, P["norm2_g"], P["ffn_up"], P["ffn_dw"], P["ffn_dw_b"], P["ffn_down"])


def _dft_tables(n, scale):
    j = np.arange(n, dtype=np.int64)
    ang = ((j[:, None] * j[None, :]) % n).astype(np.float64) * (2.0 * math.pi / n)
    return (np.cos(ang) * scale).astype(np.float32), (np.sin(ang) * scale).astype(np.float32)


def _rope_partner():
    lane = np.arange(HEAD_TILE)
    dim = lane - QK_NOPE
    in_rope = (dim >= 0) & (dim < QK_ROPE)
    quarter = QK_ROPE // 4
    first_half = (dim % (2 * quarter)) // quarter == 0
    perm = np.where(in_rope, np.where(first_half, lane + quarter, lane - quarter), lane)
    return perm, in_rope.astype(np.float32)


def _rope_table(seq, ident_rows):
    rows = seq // GRID_W
    row = np.repeat(np.arange(rows), GRID_W).astype(np.float32)
    col = np.tile(np.arange(GRID_W), rows).astype(np.float32)
    half = QK_ROPE // 2
    inv = (np.float32(ROPE_BASE) ** (-np.arange(0, half, 2, dtype=np.float32) / half)).astype(np.float32)
    ang = np.stack([row[:, None] * inv, col[:, None] * inv], axis=1)
    cos, sin = np.cos(ang), np.sin(ang)
    cos_r = np.stack([cos, cos], axis=2).reshape(seq, QK_ROPE)
    sin_r = np.stack([-sin, sin], axis=2).reshape(seq, QK_ROPE)

    def tile(mid, fill):
        left = np.full((seq, QK_NOPE), fill, np.float32)
        right = np.full((seq, HEAD_TILE - QK_DIM), fill, np.float32)
        return np.concatenate([left, mid, right], axis=1)

    tab = np.concatenate([tile(cos_r, 1.0), tile(sin_r, 0.0)], axis=1)
    ident = np.concatenate([np.ones((ident_rows, HEAD_TILE), np.float32),
                            np.zeros((ident_rows, HEAD_TILE), np.float32)], axis=1)
    return np.concatenate([tab, ident], axis=0).astype(np.float32)


def _prepare_params(W):
    depth, d, _ = W["w_in"].shape
    P = {}
    perm, rope_mask = _rope_partner()
    split = F_W + 2 * CONV_W + Q_LORA + KV_LORA
    w_in = W["w_in"]
    kr_tile = jnp.concatenate([jnp.zeros((depth, d, QK_NOPE), F32), w_in[:, :, split:],
                               jnp.zeros((depth, d, HEAD_TILE - QK_DIM), F32)], axis=2)
    P["w_in"] = jnp.concatenate([w_in[:, :, :split], kr_tile, kr_tile[:, :, perm] * rope_mask],
                                axis=2).astype(BF16)
    wq = W["w_q_up"].reshape(depth, Q_LORA, N_HEADS, QK_DIM)
    wq = jnp.pad(wq, ((0, 0), (0, 0), (0, 0), (0, HEAD_TILE - QK_DIM)))
    wq_rot = wq[..., perm][..., QK_NOPE:QK_DIM]
    P["w_q"] = jnp.concatenate([wq.reshape(depth, Q_LORA, HW),
                                wq_rot.reshape(depth, Q_LORA, N_HEADS * QK_ROPE)], axis=2).astype(BF16)
    wkv = W["w_kv_up"].reshape(depth, KV_LORA, N_HEADS, QK_NOPE + V_DIM)
    zk = jnp.zeros((depth, KV_LORA, N_HEADS, HEAD_TILE - QK_NOPE), F32)
    wk = jnp.concatenate([wkv[..., :QK_NOPE], zk], axis=-1)
    zv = jnp.zeros((depth, KV_LORA, N_HEADS, HEAD_TILE - V_DIM), F32)
    wv_even = jnp.concatenate([wkv[..., QK_NOPE:], zv], axis=-1)
    wv_odd = jnp.concatenate([zv, wkv[..., QK_NOPE:]], axis=-1)
    odd = (jnp.arange(N_HEADS) % 2 == 1)[None, None, :, None]
    wv = jnp.where(odd, wv_odd, wv_even)
    P["w_kv"] = jnp.concatenate([wk.reshape(depth, KV_LORA, HW), wv.reshape(depth, KV_LORA, HW)],
                                axis=2).astype(BF16)
    def head_gains(g):
        g = jnp.pad(g, ((0, 0), (0, HEAD_TILE - QK_DIM)))
        return jnp.stack([g, g[:, perm] * rope_mask], axis=1)

    P["gq"] = head_gains(W["qk_q_g"])
    P["gk"] = head_gains(W["qk_k_g"])
    vec = lambda a: a.reshape(depth, 1, a.shape[-1])
    for name in ("norm1_g", "norm2_g", "q_norm_g", "kv_norm_g", "b_gate", "conv_dw_b", "conv_ln_g",
                 "conv_ln_b", "ffn_dw_b"):
        P[name] = vec(W[name])
    P["conv_dw"] = W["conv_dw"]
    P["ffn_dw"] = W["ffn_dw"]
    for name in ("w_gate", "w_fourier", "w_conv_out", "w_mla_out", "w_out", "ffn_up", "ffn_down"):
        P[name] = W[name].astype(BF16)
    return P


def kernel(x_prompt, x_sample, cache_ckv, cache_krope, c, c_ctx, ada_w, ada_b, norm1_g, norm2_g, w_in, w_gate, b_gate, w_fourier, conv_dw, conv_dw_b, conv_ln_g, conv_ln_b, w_conv_out, q_norm_g, w_q_up, kv_norm_g, w_kv_up, qk_q_g, qk_k_g, w_mla_out, w_out, ffn_up, ffn_dw, ffn_dw_b, ffn_down):
    W = dict(norm1_g=norm1_g, norm2_g=norm2_g, w_in=w_in, w_gate=w_gate, b_gate=b_gate,
             w_fourier=w_fourier, conv_dw=conv_dw, conv_dw_b=conv_dw_b, conv_ln_g=conv_ln_g,
             conv_ln_b=conv_ln_b, w_conv_out=w_conv_out, q_norm_g=q_norm_g, w_q_up=w_q_up,
             kv_norm_g=kv_norm_g, w_kv_up=w_kv_up, qk_q_g=qk_q_g, qk_k_g=qk_k_g,
             w_mla_out=w_mla_out, w_out=w_out, ffn_up=ffn_up, ffn_dw=ffn_dw, ffn_dw_b=ffn_dw_b,
             ffn_down=ffn_down)
    n_bp, seq_p, d = x_prompt.shape
    n_bs, seq_s, _ = x_sample.shape
    depth = ada_w.shape[0]
    t_p = n_bp * seq_p
    t_s = n_bs * seq_s
    assert t_p % seq_s == 0 and seq_s % TM == 0 and TM % seq_p == 0 and seq_s % FFN_SUB == 0
    assert seq_p & (seq_p - 1) == 0 and seq_s & (seq_s - 1) == 0 and n_bs + 1 <= 8
    n_p = t_p // TM
    per_seq = seq_s // TM

    P = _prepare_params(W)
    c_cs, s_cs = _dft_tables(F_GROUP_W, 1.0 / math.sqrt(F_GROUP_W))
    P["dft_cs"] = jnp.asarray(np.concatenate([c_cs, s_cs], axis=1), dtype=BF16)
    dft_p = [jnp.asarray(t, dtype=BF16) for t in _dft_tables(seq_p, 1.0 / math.sqrt(seq_p))]
    dft_s = [jnp.asarray(t, dtype=BF16) for t in _dft_tables(seq_s, 1.0 / math.sqrt(seq_s))]
    rope_tab = jnp.asarray(_rope_table(seq_s, TM))

    cvec = jnp.concatenate([c_ctx[None, :], c, jnp.zeros((8 - 1 - n_bs, d), F32)], axis=0)
    mod = _mod_table(cvec, ada_w, ada_b).reshape(depth, 8, 1, N_MOD * d)

    kr_tile = jnp.pad(cache_krope, ((0, 0), (0, 0), (0, 0), (QK_NOPE, HEAD_TILE - QK_DIM)))
    kc, vc = _cache_kv(cache_ckv, kr_tile, P)

    x = jnp.concatenate([x_prompt.reshape(t_p, d), x_sample.reshape(t_s, d)], axis=0)
    ckv_out, kr_out = [], []
    hbm = lambda *arrays: [_in_hbm(a) for a in arrays]
    for l in range(depth):
        fa, ug, q, k, v, ckv, kr = _in_proj(x, mod, l, P, rope_tab, n_p=n_p, per_seq=per_seq)
        fa, ug, q, k, v = hbm(fa, ug, q, k, v)
        ckv_out.append(ckv[:t_p].reshape(n_bp, seq_p, KV_LORA))
        kr_out.append(kr[:t_p, QK_NOPE:QK_DIM].reshape(n_bp, seq_p, QK_ROPE))
        yf = hbm(_fourier_prompt(fa, *dft_p, t_p=t_p, seq=seq_p),
                 _fourier_sample(fa, *dft_s, t_p=t_p, n_b=n_bs, seq=seq_s))
        yc = hbm(_conv_module(ug, l, P, first=0, n_seq=n_bp, seq=seq_p),
                 _conv_module(ug, l, P, first=t_p // seq_s, n_seq=n_bs, seq=seq_s))
        ya = hbm(_attn_prompt(q, k, v, n_seq=n_bp, seq=seq_p),
                 _attn_sample(q, k, v, kc, vc, l, t_p=t_p, n_b=n_bs, seq=seq_s, tq=ATTN_TQ))
        x = _merge(x, mod, l, P, yf, yc, ya, n_p=n_p, per_seq=per_seq)
        x = _ffn(x, mod, l, P, n_p=t_p // FFN_SUB, per_seq=seq_s // FFN_SUB, seq_p=seq_p, seq_s=seq_s)
    y_prompt = x[:t_p].reshape(n_bp, seq_p, d)
    y_sample = x[t_p:].reshape(n_bs, seq_s, d)
    return (y_prompt, y_sample, jnp.stack(ckv_out, axis=1), jnp.stack(kr_out, axis=1))
```

```python
import functools
import math

import numpy as np
import jax
import jax.numpy as jnp
from jax import lax
from jax.experimental import pallas as pl
from jax.experimental.pallas import tpu as pltpu

F32 = jnp.float32
BF16 = jnp.bfloat16

EPS = 1e-6
N_HEADS = 8
QK_NOPE = 64
QK_ROPE = 32
V_DIM = 64
QK_DIM = QK_NOPE + QK_ROPE
F_GROUPS = 4
F_GROUP_W = 128
F_W = F_GROUPS * F_GROUP_W
CONV_W = 512
CONV_K = 31
Q_LORA = 384
KV_LORA = 256
GRID_W = 64
ROPE_BASE = 10000.0
FFN_K = 3
N_MOD = 6

LANES = 128
SUBLANES = 8
V7X_VMEM_BYTES = 64 * 2**20
VMEM_RESERVE_BYTES = 6 * 2**20

HEAD_TILE = LANES
HW = N_HEADS * HEAD_TILE
PAIR_W = 2 * HEAD_TILE
CONV_PAD = 16

TM = 512
ATTN_TQ = 512
CONV_MAX_BLOCK_GROUPS = 16
FFN_SUB = 512
FFN_WIN = FFN_SUB + 2 * SUBLANES
FFN_CK = 256


def _in_hbm(a):
    if isinstance(a, jax.core.Tracer):
        return pltpu.with_memory_space_constraint(a, pltpu.HBM)
    return a


def _vmem_limit(block_bytes, scratch_bytes, temp_bytes):
    need = 2 * block_bytes + scratch_bytes + temp_bytes
    return int(min(need, V7X_VMEM_BYTES - VMEM_RESERVE_BYTES))


def _nbytes(shape, dtype):
    return int(np.prod(shape)) * jnp.dtype(dtype).itemsize


def _dot(a, b):
    return jnp.dot(a, b, preferred_element_type=F32)


def _dot_nt(a, b):
    return lax.dot_general(a, b, (((1,), (1,)), ((), ())), preferred_element_type=F32)


def _rms(x):
    return x * lax.rsqrt(jnp.mean(x * x, axis=-1, keepdims=True) + EPS)


def _sigmoid(x):
    return 1.0 / (1.0 + jnp.exp(-x))


def _silu(x):
    return x * _sigmoid(x)


def _mod_kernel(c_ref, w_ref, b_ref, o_ref):
    s = _silu(c_ref[...]).astype(BF16)
    o_ref[...] = _dot(s, w_ref[...].astype(BF16)) + b_ref[...]


def _mod_table(cvec, ada_w, ada_b):
    depth, d, n = ada_w.shape
    rows = cvec.shape[0]
    tn = n // 4
    blocks = _nbytes((d, tn), F32) + _nbytes((rows, tn), F32) * 2
    return pl.pallas_call(
        _mod_kernel,
        grid=(depth, n // tn),
        in_specs=[
            pl.BlockSpec((rows, d), lambda l, j: (0, 0)),
            pl.BlockSpec((None, d, tn), lambda l, j: (l, 0, j)),
            pl.BlockSpec((None, 1, tn), lambda l, j: (l, 0, j)),
        ],
        out_specs=pl.BlockSpec((None, rows, tn), lambda l, j: (l, 0, j)),
        out_shape=jax.ShapeDtypeStruct((depth, rows, n), F32),
        compiler_params=pltpu.CompilerParams(
            dimension_semantics=("arbitrary", "arbitrary"),
            vmem_limit_bytes=_vmem_limit(blocks, 0, _nbytes((d, tn), BF16) + 2**20)),
        name="adaln_table",
    )(cvec, ada_w, ada_b.reshape(depth, 1, n))


def _head_inv_rms(t):
    ss = jnp.sum(t * t, axis=-1, keepdims=True)
    return lax.rsqrt(ss * (1.0 / QK_DIM) + EPS)


def _keys_values(cb, kr, wkv_ref, k_ref, vt_ref, finish_key):
    lane = lax.broadcasted_iota(jnp.int32, (1, PAIR_W), 1)
    ones_row = jnp.logical_or(lane == V_DIM, lane == HEAD_TILE).astype(F32)
    for pair in range(N_HEADS // 2):
        cols = slice(pair * PAIR_W, (pair + 1) * PAIR_W)
        t2 = _dot(cb, wkv_ref[:, cols])
        for hh in range(2):
            sl = slice((2 * pair + hh) * HEAD_TILE, (2 * pair + hh + 1) * HEAD_TILE)
            k_ref[:, sl] = finish_key(t2[:, hh * HEAD_TILE:(hh + 1) * HEAD_TILE] + kr).astype(BF16)
        v2 = _dot(cb, wkv_ref[:, HW + pair * PAIR_W:HW + (pair + 1) * PAIR_W]) + ones_row
        vt_ref[cols, :] = v2.T.astype(BF16)


def _in_kernel(*refs, d, n_p, paired):
    if paired:
        (xp_ref, xs_ref, mod_ref, g1_ref, win_ref, wkr_ref, cs_ref, qg_ref, wq_ref, kvg_ref, wkv_ref,
         gq_ref, gk_ref, rope_ref,
         fa_ref, ug_ref, q_ref, k_ref, vt_ref, ckv_ref, kr_ref, x_out_ref) = refs
        x = jnp.where(pl.program_id(0) < n_p, xp_ref[...], xs_ref[...])
        x_out_ref[...] = x
    else:
        (x_ref, mod_ref, g1_ref, win_ref, wkr_ref, cs_ref, qg_ref, wq_ref, kvg_ref, wkv_ref,
         gq_ref, gk_ref, rope_ref,
         fa_ref, ug_ref, q_ref, k_ref, vt_ref, ckv_ref, kr_ref) = refs
        x = x_ref[...]
    sh1 = mod_ref[:, 0:d]
    sc1 = mod_ref[:, d:2 * d]
    h = _rms(x) * g1_ref[...] * (1.0 + sc1) + sh1
    hb = h.astype(BF16)
    c_conv = F_W
    c_q = c_conv + 2 * CONV_W
    c0 = c_q

    cos = rope_ref[:, 0:HEAD_TILE]
    sin = rope_ref[:, HEAD_TILE:2 * HEAD_TILE]

    qn = (_rms(_dot(hb, win_ref[:, c0:c0 + Q_LORA])) * qg_ref[...]).astype(BF16)
    c0 += Q_LORA
    scale = 1.0 / math.sqrt(QK_DIM)
    q_cos = cos * (gq_ref[0:1, :] * scale)
    q_sin = sin * (gq_ref[1:2, :] * scale)
    partners = _dot(qn, wq_ref[:, HW:HW + N_HEADS * QK_ROPE])
    per_tile = HEAD_TILE // QK_ROPE
    for pair in range(N_HEADS // 2):
        t2 = _dot(qn, wq_ref[:, pair * PAIR_W:(pair + 1) * PAIR_W])
        for hd in (2 * pair, 2 * pair + 1):
            sl = slice(hd * HEAD_TILE, (hd + 1) * HEAD_TILE)
            t = t2[:, (hd % 2) * HEAD_TILE:(hd % 2 + 1) * HEAD_TILE]
            tp = partners[:, (hd // per_tile) * HEAD_TILE:(hd // per_tile + 1) * HEAD_TILE]
            shift = (QK_NOPE - (hd % per_tile) * QK_ROPE) % HEAD_TILE
            if shift:
                tp = pltpu.roll(tp, shift, axis=1)
            q_ref[:, sl] = ((t * q_cos + tp * q_sin) * _head_inv_rms(t)).astype(BF16)

    ckv = _rms(_dot(hb, win_ref[:, c0:c0 + KV_LORA])) * kvg_ref[...]
    ckv_ref[...] = ckv
    kr = _dot(hb, wkr_ref[:, 0:HEAD_TILE])
    kr_ref[...] = kr
    k_cos = cos * gk_ref[0:1, :]
    kr_rot = _dot(hb, wkr_ref[:, HEAD_TILE:2 * HEAD_TILE]) * (sin * gk_ref[1:2, :])
    _keys_values(ckv.astype(BF16), kr, wkv_ref, k_ref, vt_ref,
                 lambda t: (t * k_cos + kr_rot) * _head_inv_rms(t))

    fb = _dot(hb, win_ref[:, 0:F_W]).astype(BF16)
    for g in range(F_GROUPS):
        a = _dot(fb[:, g * F_GROUP_W:(g + 1) * F_GROUP_W], cs_ref[...])
        fa_ref[:, g * F_GROUP_W:(g + 1) * F_GROUP_W] = a[:, :F_GROUP_W].astype(BF16)
        fa_ref[:, F_W + g * F_GROUP_W:F_W + (g + 1) * F_GROUP_W] = a[:, F_GROUP_W:].astype(BF16)

    ga = _dot(hb, win_ref[:, c_conv:c_conv + CONV_W])
    gg = _dot(hb, win_ref[:, c_conv + CONV_W:c_conv + 2 * CONV_W])
    ug_ref[...] = ga * _sigmoid(gg)


def _in_proj(x, mod, l, P, rope_tab, *, n_p, per_seq):
    paired = isinstance(x, tuple)
    d = (x[0] if paired else x).shape[1]
    t_rows = x[0].shape[0] + x[1].shape[0] if paired else x.shape[0]
    n_tiles = t_rows // TM
    n_s = n_tiles - n_p
    win_cols = P["w_in"].shape[-1]

    def mod_idx(i):
        return (l, jnp.where(i < n_p, 0, 1 + (i - n_p) // per_seq), 0, 0)

    def rope_idx(i):
        return (jnp.where(i < n_p, per_seq, (i - n_p) % per_seq), 0)

    row = lambda w: pl.BlockSpec((TM, w), lambda i: (i, 0))
    lay = lambda *s: pl.BlockSpec((None,) + s, lambda i: (l,) + (0,) * len(s))
    wq_cols = P["w_q"].shape[-1]
    out_shapes = [
        jax.ShapeDtypeStruct((t_rows, 2 * F_W), BF16),
        jax.ShapeDtypeStruct((t_rows, CONV_W), F32),
        jax.ShapeDtypeStruct((t_rows, HW), BF16),
        jax.ShapeDtypeStruct((t_rows, HW), BF16),
        jax.ShapeDtypeStruct((HW, t_rows), BF16),
        jax.ShapeDtypeStruct((t_rows, KV_LORA), F32),
        jax.ShapeDtypeStruct((t_rows, HEAD_TILE), F32),
    ]
    if paired:
        out_shapes.append(jax.ShapeDtypeStruct((t_rows, d), F32))
        x_args = list(x)
        x_specs = [pl.BlockSpec((TM, d), lambda i: (jnp.minimum(i, n_p - 1), 0)),
                   pl.BlockSpec((TM, d), lambda i: (jnp.clip(i - n_p, 0, n_s - 1), 0))]
    else:
        x_args = [x]
        x_specs = [row(d)]
    out_specs = [pl.BlockSpec((HW, TM), lambda i: (0, i)) if s.shape[0] == HW else row(s.shape[1])
                 for s in out_shapes]
    blocks = (len(x_args) * _nbytes((TM, d), F32) + _nbytes((d, win_cols + 2 * HEAD_TILE), BF16)
              + _nbytes((Q_LORA, wq_cols), BF16)
              + _nbytes((KV_LORA, 2 * HW), BF16) + _nbytes((TM, 2 * HEAD_TILE), F32)
              + sum(_nbytes(s.shape, s.dtype) // (t_rows // TM) for s in out_shapes))
    return pl.pallas_call(
        functools.partial(_in_kernel, d=d, n_p=n_p, paired=paired),
        grid=(n_tiles,),
        in_specs=x_specs + [
            pl.BlockSpec((None, None, 1, N_MOD * d), mod_idx),
            lay(1, d),
            lay(d, win_cols),
            lay(d, 2 * HEAD_TILE),
            pl.BlockSpec((F_GROUP_W, 2 * F_GROUP_W), lambda i: (0, 0)),
            lay(1, Q_LORA),
            lay(Q_LORA, wq_cols),
            lay(1, KV_LORA),
            lay(KV_LORA, 2 * HW),
            lay(2, HEAD_TILE),
            lay(2, HEAD_TILE),
            pl.BlockSpec((TM, 2 * HEAD_TILE), rope_idx),
        ],
        out_specs=out_specs,
        out_shape=out_shapes,
        compiler_params=pltpu.CompilerParams(
            dimension_semantics=("arbitrary",),
            vmem_limit_bytes=_vmem_limit(blocks, 0, 12 * _nbytes((TM, d), F32))),
        name="in_proj",
    )(*x_args, mod, P["norm1_g"], P["w_in"], P["w_kr"], P["dft_cs"], P["q_norm_g"], P["w_q"],
      P["kv_norm_g"], P["w_kv"], P["gq"], P["gk"], rope_tab)


def _cache_kv_kernel(ckv_ref, kr_ref, wkv_ref, gk_ref, k_ref, vt_ref):
    k_gain = gk_ref[0:1, :]
    _keys_values(ckv_ref[...].astype(BF16), kr_ref[...], wkv_ref, k_ref, vt_ref,
                 lambda t: t * k_gain * _head_inv_rms(t))


def _cache_kv(cache_ckv, cache_kr_tile, P):
    nb, depth, past, _ = cache_ckv.shape
    blk = lambda w: pl.BlockSpec((None, None, past, w), lambda b, l: (b, l, 0, 0))
    lay = lambda *s: pl.BlockSpec((None,) + s, lambda b, l: (l,) + (0,) * len(s))
    blocks = (_nbytes((past, KV_LORA + HEAD_TILE), F32) + _nbytes((KV_LORA, 2 * HW), BF16)
              + 2 * _nbytes((past, HW), BF16))
    return pl.pallas_call(
        _cache_kv_kernel,
        grid=(nb, depth),
        in_specs=[blk(KV_LORA), blk(HEAD_TILE), lay(KV_LORA, 2 * HW), lay(2, HEAD_TILE)],
        out_specs=[blk(HW), pl.BlockSpec((None, None, HW, past), lambda b, l: (b, l, 0, 0))],
        out_shape=[jax.ShapeDtypeStruct((nb, depth, past, HW), BF16),
                   jax.ShapeDtypeStruct((nb, depth, HW, past), BF16)],
        compiler_params=pltpu.CompilerParams(
            dimension_semantics=("arbitrary", "arbitrary"),
            vmem_limit_bytes=_vmem_limit(blocks, 0, 8 * _nbytes((past, HW), F32))),
        name="cache_kv",
    )(cache_ckv, cache_kr_tile, P["w_kv"], P["gk"])


def _fourier_kernel(fa_ref, c_ref, s_ref, o_ref, *, seq, n_seq):
    for s in range(n_seq):
        rows = slice(s * seq, (s + 1) * seq)
        y = _dot(c_ref[...], fa_ref[rows, 0:F_W]) - _dot(s_ref[...], fa_ref[rows, F_W:2 * F_W])
        o_ref[rows, :] = y.astype(BF16)


def _fourier_prompt(fa, dft_c, dft_s, *, t_p, seq):
    n_seq = TM // seq
    blocks = _nbytes((TM, 2 * F_W), BF16) + 2 * _nbytes((seq, seq), BF16) + _nbytes((TM, F_W), BF16)
    return pl.pallas_call(
        functools.partial(_fourier_kernel, seq=seq, n_seq=n_seq),
        grid=(t_p // TM,),
        in_specs=[
            pl.BlockSpec((TM, 2 * F_W), lambda i: (i, 0)),
            pl.BlockSpec((seq, seq), lambda i: (0, 0)),
            pl.BlockSpec((seq, seq), lambda i: (0, 0)),
        ],
        out_specs=pl.BlockSpec((TM, F_W), lambda i: (i, 0)),
        out_shape=jax.ShapeDtypeStruct((t_p, F_W), BF16),
        compiler_params=pltpu.CompilerParams(
            dimension_semantics=("arbitrary",),
            vmem_limit_bytes=_vmem_limit(blocks, 0, 4 * _nbytes((TM, F_W), F32))),
        name="fourier_prompt",
    )(fa, dft_c, dft_s)


def _fourier_sample_kernel(fa_ref, c_ref, s_ref, o_ref):
    y = _dot(c_ref[...], fa_ref[:, 0:F_W]) - _dot(s_ref[...], fa_ref[:, F_W:2 * F_W])
    o_ref[...] = y.astype(BF16)


def _fourier_sample(fa, dft_c, dft_s, *, t_p, n_b, seq):
    tr = TM
    per = seq // tr
    first = t_p // seq
    blocks = _nbytes((seq, 2 * F_W), BF16) + 2 * _nbytes((tr, seq), BF16) + _nbytes((tr, F_W), BF16)
    return pl.pallas_call(
        _fourier_sample_kernel,
        grid=(n_b, per),
        in_specs=[
            pl.BlockSpec((seq, 2 * F_W), lambda b, j: (first + b, 0)),
            pl.BlockSpec((tr, seq), lambda b, j: (j, 0)),
            pl.BlockSpec((tr, seq), lambda b, j: (j, 0)),
        ],
        out_specs=pl.BlockSpec((tr, F_W), lambda b, j: (b * per + j, 0)),
        out_shape=jax.ShapeDtypeStruct((n_b * seq, F_W), BF16),
        compiler_params=pltpu.CompilerParams(
            dimension_semantics=("arbitrary", "arbitrary"),
            vmem_limit_bytes=_vmem_limit(blocks, 0, 4 * _nbytes((tr, F_W), F32))),
        name="fourier_sample",
    )(fa, dft_c, dft_s)


def _conv_block_groups(n_grp):
    return max(g for g in range(1, CONV_MAX_BLOCK_GROUPS + 1) if n_grp % (2 * g) == 0)


def _conv_kernel(u_ref, dw_ref, dwb_ref, lng_ref, lnb_ref, o_ref, nat_ref, ext_ref, *, seq):
    half = CONV_K // 2
    win = seq + 2 * CONV_PAD
    n_grp = win // SUBLANES
    n_slab = CONV_W // LANES
    lead = half * SUBLANES
    zeros = jnp.zeros((CONV_PAD, LANES), F32)
    for k in range(n_slab):
        nat_ref[k, 0:CONV_PAD, :] = zeros
        nat_ref[k, CONV_PAD + seq:win, :] = zeros
        nat_ref[k, CONV_PAD:CONV_PAD + seq, :] = u_ref[:, k * LANES:(k + 1) * LANES]
    for k in range(n_slab):
        for v in range(n_grp):
            g = nat_ref[k, pl.ds(v, SUBLANES, stride=n_grp), :]
            ext_ref[k, lead + v * SUBLANES:lead + (v + 1) * SUBLANES, :] = g
            if v >= n_grp - half:
                j = v - (n_grp - half)
                ext_ref[k, j * SUBLANES:(j + 1) * SUBLANES, :] = pltpu.roll(g, 1, axis=0)
            if v < half:
                j = half + n_grp + v
                ext_ref[k, j * SUBLANES:(j + 1) * SUBLANES, :] = pltpu.roll(g, SUBLANES - 1, axis=0)

    blk_grp = _conv_block_groups(n_grp)
    blk = blk_grp * SUBLANES
    n_blk = n_grp // blk_grp

    for k in range(n_slab):
        lanes = slice(k * LANES, (k + 1) * LANES)

        def conv_body(b, carry, k=k, lanes=lanes):
            r0 = pl.multiple_of(b * blk, SUBLANES)
            acc = jnp.zeros((blk, LANES), F32)
            for t in range(CONV_K):
                acc = acc + dw_ref[t:t + 1, lanes] * ext_ref[k, pl.ds(r0 + t * SUBLANES, blk), :]
            ext_ref[k, pl.ds(r0, blk), :] = acc + dwb_ref[:, lanes]
            return carry

        lax.fori_loop(0, n_blk, conv_body, 0)

    def norm_block(r0):
        acc = jnp.concatenate([ext_ref[k, pl.ds(r0, blk), :] for k in range(n_slab)], axis=1)
        mu = jnp.mean(acc, axis=-1, keepdims=True)
        cen = acc - mu
        var = jnp.mean(cen * cen, axis=-1, keepdims=True)
        y = _silu(cen * lax.rsqrt(var + EPS) * lng_ref[...] + lnb_ref[...])
        for k in range(n_slab):
            ext_ref[k, pl.ds(r0, blk), :] = y[:, k * LANES:(k + 1) * LANES]

    def norm_body(b, carry):
        norm_block(pl.multiple_of(b * blk, SUBLANES))
        norm_block(pl.multiple_of((b + n_blk // 2) * blk, SUBLANES))
        return carry

    lax.fori_loop(0, n_blk // 2, norm_body, 0)

    for k in range(n_slab):
        for v in range(n_grp):
            nat_ref[k, pl.ds(v, SUBLANES, stride=n_grp), :] = ext_ref[k, v * SUBLANES:(v + 1) * SUBLANES, :]
        o_ref[:, k * LANES:(k + 1) * LANES] = nat_ref[k, CONV_PAD:CONV_PAD + seq, :].astype(BF16)


def _conv_module(ug, l, P, *, first, n_seq, seq):
    lay = lambda *s: pl.BlockSpec((None,) + s, lambda i: (l,) + (0,) * len(s))
    blocks = _nbytes((seq, CONV_W), F32) + _nbytes((seq, CONV_W), BF16) + _nbytes((CONV_K + 4, CONV_W), F32)
    win = seq + 2 * CONV_PAD
    ext = win + 2 * (CONV_K // 2) * SUBLANES
    scratch = _nbytes((win + ext, CONV_W), F32)
    return pl.pallas_call(
        functools.partial(_conv_kernel, seq=seq),
        grid=(n_seq,),
        in_specs=[
            pl.BlockSpec((seq, CONV_W), lambda i: (first + i, 0)),
            lay(CONV_K, CONV_W), lay(1, CONV_W), lay(1, CONV_W), lay(1, CONV_W),
        ],
        out_specs=pl.BlockSpec((seq, CONV_W), lambda i: (i, 0)),
        out_shape=jax.ShapeDtypeStruct((n_seq * seq, CONV_W), BF16),
        scratch_shapes=[pltpu.VMEM((CONV_W // LANES, win, LANES), F32),
                        pltpu.VMEM((CONV_W // LANES, ext, LANES), F32)],
        compiler_params=pltpu.CompilerParams(
            dimension_semantics=("arbitrary",),
            vmem_limit_bytes=_vmem_limit(blocks, scratch, 4 * 2**20)),
        name=f"conv_seq{seq}",
    )(ug, P["conv_dw"], P["conv_dw_b"], P["conv_ln_g"], P["conv_ln_b"])


def _attn_kernel(*refs, with_cache):
    if with_cache:
        q_ref, k_ref, vt_ref, kc_ref, vct_ref, o_ref = refs
    else:
        q_ref, k_ref, vt_ref, o_ref = refs
    tq = q_ref.shape[0]
    row = lax.broadcasted_iota(jnp.int32, (HEAD_TILE, tq), 0)

    def scores(hd):
        sl = slice(hd * HEAD_TILE, (hd + 1) * HEAD_TILE)
        qh = q_ref[:, sl]
        return _dot_nt(k_ref[:, sl], qh), (_dot_nt(kc_ref[:, sl], qh) if with_cache else None)

    nxt = scores(0)
    even = None
    for hd in range(N_HEADS):
        sl = slice(hd * HEAD_TILE, (hd + 1) * HEAD_TILE)
        s_m, s_c = nxt
        if hd + 1 < N_HEADS:
            nxt = scores(hd + 1)
        m = jnp.max(s_m, axis=0, keepdims=True)
        if with_cache:
            m = jnp.maximum(m, jnp.max(s_c, axis=0, keepdims=True))
        out_t = _dot(vt_ref[sl, :], jnp.exp(s_m - m).astype(BF16))
        if with_cache:
            out_t = out_t + _dot(vct_ref[sl, :], jnp.exp(s_c - m).astype(BF16))
        den_row = V_DIM if hd % 2 == 0 else 0
        out_t = out_t * (1.0 / out_t[den_row:den_row + 1, :])
        if hd % 2 == 0:
            even = out_t
        else:
            pair_t = jnp.where(row < V_DIM, even, out_t)
            o_ref[:, (hd // 2) * HEAD_TILE:(hd // 2 + 1) * HEAD_TILE] = pair_t.T.astype(BF16)


def _attn_prompt(q, k, vt, *, n_seq, seq):
    blk = pl.BlockSpec((seq, HW), lambda i: (i, 0))
    blocks = 3 * _nbytes((seq, HW), BF16) + _nbytes((seq, N_HEADS * V_DIM), BF16)
    return pl.pallas_call(
        functools.partial(_attn_kernel, with_cache=False),
        grid=(n_seq,),
        in_specs=[blk, blk, pl.BlockSpec((HW, seq), lambda i: (0, i))],
        out_specs=pl.BlockSpec((seq, N_HEADS * V_DIM), lambda i: (i, 0)),
        out_shape=jax.ShapeDtypeStruct((n_seq * seq, N_HEADS * V_DIM), BF16),
        compiler_params=pltpu.CompilerParams(
            dimension_semantics=("arbitrary",),
            vmem_limit_bytes=_vmem_limit(blocks, 0, 8 * _nbytes((seq, seq), F32) + 2**21)),
        name="attn_prompt",
    )(q, k, vt)


def _attn_sample(q, k, vt, kc, vct, l, *, t_p, n_b, seq, tq):
    per = seq // tq
    first_q = t_p // tq
    first_s = t_p // seq
    past = kc.shape[2]
    once = pl.Buffered(1)
    k_blk = pl.BlockSpec((seq, HW), lambda b, j: (first_s + b, 0), pipeline_mode=once)
    vt_blk = pl.BlockSpec((HW, seq), lambda b, j: (0, first_s + b), pipeline_mode=once)
    kc_blk = pl.BlockSpec((None, None, past, HW), lambda b, j: (b, l, 0, 0), pipeline_mode=once)
    vct_blk = pl.BlockSpec((None, None, HW, past), lambda b, j: (b, l, 0, 0), pipeline_mode=once)
    blocks = _nbytes((tq, HW), BF16) + _nbytes((tq, N_HEADS * V_DIM), BF16)
    resident = 2 * _nbytes((seq + past, HW), BF16)
    return pl.pallas_call(
        functools.partial(_attn_kernel, with_cache=True),
        grid=(n_b, per),
        in_specs=[pl.BlockSpec((tq, HW), lambda b, j: (first_q + b * per + j, 0)),
                  k_blk, vt_blk, kc_blk, vct_blk],
        out_specs=pl.BlockSpec((tq, N_HEADS * V_DIM), lambda b, j: (b * per + j, 0)),
        out_shape=jax.ShapeDtypeStruct((n_b * seq, N_HEADS * V_DIM), BF16),
        compiler_params=pltpu.CompilerParams(
            dimension_semantics=("arbitrary", "arbitrary"),
            vmem_limit_bytes=_vmem_limit(blocks, resident, 8 * _nbytes((tq, seq + past), F32))),
        name="attn_sample",
    )(q, k, vt, kc, vct)


def _merge_kernel(x_ref, mod_ref, g1_ref, fp_ref, fs_ref, cp_ref, cs_ref, ap_ref, as_ref,
                  wg_ref, bg_ref, wf_ref, wc_ref, wa_ref, wo_ref, o_ref, *, d, n_p):
    is_prompt = pl.program_id(0) < n_p
    x = x_ref[...]
    sh1 = mod_ref[:, 0:d]
    sc1 = mod_ref[:, d:2 * d]
    g1 = mod_ref[:, 2 * d:3 * d]
    hb = (_rms(x) * g1_ref[...] * (1.0 + sc1) + sh1).astype(BF16)
    merged = None
    branches = ((fp_ref, fs_ref, wf_ref), (cp_ref, cs_ref, wc_ref), (ap_ref, as_ref, wa_ref))
    for b, (yp_ref, ys_ref, w_ref) in enumerate(branches):
        y_in = jnp.where(is_prompt, yp_ref[...], ys_ref[...])
        y = _dot(y_in, w_ref[...])
        gate = _sigmoid(_dot(hb, wg_ref[:, b * d:(b + 1) * d]) + bg_ref[:, b * d:(b + 1) * d])
        merged = gate * y if merged is None else merged + gate * y
    o_ref[...] = x + g1 * _dot(merged.astype(BF16), wo_ref[...])


def _merge(x, mod, l, P, yf, yc, ya, *, n_p, per_seq):
    t_rows, d = x.shape
    n_tiles = t_rows // TM
    n_s = n_tiles - n_p

    def mod_idx(i):
        return (l, jnp.where(i < n_p, 0, 1 + (i - n_p) // per_seq), 0, 0)

    row = pl.BlockSpec((TM, d), lambda i: (i, 0))
    lay = lambda *s: pl.BlockSpec((None,) + s, lambda i: (l,) + (0,) * len(s))
    p_blk = pl.BlockSpec((TM, F_W), lambda i: (jnp.minimum(i, n_p - 1), 0))
    s_blk = pl.BlockSpec((TM, F_W), lambda i: (jnp.clip(i - n_p, 0, n_s - 1), 0))
    blocks = (2 * _nbytes((TM, d), F32) + 6 * _nbytes((TM, F_W), BF16) + _nbytes((d, 3 * d), BF16)
              + 3 * _nbytes((F_W, d), BF16) + _nbytes((d, d), BF16))
    return pl.pallas_call(
        functools.partial(_merge_kernel, d=d, n_p=n_p),
        grid=(n_tiles,),
        in_specs=[row, pl.BlockSpec((None, None, 1, N_MOD * d), mod_idx), lay(1, d),
                  p_blk, s_blk, p_blk, s_blk, p_blk, s_blk,
                  lay(d, 3 * d), lay(1, 3 * d), lay(F_W, d), lay(CONV_W, d), lay(N_HEADS * V_DIM, d),
                  lay(d, d)],
        out_specs=row,
        out_shape=jax.ShapeDtypeStruct((t_rows, d), F32),
        compiler_params=pltpu.CompilerParams(
            dimension_semantics=("arbitrary",),
            vmem_limit_bytes=_vmem_limit(blocks, 0, 8 * _nbytes((TM, d), F32))),
        name="merge",
    )(x, mod, P["norm1_g"], yf[0], yf[1], yc[0], yc[1], ya[0], ya[1],
      P["w_gate"], P["b_gate"], P["w_fourier"], P["w_conv_out"], P["w_mla_out"], P["w_out"])


def _ffn_kernel(x_ref, xp_ref, xn_ref, mod_ref, g2_ref, up_ref, dw_ref, b_ref, dn_ref,
                o_ref, nat_ref, h_ref, act_ref, *, d, d_ff, n_p, seq_p, seq_s, first):
    i = pl.program_id(0) + first
    sh2 = mod_ref[:, 3 * d:4 * d]
    sc2 = mod_ref[:, 4 * d:5 * d]
    g2 = mod_ref[:, 5 * d:6 * d]

    def norm_mod(x):
        return _rms(x) * g2_ref[...] * (1.0 + sc2) + sh2

    is_prompt = i < n_p
    period_mask = jnp.where(is_prompt, seq_p - 1, seq_s - 1)
    has_prev = ((i * FFN_SUB) & period_mask) != 0
    has_next = (((i + 1) * FFN_SUB) & period_mask) != 0
    h_prev = jnp.where(has_prev, norm_mod(xp_ref[...]), 0.0)
    h_next = jnp.where(has_next, norm_mod(xn_ref[...]), 0.0)
    h_main = norm_mod(x_ref[...])

    n_grp = FFN_WIN // SUBLANES
    for k in range(d // LANES):
        lanes = slice(k * LANES, (k + 1) * LANES)
        nat_ref[k, 0:SUBLANES, :] = h_prev[:, lanes]
        nat_ref[k, SUBLANES:SUBLANES + FFN_SUB, :] = h_main[:, lanes]
        nat_ref[k, SUBLANES + FFN_SUB:FFN_WIN, :] = h_next[:, lanes]
    for k in range(d // LANES):
        groups = [nat_ref[k, pl.ds(v, SUBLANES, stride=n_grp), :] for v in range(n_grp)]
        h_ref[:, k * LANES:(k + 1) * LANES] = jnp.concatenate(groups, axis=0).astype(BF16)

    sub = lax.broadcasted_iota(jnp.int32, (SUBLANES, 1), 0)
    no_prev = jnp.zeros((SUBLANES, 1), jnp.bool_)
    no_next = jnp.zeros((SUBLANES, 1), jnp.bool_)
    for edge in range(seq_p, FFN_SUB, seq_p):
        assert (edge + SUBLANES) % n_grp == 0
        no_prev = jnp.logical_or(no_prev, sub == (edge + SUBLANES) // n_grp)
        no_next = jnp.logical_or(no_next, sub == (edge + SUBLANES - 1) // n_grp)
    no_prev = jnp.logical_and(is_prompt, no_prev)
    no_next = jnp.logical_and(is_prompt, no_next)

    def conv(u, c0):
        last = u[FFN_WIN - SUBLANES:FFN_WIN, :]
        first = u[0:SUBLANES, :]
        wrap_prev = jnp.where(no_prev, 0.0, pltpu.roll(last, 1, axis=0))
        wrap_next = jnp.where(no_next, 0.0, pltpu.roll(first, SUBLANES - 1, axis=0))
        prev = jnp.concatenate([wrap_prev, u[0:FFN_WIN - SUBLANES, :]], axis=0)
        nxt = jnp.concatenate([u[SUBLANES:FFN_WIN, :], wrap_next], axis=0)
        w = dw_ref[:, c0:c0 + FFN_CK]
        return w[0:1, :] * prev + w[1:2, :] * u + w[2:3, :] * nxt + b_ref[:, c0:c0 + FFN_CK]

    hx = h_ref[...]

    def up_proj(c):
        return (_dot(hx, up_ref[:, c * FFN_CK:(c + 1) * FFN_CK]),
                _dot(hx, up_ref[:, d_ff + c * FFN_CK:d_ff + (c + 1) * FFN_CK]))

    n_ck = d_ff // FFN_CK
    half_ck = (n_ck + 1) // 2
    part = None
    nxt = up_proj(0)
    for c in range(n_ck):
        u_a, u_b = nxt
        if c + 1 < n_ck:
            nxt = up_proj(c + 1)
        if c == half_ck:
            part = _dot(act_ref[:, 0:half_ck * FFN_CK], dn_ref[0:half_ck * FFN_CK, :])
        act = _silu(conv(u_a, c * FFN_CK)) * conv(u_b, d_ff + c * FFN_CK)
        act_ref[:, c * FFN_CK:(c + 1) * FFN_CK] = act.astype(BF16)
    y = part + _dot(act_ref[:, half_ck * FFN_CK:d_ff], dn_ref[half_ck * FFN_CK:d_ff, :])
    for k in range(d // LANES):
        lanes = slice(k * LANES, (k + 1) * LANES)
        for v in range(n_grp):
            nat_ref[k, pl.ds(v, SUBLANES, stride=n_grp), :] = y[v * SUBLANES:(v + 1) * SUBLANES, lanes]
        o_ref[:, lanes] = x_ref[:, lanes] + g2[:, lanes] * nat_ref[k, SUBLANES:SUBLANES + FFN_SUB, :]


def _ffn(x, mod, l, P, *, n_p, per_seq, seq_p, seq_s, first=0, count=None):
    t_rows, d = x.shape
    d_ff = P["ffn_down"].shape[1]
    n_tiles = t_rows // FFN_SUB if count is None else count
    halo_per_tile = FFN_SUB // SUBLANES
    n_halo = t_rows // SUBLANES

    def mod_idx(j):
        i = j + first
        return (l, jnp.where(i < n_p, 0, 1 + (i - n_p) // per_seq), 0, 0)

    row = pl.BlockSpec((FFN_SUB, d), lambda j: (j + first, 0))
    once = pl.Buffered(1)
    lay = lambda *s: pl.BlockSpec((None,) + s, lambda i: (l,) + (0,) * len(s))
    res = lambda *s: pl.BlockSpec((None,) + s, lambda i: (l,) + (0,) * len(s), pipeline_mode=once)
    weights = _nbytes((d, 2 * d_ff), BF16) + _nbytes((d_ff, d), BF16)
    blocks = 2 * _nbytes((FFN_SUB + 2 * SUBLANES, d), F32) + 4 * _nbytes((8, 2 * d_ff), F32)
    scratch = (_nbytes((FFN_WIN, d), F32) + _nbytes((FFN_WIN, d), BF16) + _nbytes((FFN_WIN, d_ff), BF16))
    temps = 4 * _nbytes((FFN_WIN, d), F32) + 8 * _nbytes((FFN_WIN, FFN_CK), F32)
    return pl.pallas_call(
        functools.partial(_ffn_kernel, d=d, d_ff=d_ff, n_p=n_p, seq_p=seq_p, seq_s=seq_s, first=first),
        grid=(n_tiles,),
        in_specs=[
            row,
            pl.BlockSpec((SUBLANES, d), lambda j: (jnp.maximum((j + first) * halo_per_tile - 1, 0), 0)),
            pl.BlockSpec((SUBLANES, d),
                         lambda j: (jnp.minimum((j + first + 1) * halo_per_tile, n_halo - 1), 0)),
            pl.BlockSpec((None, None, 1, N_MOD * d), mod_idx),
            lay(1, d),
            res(d, 2 * d_ff),
            lay(FFN_K, 2 * d_ff),
            lay(1, 2 * d_ff),
            res(d_ff, d),
        ],
        out_specs=pl.BlockSpec((FFN_SUB, d), lambda j: (j, 0)),
        out_shape=jax.ShapeDtypeStruct((n_tiles * FFN_SUB, d), F32),
        scratch_shapes=[pltpu.VMEM((d // LANES, FFN_WIN, LANES), F32), pltpu.VMEM((FFN_WIN, d), BF16),
                        pltpu.VMEM((FFN_WIN, d_ff), BF16)],
        compiler_params=pltpu.CompilerParams(
            dimension_semantics=("arbitrary",),
            vmem_limit_bytes=_vmem_limit(blocks, scratch + weights, temps)),
        name="conv_ffn",
    )(x, x, x, mod, P["norm2_g"], P["ffn_up"], P["ffn_dw"], P["ffn_dw_b"], P["ffn_down"])


def _dft_tables(n, scale):
    j = np.arange(n, dtype=np.int64)
    ang = ((j[:, None] * j[None, :]) % n).astype(np.float64) * (2.0 * math.pi / n)
    return (np.cos(ang) * scale).astype(np.float32), (np.sin(ang) * scale).astype(np.float32)


def _rope_partner():
    lane = np.arange(HEAD_TILE)
    dim = lane - QK_NOPE
    in_rope = (dim >= 0) & (dim < QK_ROPE)
    quarter = QK_ROPE // 4
    first_half = (dim % (2 * quarter)) // quarter == 0
    perm = np.where(in_rope, np.where(first_half, lane + quarter, lane - quarter), lane)
    return perm, in_rope.astype(np.float32)


def _rope_table(seq, ident_rows):
    rows = seq // GRID_W
    row = np.repeat(np.arange(rows), GRID_W).astype(np.float32)
    col = np.tile(np.arange(GRID_W), rows).astype(np.float32)
    half = QK_ROPE // 2
    inv = (np.float32(ROPE_BASE) ** (-np.arange(0, half, 2, dtype=np.float32) / half)).astype(np.float32)
    ang = np.stack([row[:, None] * inv, col[:, None] * inv], axis=1)
    cos, sin = np.cos(ang), np.sin(ang)
    cos_r = np.stack([cos, cos], axis=2).reshape(seq, QK_ROPE)
    sin_r = np.stack([-sin, sin], axis=2).reshape(seq, QK_ROPE)

    def tile(mid, fill):
        left = np.full((seq, QK_NOPE), fill, np.float32)
        right = np.full((seq, HEAD_TILE - QK_DIM), fill, np.float32)
        return np.concatenate([left, mid, right], axis=1)

    tab = np.concatenate([tile(cos_r, 1.0), tile(sin_r, 0.0)], axis=1)
    ident = np.concatenate([np.ones((ident_rows, HEAD_TILE), np.float32),
                            np.zeros((ident_rows, HEAD_TILE), np.float32)], axis=1)
    return np.concatenate([tab, ident], axis=0).astype(np.float32)


def _prepare_params(W):
    depth, d, _ = W["w_in"].shape
    P = {}
    perm, rope_mask = _rope_partner()
    split = F_W + 2 * CONV_W + Q_LORA + KV_LORA
    w_in = W["w_in"]
    kr_tile = jnp.concatenate([jnp.zeros((depth, d, QK_NOPE), F32), w_in[:, :, split:],
                               jnp.zeros((depth, d, HEAD_TILE - QK_DIM), F32)], axis=2)
    P["w_in"] = w_in.astype(BF16)
    P["w_kr"] = jnp.concatenate([kr_tile, kr_tile[:, :, perm] * rope_mask], axis=2).astype(BF16)
    wq = W["w_q_up"].reshape(depth, Q_LORA, N_HEADS, QK_DIM)
    wq = jnp.pad(wq, ((0, 0), (0, 0), (0, 0), (0, HEAD_TILE - QK_DIM)))
    wq_rot = wq[..., perm][..., QK_NOPE:QK_DIM]
    P["w_q"] = jnp.concatenate([wq.reshape(depth, Q_LORA, HW),
                                wq_rot.reshape(depth, Q_LORA, N_HEADS * QK_ROPE)], axis=2).astype(BF16)
    wkv = W["w_kv_up"].reshape(depth, KV_LORA, N_HEADS, QK_NOPE + V_DIM)
    zk = jnp.zeros((depth, KV_LORA, N_HEADS, HEAD_TILE - QK_NOPE), F32)
    wk = jnp.concatenate([wkv[..., :QK_NOPE], zk], axis=-1)
    zv = jnp.zeros((depth, KV_LORA, N_HEADS, HEAD_TILE - V_DIM), F32)
    wv_even = jnp.concatenate([wkv[..., QK_NOPE:], zv], axis=-1)
    wv_odd = jnp.concatenate([zv, wkv[..., QK_NOPE:]], axis=-1)
    odd = (jnp.arange(N_HEADS) % 2 == 1)[None, None, :, None]
    wv = jnp.where(odd, wv_odd, wv_even)
    P["w_kv"] = jnp.concatenate([wk.reshape(depth, KV_LORA, HW), wv.reshape(depth, KV_LORA, HW)],
                                axis=2).astype(BF16)
    def head_gains(g):
        g = jnp.pad(g, ((0, 0), (0, HEAD_TILE - QK_DIM)))
        return jnp.stack([g, g[:, perm] * rope_mask], axis=1)

    P["gq"] = head_gains(W["qk_q_g"])
    P["gk"] = head_gains(W["qk_k_g"])
    vec = lambda a: a.reshape(depth, 1, a.shape[-1])
    for name in ("norm1_g", "norm2_g", "q_norm_g", "kv_norm_g", "b_gate", "conv_dw_b", "conv_ln_g",
                 "conv_ln_b", "ffn_dw_b"):
        P[name] = vec(W[name])
    P["conv_dw"] = W["conv_dw"]
    P["ffn_dw"] = W["ffn_dw"]
    for name in ("w_gate", "w_fourier", "w_conv_out", "w_mla_out", "w_out", "ffn_up", "ffn_down"):
        P[name] = W[name].astype(BF16)
    return P


def kernel(x_prompt, x_sample, cache_ckv, cache_krope, c, c_ctx, ada_w, ada_b, norm1_g, norm2_g, w_in, w_gate, b_gate, w_fourier, conv_dw, conv_dw_b, conv_ln_g, conv_ln_b, w_conv_out, q_norm_g, w_q_up, kv_norm_g, w_kv_up, qk_q_g, qk_k_g, w_mla_out, w_out, ffn_up, ffn_dw, ffn_dw_b, ffn_down):
    W = dict(norm1_g=norm1_g, norm2_g=norm2_g, w_in=w_in, w_gate=w_gate, b_gate=b_gate,
             w_fourier=w_fourier, conv_dw=conv_dw, conv_dw_b=conv_dw_b, conv_ln_g=conv_ln_g,
             conv_ln_b=conv_ln_b, w_conv_out=w_conv_out, q_norm_g=q_norm_g, w_q_up=w_q_up,
             kv_norm_g=kv_norm_g, w_kv_up=w_kv_up, qk_q_g=qk_q_g, qk_k_g=qk_k_g,
             w_mla_out=w_mla_out, w_out=w_out, ffn_up=ffn_up, ffn_dw=ffn_dw, ffn_dw_b=ffn_dw_b,
             ffn_down=ffn_down)
    n_bp, seq_p, d = x_prompt.shape
    n_bs, seq_s, _ = x_sample.shape
    depth = ada_w.shape[0]
    t_p = n_bp * seq_p
    t_s = n_bs * seq_s
    assert t_p % seq_s == 0 and seq_s % TM == 0 and TM % seq_p == 0 and seq_s % FFN_SUB == 0
    assert seq_p & (seq_p - 1) == 0 and seq_s & (seq_s - 1) == 0 and n_bs + 1 <= 8
    n_p = t_p // TM
    per_seq = seq_s // TM

    P = _prepare_params(W)
    c_cs, s_cs = _dft_tables(F_GROUP_W, 1.0 / math.sqrt(F_GROUP_W))
    P["dft_cs"] = jnp.asarray(np.concatenate([c_cs, s_cs], axis=1), dtype=BF16)
    dft_p = [jnp.asarray(t, dtype=BF16) for t in _dft_tables(seq_p, 1.0 / math.sqrt(seq_p))]
    dft_s = [jnp.asarray(t, dtype=BF16) for t in _dft_tables(seq_s, 1.0 / math.sqrt(seq_s))]
    rope_tab = jnp.asarray(_rope_table(seq_s, TM))

    cvec = jnp.concatenate([c_ctx[None, :], c, jnp.zeros((8 - 1 - n_bs, d), F32)], axis=0)
    mod = _mod_table(cvec, ada_w, ada_b).reshape(depth, 8, 1, N_MOD * d)

    kr_tile = jnp.pad(cache_krope, ((0, 0), (0, 0), (0, 0), (QK_NOPE, HEAD_TILE - QK_DIM)))
    kc, vc = _cache_kv(cache_ckv, kr_tile, P)

    x = (x_prompt.reshape(t_p, d), x_sample.reshape(t_s, d))
    ffn_args = dict(n_p=t_p // FFN_SUB, per_seq=seq_s // FFN_SUB, seq_p=seq_p, seq_s=seq_s)
    ckv_out, kr_out = [], []
    hbm = lambda *arrays: [_in_hbm(a) for a in arrays]
    for l in range(depth):
        outs = _in_proj(x, mod, l, P, rope_tab, n_p=n_p, per_seq=per_seq)
        if l == 0:
            x = outs[-1]
        fa, ug, q, k, v, ckv, kr = outs[:7]
        fa, ug, q, k, v = hbm(fa, ug, q, k, v)
        ckv_out.append(ckv[:t_p].reshape(n_bp, seq_p, KV_LORA))
        kr_out.append(kr[:t_p, QK_NOPE:QK_DIM].reshape(n_bp, seq_p, QK_ROPE))
        yf = hbm(_fourier_prompt(fa, *dft_p, t_p=t_p, seq=seq_p),
                 _fourier_sample(fa, *dft_s, t_p=t_p, n_b=n_bs, seq=seq_s))
        yc = hbm(_conv_module(ug, l, P, first=0, n_seq=n_bp, seq=seq_p),
                 _conv_module(ug, l, P, first=t_p // seq_s, n_seq=n_bs, seq=seq_s))
        ya = hbm(_attn_prompt(q, k, v, n_seq=n_bp, seq=seq_p),
                 _attn_sample(q, k, v, kc, vc, l, t_p=t_p, n_b=n_bs, seq=seq_s, tq=ATTN_TQ))
        x = _merge(x, mod, l, P, yf, yc, ya, n_p=n_p, per_seq=per_seq)
        if l + 1 < depth:
            x = _ffn(x, mod, l, P, **ffn_args)
    y_prompt = _ffn(x, mod, depth - 1, P, first=0, count=t_p // FFN_SUB, **ffn_args)
    y_sample = _ffn(x, mod, depth - 1, P, first=t_p // FFN_SUB, count=t_s // FFN_SUB, **ffn_args)
    return (y_prompt.reshape(n_bp, seq_p, d), y_sample.reshape(n_bs, seq_s, d),
            jnp.stack(ckv_out, axis=1), jnp.stack(kr_out, axis=1))
```

```python
import functools
import math

import numpy as np
import jax
import jax.numpy as jnp
from jax import lax
from jax.experimental import pallas as pl
from jax.experimental.pallas import tpu as pltpu

F32 = jnp.float32
BF16 = jnp.bfloat16

EPS = 1e-6
N_HEADS = 8
QK_NOPE = 64
QK_ROPE = 32
V_DIM = 64
QK_DIM = QK_NOPE + QK_ROPE
F_GROUPS = 4
F_GROUP_W = 128
F_W = F_GROUPS * F_GROUP_W
CONV_W = 512
CONV_K = 31
Q_LORA = 384
KV_LORA = 256
GRID_W = 64
ROPE_BASE = 10000.0
FFN_K = 3
N_MOD = 6

LANES = 128
SUBLANES = 8
V7X_VMEM_BYTES = 64 * 2**20
VMEM_RESERVE_BYTES = 6 * 2**20

HEAD_TILE = LANES
HW = N_HEADS * HEAD_TILE
PAIR_W = 2 * HEAD_TILE
CONV_PAD = 16

TM = 512
ATTN_TQ = 512
CONV_MAX_BLOCK_GROUPS = 16
FFN_SUB = 512
FFN_WIN = FFN_SUB + 2 * SUBLANES
FFN_CK = 256


def _in_hbm(a):
    if isinstance(a, jax.core.Tracer):
        return pltpu.with_memory_space_constraint(a, pltpu.HBM)
    return a


def _vmem_limit(block_bytes, scratch_bytes, temp_bytes):
    need = 2 * block_bytes + scratch_bytes + temp_bytes
    return int(min(need, V7X_VMEM_BYTES - VMEM_RESERVE_BYTES))


def _nbytes(shape, dtype):
    return int(np.prod(shape)) * jnp.dtype(dtype).itemsize


def _dot(a, b):
    return jnp.dot(a, b, preferred_element_type=F32)


def _dot_nt(a, b):
    return lax.dot_general(a, b, (((1,), (1,)), ((), ())), preferred_element_type=F32)


def _rms(x):
    return x * lax.rsqrt(jnp.mean(x * x, axis=-1, keepdims=True) + EPS)


def _sigmoid(x):
    return 1.0 / (1.0 + jnp.exp(-x))


def _silu(x):
    return x * _sigmoid(x)


def _mod_kernel(c_ref, w_ref, b_ref, o_ref):
    s = _silu(c_ref[...]).astype(BF16)
    o_ref[...] = _dot(s, w_ref[...].astype(BF16)) + b_ref[...]


def _mod_table(cvec, ada_w, ada_b):
    depth, d, n = ada_w.shape
    rows = cvec.shape[0]
    tn = n // 4
    blocks = _nbytes((d, tn), F32) + _nbytes((rows, tn), F32) * 2
    return pl.pallas_call(
        _mod_kernel,
        grid=(depth, n // tn),
        in_specs=[
            pl.BlockSpec((rows, d), lambda l, j: (0, 0)),
            pl.BlockSpec((None, d, tn), lambda l, j: (l, 0, j)),
            pl.BlockSpec((None, 1, tn), lambda l, j: (l, 0, j)),
        ],
        out_specs=pl.BlockSpec((None, rows, tn), lambda l, j: (l, 0, j)),
        out_shape=jax.ShapeDtypeStruct((depth, rows, n), F32),
        compiler_params=pltpu.CompilerParams(
            dimension_semantics=("arbitrary", "arbitrary"),
            vmem_limit_bytes=_vmem_limit(blocks, 0, _nbytes((d, tn), BF16) + 2**20)),
        name="adaln_table",
    )(cvec, ada_w, ada_b.reshape(depth, 1, n))


def _head_inv_rms(t):
    ss = jnp.sum(t * t, axis=-1, keepdims=True)
    return lax.rsqrt(ss * (1.0 / QK_DIM) + EPS)


def _keys_values(cb, kr, wkv_ref, k_ref, vt_ref, finish_key):
    lane = lax.broadcasted_iota(jnp.int32, (1, PAIR_W), 1)
    ones_row = jnp.logical_or(lane == V_DIM, lane == HEAD_TILE).astype(F32)
    for pair in range(N_HEADS // 2):
        cols = slice(pair * PAIR_W, (pair + 1) * PAIR_W)
        t2 = _dot(cb, wkv_ref[:, cols])
        for hh in range(2):
            sl = slice((2 * pair + hh) * HEAD_TILE, (2 * pair + hh + 1) * HEAD_TILE)
            k_ref[:, sl] = finish_key(t2[:, hh * HEAD_TILE:(hh + 1) * HEAD_TILE] + kr).astype(BF16)
        v2 = _dot(cb, wkv_ref[:, HW + pair * PAIR_W:HW + (pair + 1) * PAIR_W]) + ones_row
        vt_ref[cols, :] = v2.T.astype(BF16)


def _in_kernel(*refs, d, n_p, paired):
    if paired:
        (xp_ref, xs_ref, mod_ref, g1_ref, win_ref, wkr_ref, cs_ref, qg_ref, wq_ref, kvg_ref, wkv_ref,
         gq_ref, gk_ref, rope_ref,
         fa_ref, ug_ref, q_ref, k_ref, vt_ref, ckv_ref, kr_ref, x_out_ref) = refs
        x = jnp.where(pl.program_id(0) < n_p, xp_ref[...], xs_ref[...])
        x_out_ref[...] = x
    else:
        (x_ref, mod_ref, g1_ref, win_ref, wkr_ref, cs_ref, qg_ref, wq_ref, kvg_ref, wkv_ref,
         gq_ref, gk_ref, rope_ref,
         fa_ref, ug_ref, q_ref, k_ref, vt_ref, ckv_ref, kr_ref) = refs
        x = x_ref[...]
    sh1 = mod_ref[:, 0:d]
    sc1 = mod_ref[:, d:2 * d]
    h = _rms(x) * g1_ref[...] * (1.0 + sc1) + sh1
    hb = h.astype(BF16)
    c_conv = F_W
    c_q = c_conv + 2 * CONV_W
    c0 = c_q

    cos = rope_ref[:, 0:HEAD_TILE]
    sin = rope_ref[:, HEAD_TILE:2 * HEAD_TILE]

    qn = (_rms(_dot(hb, win_ref[:, c0:c0 + Q_LORA])) * qg_ref[...]).astype(BF16)
    c0 += Q_LORA
    scale = math.log2(math.e) / math.sqrt(QK_DIM)
    q_cos = cos * (gq_ref[0:1, :] * scale)
    q_sin = sin * (gq_ref[1:2, :] * scale)
    partners = _dot(qn, wq_ref[:, HW:HW + N_HEADS * QK_ROPE])
    per_tile = HEAD_TILE // QK_ROPE
    for pair in range(N_HEADS // 2):
        t2 = _dot(qn, wq_ref[:, pair * PAIR_W:(pair + 1) * PAIR_W])
        for hd in (2 * pair, 2 * pair + 1):
            sl = slice(hd * HEAD_TILE, (hd + 1) * HEAD_TILE)
            t = t2[:, (hd % 2) * HEAD_TILE:(hd % 2 + 1) * HEAD_TILE]
            tp = partners[:, (hd // per_tile) * HEAD_TILE:(hd // per_tile + 1) * HEAD_TILE]
            shift = (QK_NOPE - (hd % per_tile) * QK_ROPE) % HEAD_TILE
            if shift:
                tp = pltpu.roll(tp, shift, axis=1)
            q_ref[:, sl] = ((t * q_cos + tp * q_sin) * _head_inv_rms(t)).astype(BF16)

    ckv = _rms(_dot(hb, win_ref[:, c0:c0 + KV_LORA])) * kvg_ref[...]
    ckv_ref[...] = ckv
    kr = _dot(hb, wkr_ref[:, 0:HEAD_TILE])
    kr_ref[...] = kr
    k_cos = cos * gk_ref[0:1, :]
    kr_rot = _dot(hb, wkr_ref[:, HEAD_TILE:2 * HEAD_TILE]) * (sin * gk_ref[1:2, :])
    _keys_values(ckv.astype(BF16), kr, wkv_ref, k_ref, vt_ref,
                 lambda t: (t * k_cos + kr_rot) * _head_inv_rms(t))

    fb = _dot(hb, win_ref[:, 0:F_W]).astype(BF16)
    for g in range(F_GROUPS):
        a = _dot(fb[:, g * F_GROUP_W:(g + 1) * F_GROUP_W], cs_ref[...])
        fa_ref[:, g * F_GROUP_W:(g + 1) * F_GROUP_W] = a[:, :F_GROUP_W].astype(BF16)
        fa_ref[:, F_W + g * F_GROUP_W:F_W + (g + 1) * F_GROUP_W] = a[:, F_GROUP_W:].astype(BF16)

    ga = _dot(hb, win_ref[:, c_conv:c_conv + CONV_W])
    gg = _dot(hb, win_ref[:, c_conv + CONV_W:c_conv + 2 * CONV_W])
    ug_ref[...] = ga * _sigmoid(gg)


def _in_proj(x, mod, l, P, rope_tab, *, n_p, per_seq):
    paired = isinstance(x, tuple)
    d = (x[0] if paired else x).shape[1]
    t_rows = x[0].shape[0] + x[1].shape[0] if paired else x.shape[0]
    n_tiles = t_rows // TM
    n_s = n_tiles - n_p
    win_cols = P["w_in"].shape[-1]

    def mod_idx(i):
        return (l, jnp.where(i < n_p, 0, 1 + (i - n_p) // per_seq), 0, 0)

    def rope_idx(i):
        return (jnp.where(i < n_p, per_seq, (i - n_p) % per_seq), 0)

    row = lambda w: pl.BlockSpec((TM, w), lambda i: (i, 0))
    lay = lambda *s: pl.BlockSpec((None,) + s, lambda i: (l,) + (0,) * len(s))
    wq_cols = P["w_q"].shape[-1]
    out_shapes = [
        jax.ShapeDtypeStruct((t_rows, 2 * F_W), BF16),
        jax.ShapeDtypeStruct((t_rows, CONV_W), F32),
        jax.ShapeDtypeStruct((t_rows, HW), BF16),
        jax.ShapeDtypeStruct((t_rows, HW), BF16),
        jax.ShapeDtypeStruct((HW, t_rows), BF16),
        jax.ShapeDtypeStruct((t_rows, KV_LORA), F32),
        jax.ShapeDtypeStruct((t_rows, HEAD_TILE), F32),
    ]
    if paired:
        out_shapes.append(jax.ShapeDtypeStruct((t_rows, d), F32))
        x_args = list(x)
        x_specs = [pl.BlockSpec((TM, d), lambda i: (jnp.minimum(i, n_p - 1), 0)),
                   pl.BlockSpec((TM, d), lambda i: (jnp.clip(i - n_p, 0, n_s - 1), 0))]
    else:
        x_args = [x]
        x_specs = [row(d)]
    out_specs = [pl.BlockSpec((HW, TM), lambda i: (0, i)) if s.shape[0] == HW else row(s.shape[1])
                 for s in out_shapes]
    blocks = (len(x_args) * _nbytes((TM, d), F32) + _nbytes((d, win_cols + 2 * HEAD_TILE), BF16)
              + _nbytes((Q_LORA, wq_cols), BF16)
              + _nbytes((KV_LORA, 2 * HW), BF16) + _nbytes((TM, 2 * HEAD_TILE), F32)
              + sum(_nbytes(s.shape, s.dtype) // (t_rows // TM) for s in out_shapes))
    return pl.pallas_call(
        functools.partial(_in_kernel, d=d, n_p=n_p, paired=paired),
        grid=(n_tiles,),
        in_specs=x_specs + [
            pl.BlockSpec((None, None, 1, N_MOD * d), mod_idx),
            lay(1, d),
            lay(d, win_cols),
            lay(d, 2 * HEAD_TILE),
            pl.BlockSpec((F_GROUP_W, 2 * F_GROUP_W), lambda i: (0, 0)),
            lay(1, Q_LORA),
            lay(Q_LORA, wq_cols),
            lay(1, KV_LORA),
            lay(KV_LORA, 2 * HW),
            lay(2, HEAD_TILE),
            lay(2, HEAD_TILE),
            pl.BlockSpec((TM, 2 * HEAD_TILE), rope_idx),
        ],
        out_specs=out_specs,
        out_shape=out_shapes,
        compiler_params=pltpu.CompilerParams(
            dimension_semantics=("arbitrary",),
            vmem_limit_bytes=_vmem_limit(blocks, 0, 12 * _nbytes((TM, d), F32))),
        name="in_proj",
    )(*x_args, mod, P["norm1_g"], P["w_in"], P["w_kr"], P["dft_cs"], P["q_norm_g"], P["w_q"],
      P["kv_norm_g"], P["w_kv"], P["gq"], P["gk"], rope_tab)


def _cache_kv_kernel(ckv_ref, kr_ref, wkv_ref, gk_ref, k_ref, vt_ref):
    k_gain = gk_ref[0:1, :]
    _keys_values(ckv_ref[...].astype(BF16), kr_ref[...], wkv_ref, k_ref, vt_ref,
                 lambda t: t * k_gain * _head_inv_rms(t))


def _cache_kv(cache_ckv, cache_kr_tile, P):
    nb, depth, past, _ = cache_ckv.shape
    blk = lambda w: pl.BlockSpec((None, None, past, w), lambda b, l: (b, l, 0, 0))
    lay = lambda *s: pl.BlockSpec((None,) + s, lambda b, l: (l,) + (0,) * len(s))
    blocks = (_nbytes((past, KV_LORA + HEAD_TILE), F32) + _nbytes((KV_LORA, 2 * HW), BF16)
              + 2 * _nbytes((past, HW), BF16))
    return pl.pallas_call(
        _cache_kv_kernel,
        grid=(nb, depth),
        in_specs=[blk(KV_LORA), blk(HEAD_TILE), lay(KV_LORA, 2 * HW), lay(2, HEAD_TILE)],
        out_specs=[blk(HW), pl.BlockSpec((None, None, HW, past), lambda b, l: (b, l, 0, 0))],
        out_shape=[jax.ShapeDtypeStruct((nb, depth, past, HW), BF16),
                   jax.ShapeDtypeStruct((nb, depth, HW, past), BF16)],
        compiler_params=pltpu.CompilerParams(
            dimension_semantics=("arbitrary", "arbitrary"),
            vmem_limit_bytes=_vmem_limit(blocks, 0, 8 * _nbytes((past, HW), F32))),
        name="cache_kv",
    )(cache_ckv, cache_kr_tile, P["w_kv"], P["gk"])


def _fourier_kernel(fa_ref, c_ref, s_ref, o_ref, *, seq, n_seq):
    for s in range(n_seq):
        rows = slice(s * seq, (s + 1) * seq)
        y = _dot(c_ref[...], fa_ref[rows, 0:F_W]) - _dot(s_ref[...], fa_ref[rows, F_W:2 * F_W])
        o_ref[rows, :] = y.astype(BF16)


def _fourier_prompt(fa, dft_c, dft_s, *, t_p, seq):
    n_seq = TM // seq
    blocks = _nbytes((TM, 2 * F_W), BF16) + 2 * _nbytes((seq, seq), BF16) + _nbytes((TM, F_W), BF16)
    return pl.pallas_call(
        functools.partial(_fourier_kernel, seq=seq, n_seq=n_seq),
        grid=(t_p // TM,),
        in_specs=[
            pl.BlockSpec((TM, 2 * F_W), lambda i: (i, 0)),
            pl.BlockSpec((seq, seq), lambda i: (0, 0)),
            pl.BlockSpec((seq, seq), lambda i: (0, 0)),
        ],
        out_specs=pl.BlockSpec((TM, F_W), lambda i: (i, 0)),
        out_shape=jax.ShapeDtypeStruct((t_p, F_W), BF16),
        compiler_params=pltpu.CompilerParams(
            dimension_semantics=("arbitrary",),
            vmem_limit_bytes=_vmem_limit(blocks, 0, 4 * _nbytes((TM, F_W), F32))),
        name="fourier_prompt",
    )(fa, dft_c, dft_s)


def _fourier_sample_kernel(fa_ref, c_ref, s_ref, o_ref):
    y = _dot(c_ref[...], fa_ref[:, 0:F_W]) - _dot(s_ref[...], fa_ref[:, F_W:2 * F_W])
    o_ref[...] = y.astype(BF16)


def _fourier_sample(fa, dft_c, dft_s, *, t_p, n_b, seq):
    tr = TM
    per = seq // tr
    first = t_p // seq
    blocks = _nbytes((seq, 2 * F_W), BF16) + 2 * _nbytes((tr, seq), BF16) + _nbytes((tr, F_W), BF16)
    return pl.pallas_call(
        _fourier_sample_kernel,
        grid=(n_b, per),
        in_specs=[
            pl.BlockSpec((seq, 2 * F_W), lambda b, j: (first + b, 0)),
            pl.BlockSpec((tr, seq), lambda b, j: (j, 0)),
            pl.BlockSpec((tr, seq), lambda b, j: (j, 0)),
        ],
        out_specs=pl.BlockSpec((tr, F_W), lambda b, j: (b * per + j, 0)),
        out_shape=jax.ShapeDtypeStruct((n_b * seq, F_W), BF16),
        compiler_params=pltpu.CompilerParams(
            dimension_semantics=("arbitrary", "arbitrary"),
            vmem_limit_bytes=_vmem_limit(blocks, 0, 4 * _nbytes((tr, F_W), F32))),
        name="fourier_sample",
    )(fa, dft_c, dft_s)


def _conv_block_groups(n_grp):
    return max(g for g in range(1, CONV_MAX_BLOCK_GROUPS + 1) if n_grp % (2 * g) == 0)


def _conv_kernel(u_ref, dw_ref, dwb_ref, lng_ref, lnb_ref, o_ref, nat_ref, ext_ref, *, seq):
    half = CONV_K // 2
    win = seq + 2 * CONV_PAD
    n_grp = win // SUBLANES
    n_slab = CONV_W // LANES
    lead = half * SUBLANES
    zeros = jnp.zeros((CONV_PAD, LANES), F32)
    for k in range(n_slab):
        nat_ref[k, 0:CONV_PAD, :] = zeros
        nat_ref[k, CONV_PAD + seq:win, :] = zeros
        nat_ref[k, CONV_PAD:CONV_PAD + seq, :] = u_ref[:, k * LANES:(k + 1) * LANES]
    for k in range(n_slab):
        for v in range(n_grp):
            g = nat_ref[k, pl.ds(v, SUBLANES, stride=n_grp), :]
            ext_ref[k, lead + v * SUBLANES:lead + (v + 1) * SUBLANES, :] = g
            if v >= n_grp - half:
                j = v - (n_grp - half)
                ext_ref[k, j * SUBLANES:(j + 1) * SUBLANES, :] = pltpu.roll(g, 1, axis=0)
            if v < half:
                j = half + n_grp + v
                ext_ref[k, j * SUBLANES:(j + 1) * SUBLANES, :] = pltpu.roll(g, SUBLANES - 1, axis=0)

    blk_grp = _conv_block_groups(n_grp)
    blk = blk_grp * SUBLANES
    n_blk = n_grp // blk_grp

    for k in range(n_slab):
        lanes = slice(k * LANES, (k + 1) * LANES)

        def conv_body(b, carry, k=k, lanes=lanes):
            r0 = pl.multiple_of(b * blk, SUBLANES)
            acc = jnp.zeros((blk, LANES), F32)
            for t in range(CONV_K):
                acc = acc + dw_ref[t:t + 1, lanes] * ext_ref[k, pl.ds(r0 + t * SUBLANES, blk), :]
            ext_ref[k, pl.ds(r0, blk), :] = acc + dwb_ref[:, lanes]
            return carry

        lax.fori_loop(0, n_blk, conv_body, 0)

    def norm_block(r0):
        acc = jnp.concatenate([ext_ref[k, pl.ds(r0, blk), :] for k in range(n_slab)], axis=1)
        mu = jnp.mean(acc, axis=-1, keepdims=True)
        cen = acc - mu
        var = jnp.mean(cen * cen, axis=-1, keepdims=True)
        y = _silu(cen * lax.rsqrt(var + EPS) * lng_ref[...] + lnb_ref[...])
        for k in range(n_slab):
            ext_ref[k, pl.ds(r0, blk), :] = y[:, k * LANES:(k + 1) * LANES]

    def norm_body(b, carry):
        norm_block(pl.multiple_of(b * blk, SUBLANES))
        norm_block(pl.multiple_of((b + n_blk // 2) * blk, SUBLANES))
        return carry

    lax.fori_loop(0, n_blk // 2, norm_body, 0)

    for k in range(n_slab):
        for v in range(n_grp):
            nat_ref[k, pl.ds(v, SUBLANES, stride=n_grp), :] = ext_ref[k, v * SUBLANES:(v + 1) * SUBLANES, :]
        o_ref[:, k * LANES:(k + 1) * LANES] = nat_ref[k, CONV_PAD:CONV_PAD + seq, :].astype(BF16)


def _conv_module(ug, l, P, *, first, n_seq, seq):
    lay = lambda *s: pl.BlockSpec((None,) + s, lambda i: (l,) + (0,) * len(s))
    blocks = _nbytes((seq, CONV_W), F32) + _nbytes((seq, CONV_W), BF16) + _nbytes((CONV_K + 4, CONV_W), F32)
    win = seq + 2 * CONV_PAD
    ext = win + 2 * (CONV_K // 2) * SUBLANES
    scratch = _nbytes((win + ext, CONV_W), F32)
    return pl.pallas_call(
        functools.partial(_conv_kernel, seq=seq),
        grid=(n_seq,),
        in_specs=[
            pl.BlockSpec((seq, CONV_W), lambda i: (first + i, 0)),
            lay(CONV_K, CONV_W), lay(1, CONV_W), lay(1, CONV_W), lay(1, CONV_W),
        ],
        out_specs=pl.BlockSpec((seq, CONV_W), lambda i: (i, 0)),
        out_shape=jax.ShapeDtypeStruct((n_seq * seq, CONV_W), BF16),
        scratch_shapes=[pltpu.VMEM((CONV_W // LANES, win, LANES), F32),
                        pltpu.VMEM((CONV_W // LANES, ext, LANES), F32)],
        compiler_params=pltpu.CompilerParams(
            dimension_semantics=("arbitrary",),
            vmem_limit_bytes=_vmem_limit(blocks, scratch, 4 * 2**20)),
        name=f"conv_seq{seq}",
    )(ug, P["conv_dw"], P["conv_dw_b"], P["conv_ln_g"], P["conv_ln_b"])


def _attn_kernel(*refs, with_cache):
    if with_cache:
        q_ref, k_ref, vt_ref, kc_ref, vct_ref, o_ref = refs
    else:
        q_ref, k_ref, vt_ref, o_ref = refs
    tq = q_ref.shape[0]
    row = lax.broadcasted_iota(jnp.int32, (HEAD_TILE, tq), 0)

    def scores(hd):
        sl = slice(hd * HEAD_TILE, (hd + 1) * HEAD_TILE)
        qh = q_ref[:, sl]
        return _dot_nt(k_ref[:, sl], qh), (_dot_nt(kc_ref[:, sl], qh) if with_cache else None)

    nxt = scores(0)
    even = None
    for hd in range(N_HEADS):
        sl = slice(hd * HEAD_TILE, (hd + 1) * HEAD_TILE)
        s_m, s_c = nxt
        if hd + 1 < N_HEADS:
            nxt = scores(hd + 1)
        m = jnp.max(s_m, axis=0, keepdims=True)
        if with_cache:
            m = jnp.maximum(m, jnp.max(s_c, axis=0, keepdims=True))
        out_t = _dot(vt_ref[sl, :], jnp.exp2(s_m - m).astype(BF16))
        if with_cache:
            out_t = out_t + _dot(vct_ref[sl, :], jnp.exp2(s_c - m).astype(BF16))
        den_row = V_DIM if hd % 2 == 0 else 0
        out_t = out_t * (1.0 / out_t[den_row:den_row + 1, :])
        if hd % 2 == 0:
            even = out_t
        else:
            pair_t = jnp.where(row < V_DIM, even, out_t)
            o_ref[:, (hd // 2) * HEAD_TILE:(hd // 2 + 1) * HEAD_TILE] = pair_t.T.astype(BF16)


def _attn_prompt(q, k, vt, *, n_seq, seq):
    blk = pl.BlockSpec((seq, HW), lambda i: (i, 0))
    blocks = 3 * _nbytes((seq, HW), BF16) + _nbytes((seq, N_HEADS * V_DIM), BF16)
    return pl.pallas_call(
        functools.partial(_attn_kernel, with_cache=False),
        grid=(n_seq,),
        in_specs=[blk, blk, pl.BlockSpec((HW, seq), lambda i: (0, i))],
        out_specs=pl.BlockSpec((seq, N_HEADS * V_DIM), lambda i: (i, 0)),
        out_shape=jax.ShapeDtypeStruct((n_seq * seq, N_HEADS * V_DIM), BF16),
        compiler_params=pltpu.CompilerParams(
            dimension_semantics=("arbitrary",),
            vmem_limit_bytes=_vmem_limit(blocks, 0, 8 * _nbytes((seq, seq), F32) + 2**21)),
        name="attn_prompt",
    )(q, k, vt)


def _attn_sample(q, k, vt, kc, vct, l, *, t_p, n_b, seq, tq):
    per = seq // tq
    first_q = t_p // tq
    first_s = t_p // seq
    past = kc.shape[2]
    once = pl.Buffered(1)
    k_blk = pl.BlockSpec((seq, HW), lambda b, j: (first_s + b, 0), pipeline_mode=once)
    vt_blk = pl.BlockSpec((HW, seq), lambda b, j: (0, first_s + b), pipeline_mode=once)
    kc_blk = pl.BlockSpec((None, None, past, HW), lambda b, j: (b, l, 0, 0), pipeline_mode=once)
    vct_blk = pl.BlockSpec((None, None, HW, past), lambda b, j: (b, l, 0, 0), pipeline_mode=once)
    blocks = _nbytes((tq, HW), BF16) + _nbytes((tq, N_HEADS * V_DIM), BF16)
    resident = 2 * _nbytes((seq + past, HW), BF16)
    return pl.pallas_call(
        functools.partial(_attn_kernel, with_cache=True),
        grid=(n_b, per),
        in_specs=[pl.BlockSpec((tq, HW), lambda b, j: (first_q + b * per + j, 0)),
                  k_blk, vt_blk, kc_blk, vct_blk],
        out_specs=pl.BlockSpec((tq, N_HEADS * V_DIM), lambda b, j: (b * per + j, 0)),
        out_shape=jax.ShapeDtypeStruct((n_b * seq, N_HEADS * V_DIM), BF16),
        compiler_params=pltpu.CompilerParams(
            dimension_semantics=("arbitrary", "arbitrary"),
            vmem_limit_bytes=_vmem_limit(blocks, resident, 8 * _nbytes((tq, seq + past), F32))),
        name="attn_sample",
    )(q, k, vt, kc, vct)


def _merge_kernel(x_ref, mod_ref, g1_ref, fp_ref, fs_ref, cp_ref, cs_ref, ap_ref, as_ref,
                  wg_ref, bg_ref, wf_ref, wc_ref, wa_ref, wo_ref, o_ref, *, d, n_p):
    is_prompt = pl.program_id(0) < n_p
    x = x_ref[...]
    sh1 = mod_ref[:, 0:d]
    sc1 = mod_ref[:, d:2 * d]
    g1 = mod_ref[:, 2 * d:3 * d]
    hb = (_rms(x) * g1_ref[...] * (1.0 + sc1) + sh1).astype(BF16)
    merged = None
    branches = ((fp_ref, fs_ref, wf_ref), (cp_ref, cs_ref, wc_ref), (ap_ref, as_ref, wa_ref))
    for b, (yp_ref, ys_ref, w_ref) in enumerate(branches):
        y_in = jnp.where(is_prompt, yp_ref[...], ys_ref[...])
        y = _dot(y_in, w_ref[...])
        gate = _sigmoid(_dot(hb, wg_ref[:, b * d:(b + 1) * d]) + bg_ref[:, b * d:(b + 1) * d])
        merged = gate * y if merged is None else merged + gate * y
    o_ref[...] = x + g1 * _dot(merged.astype(BF16), wo_ref[...])


def _merge(x, mod, l, P, yf, yc, ya, *, n_p, per_seq):
    t_rows, d = x.shape
    n_tiles = t_rows // TM
    n_s = n_tiles - n_p

    def mod_idx(i):
        return (l, jnp.where(i < n_p, 0, 1 + (i - n_p) // per_seq), 0, 0)

    row = pl.BlockSpec((TM, d), lambda i: (i, 0))
    lay = lambda *s: pl.BlockSpec((None,) + s, lambda i: (l,) + (0,) * len(s))
    p_blk = pl.BlockSpec((TM, F_W), lambda i: (jnp.minimum(i, n_p - 1), 0))
    s_blk = pl.BlockSpec((TM, F_W), lambda i: (jnp.clip(i - n_p, 0, n_s - 1), 0))
    blocks = (2 * _nbytes((TM, d), F32) + 6 * _nbytes((TM, F_W), BF16) + _nbytes((d, 3 * d), BF16)
              + 3 * _nbytes((F_W, d), BF16) + _nbytes((d, d), BF16))
    return pl.pallas_call(
        functools.partial(_merge_kernel, d=d, n_p=n_p),
        grid=(n_tiles,),
        in_specs=[row, pl.BlockSpec((None, None, 1, N_MOD * d), mod_idx), lay(1, d),
                  p_blk, s_blk, p_blk, s_blk, p_blk, s_blk,
                  lay(d, 3 * d), lay(1, 3 * d), lay(F_W, d), lay(CONV_W, d), lay(N_HEADS * V_DIM, d),
                  lay(d, d)],
        out_specs=row,
        out_shape=jax.ShapeDtypeStruct((t_rows, d), F32),
        compiler_params=pltpu.CompilerParams(
            dimension_semantics=("arbitrary",),
            vmem_limit_bytes=_vmem_limit(blocks, 0, 8 * _nbytes((TM, d), F32))),
        name="merge",
    )(x, mod, P["norm1_g"], yf[0], yf[1], yc[0], yc[1], ya[0], ya[1],
      P["w_gate"], P["b_gate"], P["w_fourier"], P["w_conv_out"], P["w_mla_out"], P["w_out"])


def _ffn_kernel(x_ref, xp_ref, xn_ref, mod_ref, g2_ref, up_ref, dw_ref, b_ref, dn_ref,
                o_ref, nat_ref, h_ref, act_ref, *, d, d_ff, n_p, seq_p, seq_s, first):
    i = pl.program_id(0) + first
    sh2 = mod_ref[:, 3 * d:4 * d]
    sc2 = mod_ref[:, 4 * d:5 * d]
    g2 = mod_ref[:, 5 * d:6 * d]

    def norm_mod(x):
        return _rms(x) * g2_ref[...] * (1.0 + sc2) + sh2

    is_prompt = i < n_p
    period_mask = jnp.where(is_prompt, seq_p - 1, seq_s - 1)
    has_prev = ((i * FFN_SUB) & period_mask) != 0
    has_next = (((i + 1) * FFN_SUB) & period_mask) != 0
    h_prev = jnp.where(has_prev, norm_mod(xp_ref[...]), 0.0)
    h_next = jnp.where(has_next, norm_mod(xn_ref[...]), 0.0)
    h_main = norm_mod(x_ref[...])

    n_grp = FFN_WIN // SUBLANES
    for k in range(d // LANES):
        lanes = slice(k * LANES, (k + 1) * LANES)
        nat_ref[k, 0:SUBLANES, :] = h_prev[:, lanes]
        nat_ref[k, SUBLANES:SUBLANES + FFN_SUB, :] = h_main[:, lanes]
        nat_ref[k, SUBLANES + FFN_SUB:FFN_WIN, :] = h_next[:, lanes]
    for k in range(d // LANES):
        groups = [nat_ref[k, pl.ds(v, SUBLANES, stride=n_grp), :] for v in range(n_grp)]
        h_ref[:, k * LANES:(k + 1) * LANES] = jnp.concatenate(groups, axis=0).astype(BF16)

    sub = lax.broadcasted_iota(jnp.int32, (SUBLANES, 1), 0)
    no_prev = jnp.zeros((SUBLANES, 1), jnp.bool_)
    no_next = jnp.zeros((SUBLANES, 1), jnp.bool_)
    for edge in range(seq_p, FFN_SUB, seq_p):
        assert (edge + SUBLANES) % n_grp == 0
        no_prev = jnp.logical_or(no_prev, sub == (edge + SUBLANES) // n_grp)
        no_next = jnp.logical_or(no_next, sub == (edge + SUBLANES - 1) // n_grp)
    no_prev = jnp.logical_and(is_prompt, no_prev)
    no_next = jnp.logical_and(is_prompt, no_next)

    def conv(u, c0):
        last = u[FFN_WIN - SUBLANES:FFN_WIN, :]
        first = u[0:SUBLANES, :]
        wrap_prev = jnp.where(no_prev, 0.0, pltpu.roll(last, 1, axis=0))
        wrap_next = jnp.where(no_next, 0.0, pltpu.roll(first, SUBLANES - 1, axis=0))
        prev = jnp.concatenate([wrap_prev, u[0:FFN_WIN - SUBLANES, :]], axis=0)
        nxt = jnp.concatenate([u[SUBLANES:FFN_WIN, :], wrap_next], axis=0)
        w = dw_ref[:, c0:c0 + FFN_CK]
        return w[0:1, :] * prev + w[1:2, :] * u + w[2:3, :] * nxt + b_ref[:, c0:c0 + FFN_CK]

    hx = h_ref[...]

    def up_proj(c):
        return (_dot(hx, up_ref[:, c * FFN_CK:(c + 1) * FFN_CK]),
                _dot(hx, up_ref[:, d_ff + c * FFN_CK:d_ff + (c + 1) * FFN_CK]))

    n_ck = d_ff // FFN_CK
    half_ck = (n_ck + 1) // 2
    part = None
    nxt = up_proj(0)
    for c in range(n_ck):
        u_a, u_b = nxt
        if c + 1 < n_ck:
            nxt = up_proj(c + 1)
        if c == half_ck:
            part = _dot(act_ref[:, 0:half_ck * FFN_CK], dn_ref[0:half_ck * FFN_CK, :])
        act = _silu(conv(u_a, c * FFN_CK)) * conv(u_b, d_ff + c * FFN_CK)
        act_ref[:, c * FFN_CK:(c + 1) * FFN_CK] = act.astype(BF16)
    y = part + _dot(act_ref[:, half_ck * FFN_CK:d_ff], dn_ref[half_ck * FFN_CK:d_ff, :])
    for k in range(d // LANES):
        lanes = slice(k * LANES, (k + 1) * LANES)
        for v in range(n_grp):
            nat_ref[k, pl.ds(v, SUBLANES, stride=n_grp), :] = y[v * SUBLANES:(v + 1) * SUBLANES, lanes]
        o_ref[:, lanes] = x_ref[:, lanes] + g2[:, lanes] * nat_ref[k, SUBLANES:SUBLANES + FFN_SUB, :]


def _ffn(x, mod, l, P, *, n_p, per_seq, seq_p, seq_s, first=0, count=None):
    t_rows, d = x.shape
    d_ff = P["ffn_down"].shape[1]
    n_tiles = t_rows // FFN_SUB if count is None else count
    halo_per_tile = FFN_SUB // SUBLANES
    n_halo = t_rows // SUBLANES

    def mod_idx(j):
        i = j + first
        return (l, jnp.where(i < n_p, 0, 1 + (i - n_p) // per_seq), 0, 0)

    row = pl.BlockSpec((FFN_SUB, d), lambda j: (j + first, 0))
    once = pl.Buffered(1)
    lay = lambda *s: pl.BlockSpec((None,) + s, lambda i: (l,) + (0,) * len(s))
    res = lambda *s: pl.BlockSpec((None,) + s, lambda i: (l,) + (0,) * len(s), pipeline_mode=once)
    weights = _nbytes((d, 2 * d_ff), BF16) + _nbytes((d_ff, d), BF16)
    blocks = 2 * _nbytes((FFN_SUB + 2 * SUBLANES, d), F32) + 4 * _nbytes((8, 2 * d_ff), F32)
    scratch = (_nbytes((FFN_WIN, d), F32) + _nbytes((FFN_WIN, d), BF16) + _nbytes((FFN_WIN, d_ff), BF16))
    temps = 4 * _nbytes((FFN_WIN, d), F32) + 8 * _nbytes((FFN_WIN, FFN_CK), F32)
    return pl.pallas_call(
        functools.partial(_ffn_kernel, d=d, d_ff=d_ff, n_p=n_p, seq_p=seq_p, seq_s=seq_s, first=first),
        grid=(n_tiles,),
        in_specs=[
            row,
            pl.BlockSpec((SUBLANES, d), lambda j: (jnp.maximum((j + first) * halo_per_tile - 1, 0), 0)),
            pl.BlockSpec((SUBLANES, d),
                         lambda j: (jnp.minimum((j + first + 1) * halo_per_tile, n_halo - 1), 0)),
            pl.BlockSpec((None, None, 1, N_MOD * d), mod_idx),
            lay(1, d),
            res(d, 2 * d_ff),
            lay(FFN_K, 2 * d_ff),
            lay(1, 2 * d_ff),
            res(d_ff, d),
        ],
        out_specs=pl.BlockSpec((FFN_SUB, d), lambda j: (j, 0)),
        out_shape=jax.ShapeDtypeStruct((n_tiles * FFN_SUB, d), F32),
        scratch_shapes=[pltpu.VMEM((d // LANES, FFN_WIN, LANES), F32), pltpu.VMEM((FFN_WIN, d), BF16),
                        pltpu.VMEM((FFN_WIN, d_ff), BF16)],
        compiler_params=pltpu.CompilerParams(
            dimension_semantics=("arbitrary",),
            vmem_limit_bytes=_vmem_limit(blocks, scratch + weights, temps)),
        name="conv_ffn",
    )(x, x, x, mod, P["norm2_g"], P["ffn_up"], P["ffn_dw"], P["ffn_dw_b"], P["ffn_down"])


def _dft_tables(n, scale):
    j = np.arange(n, dtype=np.int64)
    ang = ((j[:, None] * j[None, :]) % n).astype(np.float64) * (2.0 * math.pi / n)
    return (np.cos(ang) * scale).astype(np.float32), (np.sin(ang) * scale).astype(np.float32)


def _rope_partner():
    lane = np.arange(HEAD_TILE)
    dim = lane - QK_NOPE
    in_rope = (dim >= 0) & (dim < QK_ROPE)
    quarter = QK_ROPE // 4
    first_half = (dim % (2 * quarter)) // quarter == 0
    perm = np.where(in_rope, np.where(first_half, lane + quarter, lane - quarter), lane)
    return perm, in_rope.astype(np.float32)


def _rope_table(seq, ident_rows):
    rows = seq // GRID_W
    row = np.repeat(np.arange(rows), GRID_W).astype(np.float32)
    col = np.tile(np.arange(GRID_W), rows).astype(np.float32)
    half = QK_ROPE // 2
    inv = (np.float32(ROPE_BASE) ** (-np.arange(0, half, 2, dtype=np.float32) / half)).astype(np.float32)
    ang = np.stack([row[:, None] * inv, col[:, None] * inv], axis=1)
    cos, sin = np.cos(ang), np.sin(ang)
    cos_r = np.stack([cos, cos], axis=2).reshape(seq, QK_ROPE)
    sin_r = np.stack([-sin, sin], axis=2).reshape(seq, QK_ROPE)

    def tile(mid, fill):
        left = np.full((seq, QK_NOPE), fill, np.float32)
        right = np.full((seq, HEAD_TILE - QK_DIM), fill, np.float32)
        return np.concatenate([left, mid, right], axis=1)

    tab = np.concatenate([tile(cos_r, 1.0), tile(sin_r, 0.0)], axis=1)
    ident = np.concatenate([np.ones((ident_rows, HEAD_TILE), np.float32),
                            np.zeros((ident_rows, HEAD_TILE), np.float32)], axis=1)
    return np.concatenate([tab, ident], axis=0).astype(np.float32)


def _prepare_params(W):
    depth, d, _ = W["w_in"].shape
    P = {}
    perm, rope_mask = _rope_partner()
    split = F_W + 2 * CONV_W + Q_LORA + KV_LORA
    w_in = W["w_in"]
    kr_tile = jnp.concatenate([jnp.zeros((depth, d, QK_NOPE), F32), w_in[:, :, split:],
                               jnp.zeros((depth, d, HEAD_TILE - QK_DIM), F32)], axis=2)
    P["w_in"] = w_in.astype(BF16)
    P["w_kr"] = jnp.concatenate([kr_tile, kr_tile[:, :, perm] * rope_mask], axis=2).astype(BF16)
    wq = W["w_q_up"].reshape(depth, Q_LORA, N_HEADS, QK_DIM)
    wq = jnp.pad(wq, ((0, 0), (0, 0), (0, 0), (0, HEAD_TILE - QK_DIM)))
    wq_rot = wq[..., perm][..., QK_NOPE:QK_DIM]
    P["w_q"] = jnp.concatenate([wq.reshape(depth, Q_LORA, HW),
                                wq_rot.reshape(depth, Q_LORA, N_HEADS * QK_ROPE)], axis=2).astype(BF16)
    wkv = W["w_kv_up"].reshape(depth, KV_LORA, N_HEADS, QK_NOPE + V_DIM)
    zk = jnp.zeros((depth, KV_LORA, N_HEADS, HEAD_TILE - QK_NOPE), F32)
    wk = jnp.concatenate([wkv[..., :QK_NOPE], zk], axis=-1)
    zv = jnp.zeros((depth, KV_LORA, N_HEADS, HEAD_TILE - V_DIM), F32)
    wv_even = jnp.concatenate([wkv[..., QK_NOPE:], zv], axis=-1)
    wv_odd = jnp.concatenate([zv, wkv[..., QK_NOPE:]], axis=-1)
    odd = (jnp.arange(N_HEADS) % 2 == 1)[None, None, :, None]
    wv = jnp.where(odd, wv_odd, wv_even)
    P["w_kv"] = jnp.concatenate([wk.reshape(depth, KV_LORA, HW), wv.reshape(depth, KV_LORA, HW)],
                                axis=2).astype(BF16)
    def head_gains(g):
        g = jnp.pad(g, ((0, 0), (0, HEAD_TILE - QK_DIM)))
        return jnp.stack([g, g[:, perm] * rope_mask], axis=1)

    P["gq"] = head_gains(W["qk_q_g"])
    P["gk"] = head_gains(W["qk_k_g"])
    vec = lambda a: a.reshape(depth, 1, a.shape[-1])
    for name in ("norm1_g", "norm2_g", "q_norm_g", "kv_norm_g", "b_gate", "conv_dw_b", "conv_ln_g",
                 "conv_ln_b", "ffn_dw_b"):
        P[name] = vec(W[name])
    P["conv_dw"] = W["conv_dw"]
    P["ffn_dw"] = W["ffn_dw"]
    for name in ("w_gate", "w_fourier", "w_conv_out", "w_mla_out", "w_out", "ffn_up", "ffn_down"):
        P[name] = W[name].astype(BF16)
    return P


def kernel(x_prompt, x_sample, cache_ckv, cache_krope, c, c_ctx, ada_w, ada_b, norm1_g, norm2_g, w_in, w_gate, b_gate, w_fourier, conv_dw, conv_dw_b, conv_ln_g, conv_ln_b, w_conv_out, q_norm_g, w_q_up, kv_norm_g, w_kv_up, qk_q_g, qk_k_g, w_mla_out, w_out, ffn_up, ffn_dw, ffn_dw_b, ffn_down):
    W = dict(norm1_g=norm1_g, norm2_g=norm2_g, w_in=w_in, w_gate=w_gate, b_gate=b_gate,
             w_fourier=w_fourier, conv_dw=conv_dw, conv_dw_b=conv_dw_b, conv_ln_g=conv_ln_g,
             conv_ln_b=conv_ln_b, w_conv_out=w_conv_out, q_norm_g=q_norm_g, w_q_up=w_q_up,
             kv_norm_g=kv_norm_g, w_kv_up=w_kv_up, qk_q_g=qk_q_g, qk_k_g=qk_k_g,
             w_mla_out=w_mla_out, w_out=w_out, ffn_up=ffn_up, ffn_dw=ffn_dw, ffn_dw_b=ffn_dw_b,
             ffn_down=ffn_down)
    n_bp, seq_p, d = x_prompt.shape
    n_bs, seq_s, _ = x_sample.shape
    depth = ada_w.shape[0]
    t_p = n_bp * seq_p
    t_s = n_bs * seq_s
    assert t_p % seq_s == 0 and seq_s % TM == 0 and TM % seq_p == 0 and seq_s % FFN_SUB == 0
    assert seq_p & (seq_p - 1) == 0 and seq_s & (seq_s - 1) == 0 and n_bs + 1 <= 8
    n_p = t_p // TM
    per_seq = seq_s // TM

    P = _prepare_params(W)
    c_cs, s_cs = _dft_tables(F_GROUP_W, 1.0 / math.sqrt(F_GROUP_W))
    P["dft_cs"] = jnp.asarray(np.concatenate([c_cs, s_cs], axis=1), dtype=BF16)
    dft_p = [jnp.asarray(t, dtype=BF16) for t in _dft_tables(seq_p, 1.0 / math.sqrt(seq_p))]
    dft_s = [jnp.asarray(t, dtype=BF16) for t in _dft_tables(seq_s, 1.0 / math.sqrt(seq_s))]
    rope_tab = jnp.asarray(_rope_table(seq_s, TM))

    cvec = jnp.concatenate([c_ctx[None, :], c, jnp.zeros((8 - 1 - n_bs, d), F32)], axis=0)
    mod = _mod_table(cvec, ada_w, ada_b).reshape(depth, 8, 1, N_MOD * d)

    kr_tile = jnp.pad(cache_krope, ((0, 0), (0, 0), (0, 0), (QK_NOPE, HEAD_TILE - QK_DIM)))
    kc, vc = _cache_kv(cache_ckv, kr_tile, P)

    x = (x_prompt.reshape(t_p, d), x_sample.reshape(t_s, d))
    ffn_args = dict(n_p=t_p // FFN_SUB, per_seq=seq_s // FFN_SUB, seq_p=seq_p, seq_s=seq_s)
    ckv_out, kr_out = [], []
    hbm = lambda *arrays: [_in_hbm(a) for a in arrays]
    for l in range(depth):
        outs = _in_proj(x, mod, l, P, rope_tab, n_p=n_p, per_seq=per_seq)
        if l == 0:
            x = outs[-1]
        fa, ug, q, k, v, ckv, kr = outs[:7]
        fa, ug, q, k, v = hbm(fa, ug, q, k, v)
        ckv_out.append(ckv[:t_p].reshape(n_bp, seq_p, KV_LORA))
        kr_out.append(kr[:t_p, QK_NOPE:QK_DIM].reshape(n_bp, seq_p, QK_ROPE))
        yf = hbm(_fourier_prompt(fa, *dft_p, t_p=t_p, seq=seq_p),
                 _fourier_sample(fa, *dft_s, t_p=t_p, n_b=n_bs, seq=seq_s))
        yc = hbm(_conv_module(ug, l, P, first=0, n_seq=n_bp, seq=seq_p),
                 _conv_module(ug, l, P, first=t_p // seq_s, n_seq=n_bs, seq=seq_s))
        ya = hbm(_attn_prompt(q, k, v, n_seq=n_bp, seq=seq_p),
                 _attn_sample(q, k, v, kc, vc, l, t_p=t_p, n_b=n_bs, seq=seq_s, tq=ATTN_TQ))
        x = _merge(x, mod, l, P, yf, yc, ya, n_p=n_p, per_seq=per_seq)
        if l + 1 < depth:
            x = _ffn(x, mod, l, P, **ffn_args)
    y_prompt = _ffn(x, mod, depth - 1, P, first=0, count=t_p // FFN_SUB, **ffn_args)
    y_sample = _ffn(x, mod, depth - 1, P, first=t_p // FFN_SUB, count=t_s // FFN_SUB, **ffn_args)
    return (y_prompt.reshape(n_bp, seq_p, d), y_sample.reshape(n_bs, seq_s, d),
            jnp.stack(ckv_out, axis=1), jnp.stack(kr_out, axis=1))
```

```python
import functools
import math

import numpy as np
import jax
import jax.numpy as jnp
from jax import lax
from jax.experimental import pallas as pl
from jax.experimental.pallas import tpu as pltpu

F32 = jnp.float32
BF16 = jnp.bfloat16

EPS = 1e-6
N_HEADS = 8
QK_NOPE = 64
QK_ROPE = 32
V_DIM = 64
QK_DIM = QK_NOPE + QK_ROPE
F_GROUPS = 4
F_GROUP_W = 128
F_W = F_GROUPS * F_GROUP_W
CONV_W = 512
CONV_K = 31
Q_LORA = 384
KV_LORA = 256
GRID_W = 64
ROPE_BASE = 10000.0
FFN_K = 3
N_MOD = 6

LANES = 128
SUBLANES = 8
V7X_VMEM_BYTES = 64 * 2**20
VMEM_RESERVE_BYTES = 6 * 2**20

HEAD_TILE = LANES
HW = N_HEADS * HEAD_TILE
PAIR_W = 2 * HEAD_TILE
CONV_PAD = 16

TM = 512
ATTN_TQ = 512
CONV_MAX_BLOCK_GROUPS = 16
FFN_SUB = 512
FFN_WIN = FFN_SUB + 2 * SUBLANES
FFN_CK = 256


def _in_hbm(a):
    if isinstance(a, jax.core.Tracer):
        return pltpu.with_memory_space_constraint(a, pltpu.HBM)
    return a


def _vmem_limit(block_bytes, scratch_bytes, temp_bytes):
    need = 2 * block_bytes + scratch_bytes + temp_bytes
    return int(min(need, V7X_VMEM_BYTES - VMEM_RESERVE_BYTES))


def _nbytes(shape, dtype):
    return int(np.prod(shape)) * jnp.dtype(dtype).itemsize


def _dot(a, b):
    return jnp.dot(a, b, preferred_element_type=F32)


def _dot_nt(a, b):
    return lax.dot_general(a, b, (((1,), (1,)), ((), ())), preferred_element_type=F32)


def _rms(x):
    return x * lax.rsqrt(jnp.mean(x * x, axis=-1, keepdims=True) + EPS)


def _sigmoid(x):
    return 1.0 / (1.0 + jnp.exp(-x))


def _silu(x):
    return x * _sigmoid(x)


def _mod_kernel(c_ref, w_ref, b_ref, o_ref):
    s = _silu(c_ref[...]).astype(BF16)
    o_ref[...] = _dot(s, w_ref[...].astype(BF16)) + b_ref[...]


def _mod_table(cvec, ada_w, ada_b):
    depth, d, n = ada_w.shape
    rows = cvec.shape[0]
    tn = n // 4
    blocks = _nbytes((d, tn), F32) + _nbytes((rows, tn), F32) * 2
    return pl.pallas_call(
        _mod_kernel,
        grid=(depth, n // tn),
        in_specs=[
            pl.BlockSpec((rows, d), lambda l, j: (0, 0)),
            pl.BlockSpec((None, d, tn), lambda l, j: (l, 0, j)),
            pl.BlockSpec((None, 1, tn), lambda l, j: (l, 0, j)),
        ],
        out_specs=pl.BlockSpec((None, rows, tn), lambda l, j: (l, 0, j)),
        out_shape=jax.ShapeDtypeStruct((depth, rows, n), F32),
        compiler_params=pltpu.CompilerParams(
            dimension_semantics=("arbitrary", "arbitrary"),
            vmem_limit_bytes=_vmem_limit(blocks, 0, _nbytes((d, tn), BF16) + 2**20)),
        name="adaln_table",
    )(cvec, ada_w, ada_b.reshape(depth, 1, n))


def _head_inv_rms(t):
    ss = jnp.sum(t * t, axis=-1, keepdims=True)
    return lax.rsqrt(ss * (1.0 / QK_DIM) + EPS)


def _keys_values(cb, kr, wkv_ref, k_ref, vt_ref, finish_key):
    lane = lax.broadcasted_iota(jnp.int32, (1, PAIR_W), 1)
    ones_row = jnp.logical_or(lane == V_DIM, lane == HEAD_TILE).astype(F32)
    for pair in range(N_HEADS // 2):
        cols = slice(pair * PAIR_W, (pair + 1) * PAIR_W)
        t2 = _dot(cb, wkv_ref[:, cols])
        for hh in range(2):
            sl = slice((2 * pair + hh) * HEAD_TILE, (2 * pair + hh + 1) * HEAD_TILE)
            k_ref[:, sl] = finish_key(t2[:, hh * HEAD_TILE:(hh + 1) * HEAD_TILE] + kr).astype(BF16)
        v2 = _dot(cb, wkv_ref[:, HW + pair * PAIR_W:HW + (pair + 1) * PAIR_W]) + ones_row
        vt_ref[cols, :] = v2.T.astype(BF16)


def _in_kernel(*refs, d, n_p, paired):
    if paired:
        (xp_ref, xs_ref, mod_ref, g1_ref, win_ref, wkr_ref, cs_ref, qg_ref, wq_ref, kvg_ref, wkv_ref,
         gq_ref, gk_ref, rope_ref,
         fa_ref, ug_ref, q_ref, k_ref, vt_ref, ckv_ref, kr_ref, x_out_ref) = refs
        x = jnp.where(pl.program_id(0) < n_p, xp_ref[...], xs_ref[...])
        x_out_ref[...] = x
    else:
        (x_ref, mod_ref, g1_ref, win_ref, wkr_ref, cs_ref, qg_ref, wq_ref, kvg_ref, wkv_ref,
         gq_ref, gk_ref, rope_ref,
         fa_ref, ug_ref, q_ref, k_ref, vt_ref, ckv_ref, kr_ref) = refs
        x = x_ref[...]
    sh1 = mod_ref[:, 0:d]
    sc1 = mod_ref[:, d:2 * d]
    h = _rms(x) * g1_ref[...] * (1.0 + sc1) + sh1
    hb = h.astype(BF16)
    c_conv = F_W
    c_q = c_conv + 2 * CONV_W
    c0 = c_q

    cos = rope_ref[:, 0:HEAD_TILE]
    sin = rope_ref[:, HEAD_TILE:2 * HEAD_TILE]

    qn = (_rms(_dot(hb, win_ref[:, c0:c0 + Q_LORA])) * qg_ref[...]).astype(BF16)
    c0 += Q_LORA
    scale = math.log2(math.e) / math.sqrt(QK_DIM)
    q_cos = cos * (gq_ref[0:1, :] * scale)
    q_sin = sin * (gq_ref[1:2, :] * scale)
    partners = _dot(qn, wq_ref[:, HW:HW + N_HEADS * QK_ROPE])
    per_tile = HEAD_TILE // QK_ROPE
    for pair in range(N_HEADS // 2):
        t2 = _dot(qn, wq_ref[:, pair * PAIR_W:(pair + 1) * PAIR_W])
        for hd in (2 * pair, 2 * pair + 1):
            sl = slice(hd * HEAD_TILE, (hd + 1) * HEAD_TILE)
            t = t2[:, (hd % 2) * HEAD_TILE:(hd % 2 + 1) * HEAD_TILE]
            tp = partners[:, (hd // per_tile) * HEAD_TILE:(hd // per_tile + 1) * HEAD_TILE]
            shift = (QK_NOPE - (hd % per_tile) * QK_ROPE) % HEAD_TILE
            if shift:
                tp = pltpu.roll(tp, shift, axis=1)
            q_ref[:, sl] = ((t * q_cos + tp * q_sin) * _head_inv_rms(t)).astype(BF16)

    ckv = _rms(_dot(hb, win_ref[:, c0:c0 + KV_LORA])) * kvg_ref[...]
    ckv_ref[...] = ckv
    kr = _dot(hb, wkr_ref[:, 0:HEAD_TILE])
    kr_ref[...] = kr
    k_cos = cos * gk_ref[0:1, :]
    kr_rot = _dot(hb, wkr_ref[:, HEAD_TILE:2 * HEAD_TILE]) * (sin * gk_ref[1:2, :])
    _keys_values(ckv.astype(BF16), kr, wkv_ref, k_ref, vt_ref,
                 lambda t: (t * k_cos + kr_rot) * _head_inv_rms(t))

    fb = _dot(hb, win_ref[:, 0:F_W]).astype(BF16)
    for g in range(F_GROUPS):
        a = _dot(fb[:, g * F_GROUP_W:(g + 1) * F_GROUP_W], cs_ref[...])
        fa_ref[:, g * F_GROUP_W:(g + 1) * F_GROUP_W] = a[:, :F_GROUP_W].astype(BF16)
        fa_ref[:, F_W + g * F_GROUP_W:F_W + (g + 1) * F_GROUP_W] = a[:, F_GROUP_W:].astype(BF16)

    ga = _dot(hb, win_ref[:, c_conv:c_conv + CONV_W])
    gg = _dot(hb, win_ref[:, c_conv + CONV_W:c_conv + 2 * CONV_W])
    ug_ref[...] = ga * _sigmoid(gg)


def _in_proj(x, mod, l, P, rope_tab, *, n_p, per_seq):
    paired = isinstance(x, tuple)
    d = (x[0] if paired else x).shape[1]
    t_rows = x[0].shape[0] + x[1].shape[0] if paired else x.shape[0]
    n_tiles = t_rows // TM
    n_s = n_tiles - n_p
    win_cols = P["w_in"].shape[-1]

    def mod_idx(i):
        return (l, jnp.where(i < n_p, 0, 1 + (i - n_p) // per_seq), 0, 0)

    def rope_idx(i):
        return (jnp.where(i < n_p, per_seq, (i - n_p) % per_seq), 0)

    row = lambda w: pl.BlockSpec((TM, w), lambda i: (i, 0))
    lay = lambda *s: pl.BlockSpec((None,) + s, lambda i: (l,) + (0,) * len(s))
    wq_cols = P["w_q"].shape[-1]
    out_shapes = [
        jax.ShapeDtypeStruct((t_rows, 2 * F_W), BF16),
        jax.ShapeDtypeStruct((t_rows, CONV_W), F32),
        jax.ShapeDtypeStruct((t_rows, HW), BF16),
        jax.ShapeDtypeStruct((t_rows, HW), BF16),
        jax.ShapeDtypeStruct((HW, t_rows), BF16),
        jax.ShapeDtypeStruct((t_rows, KV_LORA), F32),
        jax.ShapeDtypeStruct((t_rows, HEAD_TILE), F32),
    ]
    if paired:
        out_shapes.append(jax.ShapeDtypeStruct((t_rows, d), F32))
        x_args = list(x)
        x_specs = [pl.BlockSpec((TM, d), lambda i: (jnp.minimum(i, n_p - 1), 0)),
                   pl.BlockSpec((TM, d), lambda i: (jnp.clip(i - n_p, 0, n_s - 1), 0))]
    else:
        x_args = [x]
        x_specs = [row(d)]
    out_specs = [pl.BlockSpec((HW, TM), lambda i: (0, i)) if s.shape[0] == HW else row(s.shape[1])
                 for s in out_shapes]
    blocks = (len(x_args) * _nbytes((TM, d), F32) + _nbytes((d, win_cols + 2 * HEAD_TILE), BF16)
              + _nbytes((Q_LORA, wq_cols), BF16)
              + _nbytes((KV_LORA, 2 * HW), BF16) + _nbytes((TM, 2 * HEAD_TILE), F32)
              + sum(_nbytes(s.shape, s.dtype) // (t_rows // TM) for s in out_shapes))
    return pl.pallas_call(
        functools.partial(_in_kernel, d=d, n_p=n_p, paired=paired),
        grid=(n_tiles,),
        in_specs=x_specs + [
            pl.BlockSpec((None, None, 1, N_MOD * d), mod_idx),
            lay(1, d),
            lay(d, win_cols),
            lay(d, 2 * HEAD_TILE),
            pl.BlockSpec((F_GROUP_W, 2 * F_GROUP_W), lambda i: (0, 0)),
            lay(1, Q_LORA),
            lay(Q_LORA, wq_cols),
            lay(1, KV_LORA),
            lay(KV_LORA, 2 * HW),
            lay(2, HEAD_TILE),
            lay(2, HEAD_TILE),
            pl.BlockSpec((TM, 2 * HEAD_TILE), rope_idx),
        ],
        out_specs=out_specs,
        out_shape=out_shapes,
        compiler_params=pltpu.CompilerParams(
            dimension_semantics=("arbitrary",),
            vmem_limit_bytes=_vmem_limit(blocks, 0, 12 * _nbytes((TM, d), F32))),
        name="in_proj",
    )(*x_args, mod, P["norm1_g"], P["w_in"], P["w_kr"], P["dft_cs"], P["q_norm_g"], P["w_q"],
      P["kv_norm_g"], P["w_kv"], P["gq"], P["gk"], rope_tab)


def _cache_kv_kernel(ckv_ref, kr_ref, wkv_ref, gk_ref, k_ref, vt_ref):
    k_gain = gk_ref[0:1, :]
    _keys_values(ckv_ref[...].astype(BF16), kr_ref[...], wkv_ref, k_ref, vt_ref,
                 lambda t: t * k_gain * _head_inv_rms(t))


def _cache_kv(cache_ckv, cache_kr_tile, P):
    nb, depth, past, _ = cache_ckv.shape
    blk = lambda w: pl.BlockSpec((None, None, past, w), lambda b, l: (b, l, 0, 0))
    lay = lambda *s: pl.BlockSpec((None,) + s, lambda b, l: (l,) + (0,) * len(s))
    blocks = (_nbytes((past, KV_LORA + HEAD_TILE), F32) + _nbytes((KV_LORA, 2 * HW), BF16)
              + 2 * _nbytes((past, HW), BF16))
    return pl.pallas_call(
        _cache_kv_kernel,
        grid=(nb, depth),
        in_specs=[blk(KV_LORA), blk(HEAD_TILE), lay(KV_LORA, 2 * HW), lay(2, HEAD_TILE)],
        out_specs=[blk(HW), pl.BlockSpec((None, None, HW, past), lambda b, l: (b, l, 0, 0))],
        out_shape=[jax.ShapeDtypeStruct((nb, depth, past, HW), BF16),
                   jax.ShapeDtypeStruct((nb, depth, HW, past), BF16)],
        compiler_params=pltpu.CompilerParams(
            dimension_semantics=("arbitrary", "arbitrary"),
            vmem_limit_bytes=_vmem_limit(blocks, 0, 8 * _nbytes((past, HW), F32))),
        name="cache_kv",
    )(cache_ckv, cache_kr_tile, P["w_kv"], P["gk"])


def _fourier_kernel(fa_ref, c_ref, s_ref, o_ref, *, seq, n_seq):
    for s in range(n_seq):
        rows = slice(s * seq, (s + 1) * seq)
        y = _dot(c_ref[...], fa_ref[rows, 0:F_W]) - _dot(s_ref[...], fa_ref[rows, F_W:2 * F_W])
        o_ref[rows, :] = y.astype(BF16)


def _fourier_prompt(fa, dft_c, dft_s, *, t_p, seq):
    n_seq = TM // seq
    blocks = _nbytes((TM, 2 * F_W), BF16) + 2 * _nbytes((seq, seq), BF16) + _nbytes((TM, F_W), BF16)
    return pl.pallas_call(
        functools.partial(_fourier_kernel, seq=seq, n_seq=n_seq),
        grid=(t_p // TM,),
        in_specs=[
            pl.BlockSpec((TM, 2 * F_W), lambda i: (i, 0)),
            pl.BlockSpec((seq, seq), lambda i: (0, 0)),
            pl.BlockSpec((seq, seq), lambda i: (0, 0)),
        ],
        out_specs=pl.BlockSpec((TM, F_W), lambda i: (i, 0)),
        out_shape=jax.ShapeDtypeStruct((t_p, F_W), BF16),
        compiler_params=pltpu.CompilerParams(
            dimension_semantics=("arbitrary",),
            vmem_limit_bytes=_vmem_limit(blocks, 0, 4 * _nbytes((TM, F_W), F32))),
        name="fourier_prompt",
    )(fa, dft_c, dft_s)


def _fourier_sample_kernel(fa_ref, c_ref, s_ref, o_ref):
    y = _dot(c_ref[...], fa_ref[:, 0:F_W]) - _dot(s_ref[...], fa_ref[:, F_W:2 * F_W])
    o_ref[...] = y.astype(BF16)


def _fourier_sample(fa, dft_c, dft_s, *, t_p, n_b, seq):
    tr = TM
    per = seq // tr
    first = t_p // seq
    blocks = _nbytes((seq, 2 * F_W), BF16) + 2 * _nbytes((tr, seq), BF16) + _nbytes((tr, F_W), BF16)
    return pl.pallas_call(
        _fourier_sample_kernel,
        grid=(n_b, per),
        in_specs=[
            pl.BlockSpec((seq, 2 * F_W), lambda b, j: (first + b, 0)),
            pl.BlockSpec((tr, seq), lambda b, j: (j, 0)),
            pl.BlockSpec((tr, seq), lambda b, j: (j, 0)),
        ],
        out_specs=pl.BlockSpec((tr, F_W), lambda b, j: (b * per + j, 0)),
        out_shape=jax.ShapeDtypeStruct((n_b * seq, F_W), BF16),
        compiler_params=pltpu.CompilerParams(
            dimension_semantics=("arbitrary", "arbitrary"),
            vmem_limit_bytes=_vmem_limit(blocks, 0, 4 * _nbytes((tr, F_W), F32))),
        name="fourier_sample",
    )(fa, dft_c, dft_s)


def _conv_block_groups(n_grp):
    return max(g for g in range(1, CONV_MAX_BLOCK_GROUPS + 1) if n_grp % (2 * g) == 0)


def _conv_kernel(u_ref, dw_ref, dwb_ref, lng_ref, lnb_ref, o_ref, nat_ref, ext_ref, *, seq):
    half = CONV_K // 2
    win = seq + 2 * CONV_PAD
    n_grp = win // SUBLANES
    n_slab = CONV_W // LANES
    lead = half * SUBLANES
    zeros = jnp.zeros((CONV_PAD, LANES), F32)
    for k in range(n_slab):
        nat_ref[k, 0:CONV_PAD, :] = zeros
        nat_ref[k, CONV_PAD + seq:win, :] = zeros
        nat_ref[k, CONV_PAD:CONV_PAD + seq, :] = u_ref[:, k * LANES:(k + 1) * LANES]
    for k in range(n_slab):
        for v in range(n_grp):
            g = nat_ref[k, pl.ds(v, SUBLANES, stride=n_grp), :]
            ext_ref[k, lead + v * SUBLANES:lead + (v + 1) * SUBLANES, :] = g
            if v >= n_grp - half:
                j = v - (n_grp - half)
                ext_ref[k, j * SUBLANES:(j + 1) * SUBLANES, :] = pltpu.roll(g, 1, axis=0)
            if v < half:
                j = half + n_grp + v
                ext_ref[k, j * SUBLANES:(j + 1) * SUBLANES, :] = pltpu.roll(g, SUBLANES - 1, axis=0)

    blk_grp = _conv_block_groups(n_grp)
    blk = blk_grp * SUBLANES
    n_blk = n_grp // blk_grp

    for k in range(n_slab):
        lanes = slice(k * LANES, (k + 1) * LANES)

        def conv_body(b, carry, k=k, lanes=lanes):
            r0 = pl.multiple_of(b * blk, SUBLANES)
            acc = jnp.zeros((blk, LANES), F32)
            for t in range(CONV_K):
                acc = acc + dw_ref[t:t + 1, lanes] * ext_ref[k, pl.ds(r0 + t * SUBLANES, blk), :]
            ext_ref[k, pl.ds(r0, blk), :] = acc + dwb_ref[:, lanes]
            return carry

        lax.fori_loop(0, n_blk, conv_body, 0)

    def norm_block(r0):
        acc = jnp.concatenate([ext_ref[k, pl.ds(r0, blk), :] for k in range(n_slab)], axis=1)
        mu = jnp.mean(acc, axis=-1, keepdims=True)
        cen = acc - mu
        var = jnp.mean(cen * cen, axis=-1, keepdims=True)
        y = _silu(cen * lax.rsqrt(var + EPS) * lng_ref[...] + lnb_ref[...])
        for k in range(n_slab):
            ext_ref[k, pl.ds(r0, blk), :] = y[:, k * LANES:(k + 1) * LANES]

    per_iter = max(u for u in (1, 2, 4) if n_blk % u == 0)

    def norm_body(b, carry):
        for u in range(per_iter):
            norm_block(pl.multiple_of((b + u * (n_blk // per_iter)) * blk, SUBLANES))
        return carry

    lax.fori_loop(0, n_blk // per_iter, norm_body, 0)

    for k in range(n_slab):
        for v in range(n_grp):
            nat_ref[k, pl.ds(v, SUBLANES, stride=n_grp), :] = ext_ref[k, v * SUBLANES:(v + 1) * SUBLANES, :]
        o_ref[:, k * LANES:(k + 1) * LANES] = nat_ref[k, CONV_PAD:CONV_PAD + seq, :].astype(BF16)


def _conv_module(ug, l, P, *, first, n_seq, seq):
    lay = lambda *s: pl.BlockSpec((None,) + s, lambda i: (l,) + (0,) * len(s))
    blocks = _nbytes((seq, CONV_W), F32) + _nbytes((seq, CONV_W), BF16) + _nbytes((CONV_K + 4, CONV_W), F32)
    win = seq + 2 * CONV_PAD
    ext = win + 2 * (CONV_K // 2) * SUBLANES
    scratch = _nbytes((win + ext, CONV_W), F32)
    return pl.pallas_call(
        functools.partial(_conv_kernel, seq=seq),
        grid=(n_seq,),
        in_specs=[
            pl.BlockSpec((seq, CONV_W), lambda i: (first + i, 0)),
            lay(CONV_K, CONV_W), lay(1, CONV_W), lay(1, CONV_W), lay(1, CONV_W),
        ],
        out_specs=pl.BlockSpec((seq, CONV_W), lambda i: (i, 0)),
        out_shape=jax.ShapeDtypeStruct((n_seq * seq, CONV_W), BF16),
        scratch_shapes=[pltpu.VMEM((CONV_W // LANES, win, LANES), F32),
                        pltpu.VMEM((CONV_W // LANES, ext, LANES), F32)],
        compiler_params=pltpu.CompilerParams(
            dimension_semantics=("arbitrary",),
            vmem_limit_bytes=_vmem_limit(blocks, scratch, 4 * 2**20)),
        name=f"conv_seq{seq}",
    )(ug, P["conv_dw"], P["conv_dw_b"], P["conv_ln_g"], P["conv_ln_b"])


def _attn_kernel(*refs, with_cache):
    if with_cache:
        q_ref, k_ref, vt_ref, kc_ref, vct_ref, o_ref = refs
    else:
        q_ref, k_ref, vt_ref, o_ref = refs
    tq = q_ref.shape[0]
    row = lax.broadcasted_iota(jnp.int32, (HEAD_TILE, tq), 0)

    def scores(hd):
        sl = slice(hd * HEAD_TILE, (hd + 1) * HEAD_TILE)
        qh = q_ref[:, sl]
        return _dot_nt(k_ref[:, sl], qh), (_dot_nt(kc_ref[:, sl], qh) if with_cache else None)

    nxt = scores(0)
    even = None
    for hd in range(N_HEADS):
        sl = slice(hd * HEAD_TILE, (hd + 1) * HEAD_TILE)
        s_m, s_c = nxt
        if hd + 1 < N_HEADS:
            nxt = scores(hd + 1)
        m = jnp.max(s_m, axis=0, keepdims=True)
        if with_cache:
            m = jnp.maximum(m, jnp.max(s_c, axis=0, keepdims=True))
        out_t = _dot(vt_ref[sl, :], jnp.exp2(s_m - m).astype(BF16))
        if with_cache:
            out_t = out_t + _dot(vct_ref[sl, :], jnp.exp2(s_c - m).astype(BF16))
        den_row = V_DIM if hd % 2 == 0 else 0
        out_t = out_t * (1.0 / out_t[den_row:den_row + 1, :])
        if hd % 2 == 0:
            even = out_t
        else:
            pair_t = jnp.where(row < V_DIM, even, out_t)
            o_ref[:, (hd // 2) * HEAD_TILE:(hd // 2 + 1) * HEAD_TILE] = pair_t.T.astype(BF16)


def _attn_prompt(q, k, vt, *, n_seq, seq):
    blk = pl.BlockSpec((seq, HW), lambda i: (i, 0))
    blocks = 3 * _nbytes((seq, HW), BF16) + _nbytes((seq, N_HEADS * V_DIM), BF16)
    return pl.pallas_call(
        functools.partial(_attn_kernel, with_cache=False),
        grid=(n_seq,),
        in_specs=[blk, blk, pl.BlockSpec((HW, seq), lambda i: (0, i))],
        out_specs=pl.BlockSpec((seq, N_HEADS * V_DIM), lambda i: (i, 0)),
        out_shape=jax.ShapeDtypeStruct((n_seq * seq, N_HEADS * V_DIM), BF16),
        compiler_params=pltpu.CompilerParams(
            dimension_semantics=("arbitrary",),
            vmem_limit_bytes=_vmem_limit(blocks, 0, 8 * _nbytes((seq, seq), F32) + 2**21)),
        name="attn_prompt",
    )(q, k, vt)


def _attn_sample(q, k, vt, kc, vct, l, *, t_p, n_b, seq, tq):
    per = seq // tq
    first_q = t_p // tq
    first_s = t_p // seq
    past = kc.shape[2]
    once = pl.Buffered(1)
    k_blk = pl.BlockSpec((seq, HW), lambda b, j: (first_s + b, 0), pipeline_mode=once)
    vt_blk = pl.BlockSpec((HW, seq), lambda b, j: (0, first_s + b), pipeline_mode=once)
    kc_blk = pl.BlockSpec((None, None, past, HW), lambda b, j: (b, l, 0, 0), pipeline_mode=once)
    vct_blk = pl.BlockSpec((None, None, HW, past), lambda b, j: (b, l, 0, 0), pipeline_mode=once)
    blocks = _nbytes((tq, HW), BF16) + _nbytes((tq, N_HEADS * V_DIM), BF16)
    resident = 2 * _nbytes((seq + past, HW), BF16)
    return pl.pallas_call(
        functools.partial(_attn_kernel, with_cache=True),
        grid=(n_b, per),
        in_specs=[pl.BlockSpec((tq, HW), lambda b, j: (first_q + b * per + j, 0)),
                  k_blk, vt_blk, kc_blk, vct_blk],
        out_specs=pl.BlockSpec((tq, N_HEADS * V_DIM), lambda b, j: (b * per + j, 0)),
        out_shape=jax.ShapeDtypeStruct((n_b * seq, N_HEADS * V_DIM), BF16),
        compiler_params=pltpu.CompilerParams(
            dimension_semantics=("arbitrary", "arbitrary"),
            vmem_limit_bytes=_vmem_limit(blocks, resident, 8 * _nbytes((tq, seq + past), F32))),
        name="attn_sample",
    )(q, k, vt, kc, vct)


def _merge_kernel(x_ref, mod_ref, g1_ref, fp_ref, fs_ref, cp_ref, cs_ref, ap_ref, as_ref,
                  wg_ref, bg_ref, wf_ref, wc_ref, wa_ref, wo_ref, o_ref, *, d, n_p):
    is_prompt = pl.program_id(0) < n_p
    x = x_ref[...]
    sh1 = mod_ref[:, 0:d]
    sc1 = mod_ref[:, d:2 * d]
    g1 = mod_ref[:, 2 * d:3 * d]
    hb = (_rms(x) * g1_ref[...] * (1.0 + sc1) + sh1).astype(BF16)
    merged = None
    branches = ((fp_ref, fs_ref, wf_ref), (cp_ref, cs_ref, wc_ref), (ap_ref, as_ref, wa_ref))
    for b, (yp_ref, ys_ref, w_ref) in enumerate(branches):
        y_in = jnp.where(is_prompt, yp_ref[...], ys_ref[...])
        y = _dot(y_in, w_ref[...])
        gate = _sigmoid(_dot(hb, wg_ref[:, b * d:(b + 1) * d]) + bg_ref[:, b * d:(b + 1) * d])
        merged = gate * y if merged is None else merged + gate * y
    o_ref[...] = x + g1 * _dot(merged.astype(BF16), wo_ref[...])


def _merge(x, mod, l, P, yf, yc, ya, *, n_p, per_seq):
    t_rows, d = x.shape
    n_tiles = t_rows // TM
    n_s = n_tiles - n_p

    def mod_idx(i):
        return (l, jnp.where(i < n_p, 0, 1 + (i - n_p) // per_seq), 0, 0)

    row = pl.BlockSpec((TM, d), lambda i: (i, 0))
    lay = lambda *s: pl.BlockSpec((None,) + s, lambda i: (l,) + (0,) * len(s))
    p_blk = pl.BlockSpec((TM, F_W), lambda i: (jnp.minimum(i, n_p - 1), 0))
    s_blk = pl.BlockSpec((TM, F_W), lambda i: (jnp.clip(i - n_p, 0, n_s - 1), 0))
    blocks = (2 * _nbytes((TM, d), F32) + 6 * _nbytes((TM, F_W), BF16) + _nbytes((d, 3 * d), BF16)
              + 3 * _nbytes((F_W, d), BF16) + _nbytes((d, d), BF16))
    return pl.pallas_call(
        functools.partial(_merge_kernel, d=d, n_p=n_p),
        grid=(n_tiles,),
        in_specs=[row, pl.BlockSpec((None, None, 1, N_MOD * d), mod_idx), lay(1, d),
                  p_blk, s_blk, p_blk, s_blk, p_blk, s_blk,
                  lay(d, 3 * d), lay(1, 3 * d), lay(F_W, d), lay(CONV_W, d), lay(N_HEADS * V_DIM, d),
                  lay(d, d)],
        out_specs=row,
        out_shape=jax.ShapeDtypeStruct((t_rows, d), F32),
        compiler_params=pltpu.CompilerParams(
            dimension_semantics=("arbitrary",),
            vmem_limit_bytes=_vmem_limit(blocks, 0, 8 * _nbytes((TM, d), F32))),
        name="merge",
    )(x, mod, P["norm1_g"], yf[0], yf[1], yc[0], yc[1], ya[0], ya[1],
      P["w_gate"], P["b_gate"], P["w_fourier"], P["w_conv_out"], P["w_mla_out"], P["w_out"])


def _ffn_kernel(x_ref, xp_ref, xn_ref, mod_ref, g2_ref, up_ref, dw_ref, b_ref, dn_ref,
                o_ref, nat_ref, h_ref, act_ref, *, d, d_ff, n_p, seq_p, seq_s, first):
    i = pl.program_id(0) + first
    sh2 = mod_ref[:, 3 * d:4 * d]
    sc2 = mod_ref[:, 4 * d:5 * d]
    g2 = mod_ref[:, 5 * d:6 * d]

    def norm_mod(x):
        return _rms(x) * g2_ref[...] * (1.0 + sc2) + sh2

    is_prompt = i < n_p
    period_mask = jnp.where(is_prompt, seq_p - 1, seq_s - 1)
    has_prev = ((i * FFN_SUB) & period_mask) != 0
    has_next = (((i + 1) * FFN_SUB) & period_mask) != 0
    h_prev = jnp.where(has_prev, norm_mod(xp_ref[...]), 0.0)
    h_next = jnp.where(has_next, norm_mod(xn_ref[...]), 0.0)
    h_main = norm_mod(x_ref[...])

    n_grp = FFN_WIN // SUBLANES
    for k in range(d // LANES):
        lanes = slice(k * LANES, (k + 1) * LANES)
        nat_ref[k, 0:SUBLANES, :] = h_prev[:, lanes]
        nat_ref[k, SUBLANES:SUBLANES + FFN_SUB, :] = h_main[:, lanes]
        nat_ref[k, SUBLANES + FFN_SUB:FFN_WIN, :] = h_next[:, lanes]
    for k in range(d // LANES):
        groups = [nat_ref[k, pl.ds(v, SUBLANES, stride=n_grp), :] for v in range(n_grp)]
        h_ref[:, k * LANES:(k + 1) * LANES] = jnp.concatenate(groups, axis=0).astype(BF16)

    sub = lax.broadcasted_iota(jnp.int32, (SUBLANES, 1), 0)
    no_prev = jnp.zeros((SUBLANES, 1), jnp.bool_)
    no_next = jnp.zeros((SUBLANES, 1), jnp.bool_)
    for edge in range(seq_p, FFN_SUB, seq_p):
        assert (edge + SUBLANES) % n_grp == 0
        no_prev = jnp.logical_or(no_prev, sub == (edge + SUBLANES) // n_grp)
        no_next = jnp.logical_or(no_next, sub == (edge + SUBLANES - 1) // n_grp)
    no_prev = jnp.logical_and(is_prompt, no_prev)
    no_next = jnp.logical_and(is_prompt, no_next)

    def conv(u, c0):
        last = u[FFN_WIN - SUBLANES:FFN_WIN, :]
        first = u[0:SUBLANES, :]
        wrap_prev = jnp.where(no_prev, 0.0, pltpu.roll(last, 1, axis=0))
        wrap_next = jnp.where(no_next, 0.0, pltpu.roll(first, SUBLANES - 1, axis=0))
        prev = jnp.concatenate([wrap_prev, u[0:FFN_WIN - SUBLANES, :]], axis=0)
        nxt = jnp.concatenate([u[SUBLANES:FFN_WIN, :], wrap_next], axis=0)
        w = dw_ref[:, c0:c0 + FFN_CK]
        return w[0:1, :] * prev + w[1:2, :] * u + w[2:3, :] * nxt + b_ref[:, c0:c0 + FFN_CK]

    hx = h_ref[...]

    def up_proj(c):
        return (_dot(hx, up_ref[:, c * FFN_CK:(c + 1) * FFN_CK]),
                _dot(hx, up_ref[:, d_ff + c * FFN_CK:d_ff + (c + 1) * FFN_CK]))

    n_ck = d_ff // FFN_CK
    half_ck = (n_ck + 1) // 2
    part = None
    nxt = up_proj(0)
    for c in range(n_ck):
        u_a, u_b = nxt
        if c + 1 < n_ck:
            nxt = up_proj(c + 1)
        if c == half_ck:
            part = _dot(act_ref[:, 0:half_ck * FFN_CK], dn_ref[0:half_ck * FFN_CK, :])
        act = _silu(conv(u_a, c * FFN_CK)) * conv(u_b, d_ff + c * FFN_CK)
        act_ref[:, c * FFN_CK:(c + 1) * FFN_CK] = act.astype(BF16)
    y = part + _dot(act_ref[:, half_ck * FFN_CK:d_ff], dn_ref[half_ck * FFN_CK:d_ff, :])
    for k in range(d // LANES):
        lanes = slice(k * LANES, (k + 1) * LANES)
        for v in range(n_grp):
            nat_ref[k, pl.ds(v, SUBLANES, stride=n_grp), :] = y[v * SUBLANES:(v + 1) * SUBLANES, lanes]
        o_ref[:, lanes] = x_ref[:, lanes] + g2[:, lanes] * nat_ref[k, SUBLANES:SUBLANES + FFN_SUB, :]


def _ffn(x, mod, l, P, *, n_p, per_seq, seq_p, seq_s, first=0, count=None):
    t_rows, d = x.shape
    d_ff = P["ffn_down"].shape[1]
    n_tiles = t_rows // FFN_SUB if count is None else count
    halo_per_tile = FFN_SUB // SUBLANES
    n_halo = t_rows // SUBLANES

    def mod_idx(j):
        i = j + first
        return (l, jnp.where(i < n_p, 0, 1 + (i - n_p) // per_seq), 0, 0)

    row = pl.BlockSpec((FFN_SUB, d), lambda j: (j + first, 0))
    once = pl.Buffered(1)
    lay = lambda *s: pl.BlockSpec((None,) + s, lambda i: (l,) + (0,) * len(s))
    res = lambda *s: pl.BlockSpec((None,) + s, lambda i: (l,) + (0,) * len(s), pipeline_mode=once)
    weights = _nbytes((d, 2 * d_ff), BF16) + _nbytes((d_ff, d), BF16)
    blocks = 2 * _nbytes((FFN_SUB + 2 * SUBLANES, d), F32) + 4 * _nbytes((8, 2 * d_ff), F32)
    scratch = (_nbytes((FFN_WIN, d), F32) + _nbytes((FFN_WIN, d), BF16) + _nbytes((FFN_WIN, d_ff), BF16))
    temps = 4 * _nbytes((FFN_WIN, d), F32) + 8 * _nbytes((FFN_WIN, FFN_CK), F32)
    return pl.pallas_call(
        functools.partial(_ffn_kernel, d=d, d_ff=d_ff, n_p=n_p, seq_p=seq_p, seq_s=seq_s, first=first),
        grid=(n_tiles,),
        in_specs=[
            row,
            pl.BlockSpec((SUBLANES, d), lambda j: (jnp.maximum((j + first) * halo_per_tile - 1, 0), 0)),
            pl.BlockSpec((SUBLANES, d),
                         lambda j: (jnp.minimum((j + first + 1) * halo_per_tile, n_halo - 1), 0)),
            pl.BlockSpec((None, None, 1, N_MOD * d), mod_idx),
            lay(1, d),
            res(d, 2 * d_ff),
            lay(FFN_K, 2 * d_ff),
            lay(1, 2 * d_ff),
            res(d_ff, d),
        ],
        out_specs=pl.BlockSpec((FFN_SUB, d), lambda j: (j, 0)),
        out_shape=jax.ShapeDtypeStruct((n_tiles * FFN_SUB, d), F32),
        scratch_shapes=[pltpu.VMEM((d // LANES, FFN_WIN, LANES), F32), pltpu.VMEM((FFN_WIN, d), BF16),
                        pltpu.VMEM((FFN_WIN, d_ff), BF16)],
        compiler_params=pltpu.CompilerParams(
            dimension_semantics=("arbitrary",),
            vmem_limit_bytes=_vmem_limit(blocks, scratch + weights, temps)),
        name="conv_ffn",
    )(x, x, x, mod, P["norm2_g"], P["ffn_up"], P["ffn_dw"], P["ffn_dw_b"], P["ffn_down"])


def _dft_tables(n, scale):
    j = np.arange(n, dtype=np.int64)
    ang = ((j[:, None] * j[None, :]) % n).astype(np.float64) * (2.0 * math.pi / n)
    return (np.cos(ang) * scale).astype(np.float32), (np.sin(ang) * scale).astype(np.float32)


def _rope_partner():
    lane = np.arange(HEAD_TILE)
    dim = lane - QK_NOPE
    in_rope = (dim >= 0) & (dim < QK_ROPE)
    quarter = QK_ROPE // 4
    first_half = (dim % (2 * quarter)) // quarter == 0
    perm = np.where(in_rope, np.where(first_half, lane + quarter, lane - quarter), lane)
    return perm, in_rope.astype(np.float32)


def _rope_table(seq, ident_rows):
    rows = seq // GRID_W
    row = np.repeat(np.arange(rows), GRID_W).astype(np.float32)
    col = np.tile(np.arange(GRID_W), rows).astype(np.float32)
    half = QK_ROPE // 2
    inv = (np.float32(ROPE_BASE) ** (-np.arange(0, half, 2, dtype=np.float32) / half)).astype(np.float32)
    ang = np.stack([row[:, None] * inv, col[:, None] * inv], axis=1)
    cos, sin = np.cos(ang), np.sin(ang)
    cos_r = np.stack([cos, cos], axis=2).reshape(seq, QK_ROPE)
    sin_r = np.stack([-sin, sin], axis=2).reshape(seq, QK_ROPE)

    def tile(mid, fill):
        left = np.full((seq, QK_NOPE), fill, np.float32)
        right = np.full((seq, HEAD_TILE - QK_DIM), fill, np.float32)
        return np.concatenate([left, mid, right], axis=1)

    tab = np.concatenate([tile(cos_r, 1.0), tile(sin_r, 0.0)], axis=1)
    ident = np.concatenate([np.ones((ident_rows, HEAD_TILE), np.float32),
                            np.zeros((ident_rows, HEAD_TILE), np.float32)], axis=1)
    return np.concatenate([tab, ident], axis=0).astype(np.float32)


def _prepare_params(W):
    depth, d, _ = W["w_in"].shape
    P = {}
    perm, rope_mask = _rope_partner()
    split = F_W + 2 * CONV_W + Q_LORA + KV_LORA
    w_in = W["w_in"]
    kr_tile = jnp.concatenate([jnp.zeros((depth, d, QK_NOPE), F32), w_in[:, :, split:],
                               jnp.zeros((depth, d, HEAD_TILE - QK_DIM), F32)], axis=2)
    P["w_in"] = w_in.astype(BF16)
    P["w_kr"] = jnp.concatenate([kr_tile, kr_tile[:, :, perm] * rope_mask], axis=2).astype(BF16)
    wq = W["w_q_up"].reshape(depth, Q_LORA, N_HEADS, QK_DIM)
    wq = jnp.pad(wq, ((0, 0), (0, 0), (0, 0), (0, HEAD_TILE - QK_DIM)))
    wq_rot = wq[..., perm][..., QK_NOPE:QK_DIM]
    P["w_q"] = jnp.concatenate([wq.reshape(depth, Q_LORA, HW),
                                wq_rot.reshape(depth, Q_LORA, N_HEADS * QK_ROPE)], axis=2).astype(BF16)
    wkv = W["w_kv_up"].reshape(depth, KV_LORA, N_HEADS, QK_NOPE + V_DIM)
    zk = jnp.zeros((depth, KV_LORA, N_HEADS, HEAD_TILE - QK_NOPE), F32)
    wk = jnp.concatenate([wkv[..., :QK_NOPE], zk], axis=-1)
    zv = jnp.zeros((depth, KV_LORA, N_HEADS, HEAD_TILE - V_DIM), F32)
    wv_even = jnp.concatenate([wkv[..., QK_NOPE:], zv], axis=-1)
    wv_odd = jnp.concatenate([zv, wkv[..., QK_NOPE:]], axis=-1)
    odd = (jnp.arange(N_HEADS) % 2 == 1)[None, None, :, None]
    wv = jnp.where(odd, wv_odd, wv_even)
    P["w_kv"] = jnp.concatenate([wk.reshape(depth, KV_LORA, HW), wv.reshape(depth, KV_LORA, HW)],
                                axis=2).astype(BF16)
    def head_gains(g):
        g = jnp.pad(g, ((0, 0), (0, HEAD_TILE - QK_DIM)))
        return jnp.stack([g, g[:, perm] * rope_mask], axis=1)

    P["gq"] = head_gains(W["qk_q_g"])
    P["gk"] = head_gains(W["qk_k_g"])
    vec = lambda a: a.reshape(depth, 1, a.shape[-1])
    for name in ("norm1_g", "norm2_g", "q_norm_g", "kv_norm_g", "b_gate", "conv_dw_b", "conv_ln_g",
                 "conv_ln_b", "ffn_dw_b"):
        P[name] = vec(W[name])
    P["conv_dw"] = W["conv_dw"]
    P["ffn_dw"] = W["ffn_dw"]
    for name in ("w_gate", "w_fourier", "w_conv_out", "w_mla_out", "w_out", "ffn_up", "ffn_down"):
        P[name] = W[name].astype(BF16)
    return P


def kernel(x_prompt, x_sample, cache_ckv, cache_krope, c, c_ctx, ada_w, ada_b, norm1_g, norm2_g, w_in, w_gate, b_gate, w_fourier, conv_dw, conv_dw_b, conv_ln_g, conv_ln_b, w_conv_out, q_norm_g, w_q_up, kv_norm_g, w_kv_up, qk_q_g, qk_k_g, w_mla_out, w_out, ffn_up, ffn_dw, ffn_dw_b, ffn_down):
    W = dict(norm1_g=norm1_g, norm2_g=norm2_g, w_in=w_in, w_gate=w_gate, b_gate=b_gate,
             w_fourier=w_fourier, conv_dw=conv_dw, conv_dw_b=conv_dw_b, conv_ln_g=conv_ln_g,
             conv_ln_b=conv_ln_b, w_conv_out=w_conv_out, q_norm_g=q_norm_g, w_q_up=w_q_up,
             kv_norm_g=kv_norm_g, w_kv_up=w_kv_up, qk_q_g=qk_q_g, qk_k_g=qk_k_g,
             w_mla_out=w_mla_out, w_out=w_out, ffn_up=ffn_up, ffn_dw=ffn_dw, ffn_dw_b=ffn_dw_b,
             ffn_down=ffn_down)
    n_bp, seq_p, d = x_prompt.shape
    n_bs, seq_s, _ = x_sample.shape
    depth = ada_w.shape[0]
    t_p = n_bp * seq_p
    t_s = n_bs * seq_s
    assert t_p % seq_s == 0 and seq_s % TM == 0 and TM % seq_p == 0 and seq_s % FFN_SUB == 0
    assert seq_p & (seq_p - 1) == 0 and seq_s & (seq_s - 1) == 0 and n_bs + 1 <= 8
    n_p = t_p // TM
    per_seq = seq_s // TM

    P = _prepare_params(W)
    c_cs, s_cs = _dft_tables(F_GROUP_W, 1.0 / math.sqrt(F_GROUP_W))
    P["dft_cs"] = jnp.asarray(np.concatenate([c_cs, s_cs], axis=1), dtype=BF16)
    dft_p = [jnp.asarray(t, dtype=BF16) for t in _dft_tables(seq_p, 1.0 / math.sqrt(seq_p))]
    dft_s = [jnp.asarray(t, dtype=BF16) for t in _dft_tables(seq_s, 1.0 / math.sqrt(seq_s))]
    rope_tab = jnp.asarray(_rope_table(seq_s, TM))

    cvec = jnp.concatenate([c_ctx[None, :], c, jnp.zeros((8 - 1 - n_bs, d), F32)], axis=0)
    mod = _mod_table(cvec, ada_w, ada_b).reshape(depth, 8, 1, N_MOD * d)

    kr_tile = jnp.pad(cache_krope, ((0, 0), (0, 0), (0, 0), (QK_NOPE, HEAD_TILE - QK_DIM)))
    kc, vc = _cache_kv(cache_ckv, kr_tile, P)

    x = (x_prompt.reshape(t_p, d), x_sample.reshape(t_s, d))
    ffn_args = dict(n_p=t_p // FFN_SUB, per_seq=seq_s // FFN_SUB, seq_p=seq_p, seq_s=seq_s)
    ckv_out, kr_out = [], []
    hbm = lambda *arrays: [_in_hbm(a) for a in arrays]
    for l in range(depth):
        outs = _in_proj(x, mod, l, P, rope_tab, n_p=n_p, per_seq=per_seq)
        if l == 0:
            x = outs[-1]
        fa, ug, q, k, v, ckv, kr = outs[:7]
        fa, ug, q, k, v = hbm(fa, ug, q, k, v)
        ckv_out.append(ckv[:t_p].reshape(n_bp, seq_p, KV_LORA))
        kr_out.append(kr[:t_p, QK_NOPE:QK_DIM].reshape(n_bp, seq_p, QK_ROPE))
        yf = hbm(_fourier_prompt(fa, *dft_p, t_p=t_p, seq=seq_p),
                 _fourier_sample(fa, *dft_s, t_p=t_p, n_b=n_bs, seq=seq_s))
        yc = hbm(_conv_module(ug, l, P, first=0, n_seq=n_bp, seq=seq_p),
                 _conv_module(ug, l, P, first=t_p // seq_s, n_seq=n_bs, seq=seq_s))
        ya = hbm(_attn_prompt(q, k, v, n_seq=n_bp, seq=seq_p),
                 _attn_sample(q, k, v, kc, vc, l, t_p=t_p, n_b=n_bs, seq=seq_s, tq=ATTN_TQ))
        x = _merge(x, mod, l, P, yf, yc, ya, n_p=n_p, per_seq=per_seq)
        if l + 1 < depth:
            x = _ffn(x, mod, l, P, **ffn_args)
    y_prompt = _ffn(x, mod, depth - 1, P, first=0, count=t_p // FFN_SUB, **ffn_args)
    y_sample = _ffn(x, mod, depth - 1, P, first=t_p // FFN_SUB, count=t_s // FFN_SUB, **ffn_args)
    return (y_prompt.reshape(n_bp, seq_p, d), y_sample.reshape(n_bs, seq_s, d),
            jnp.stack(ckv_out, axis=1), jnp.stack(kr_out, axis=1))
```

```python
import functools
import math

import numpy as np
import jax
import jax.numpy as jnp
from jax import lax
from jax.experimental import pallas as pl
from jax.experimental.pallas import tpu as pltpu

F32 = jnp.float32
BF16 = jnp.bfloat16

EPS = 1e-6
N_HEADS = 8
QK_NOPE = 64
QK_ROPE = 32
V_DIM = 64
QK_DIM = QK_NOPE + QK_ROPE
F_GROUPS = 4
F_GROUP_W = 128
F_W = F_GROUPS * F_GROUP_W
CONV_W = 512
CONV_K = 31
Q_LORA = 384
KV_LORA = 256
GRID_W = 64
ROPE_BASE = 10000.0
FFN_K = 3
N_MOD = 6

LANES = 128
SUBLANES = 8
V7X_VMEM_BYTES = 64 * 2**20
VMEM_RESERVE_BYTES = 6 * 2**20

HEAD_TILE = LANES
HW = N_HEADS * HEAD_TILE
PAIR_W = 2 * HEAD_TILE
CONV_PAD = 16

TM = 512
ATTN_TQ = 512
CONV_MAX_BLOCK_GROUPS = 16
FFN_SUB = 512
FFN_WIN = FFN_SUB + 2 * SUBLANES
FFN_CK = 256


def _in_hbm(a):
    if isinstance(a, jax.core.Tracer):
        return pltpu.with_memory_space_constraint(a, pltpu.HBM)
    return a


def _vmem_limit(block_bytes, scratch_bytes, temp_bytes):
    need = 2 * block_bytes + scratch_bytes + temp_bytes
    return int(min(need, V7X_VMEM_BYTES - VMEM_RESERVE_BYTES))


def _nbytes(shape, dtype):
    return int(np.prod(shape)) * jnp.dtype(dtype).itemsize


def _dot(a, b):
    return jnp.dot(a, b, preferred_element_type=F32)


def _dot_nt(a, b):
    return lax.dot_general(a, b, (((1,), (1,)), ((), ())), preferred_element_type=F32)


def _rms(x):
    return x * lax.rsqrt(jnp.mean(x * x, axis=-1, keepdims=True) + EPS)


def _sigmoid(x):
    return 1.0 / (1.0 + jnp.exp(-x))


def _silu(x):
    return x * _sigmoid(x)


def _mod_kernel(c_ref, w_ref, b_ref, o_ref):
    s = _silu(c_ref[...]).astype(BF16)
    o_ref[...] = _dot(s, w_ref[...].astype(BF16)) + b_ref[...]


def _mod_table(cvec, ada_w, ada_b):
    depth, d, n = ada_w.shape
    rows = cvec.shape[0]
    tn = n // 4
    blocks = _nbytes((d, tn), F32) + _nbytes((rows, tn), F32) * 2
    return pl.pallas_call(
        _mod_kernel,
        grid=(depth, n // tn),
        in_specs=[
            pl.BlockSpec((rows, d), lambda l, j: (0, 0)),
            pl.BlockSpec((None, d, tn), lambda l, j: (l, 0, j)),
            pl.BlockSpec((None, 1, tn), lambda l, j: (l, 0, j)),
        ],
        out_specs=pl.BlockSpec((None, rows, tn), lambda l, j: (l, 0, j)),
        out_shape=jax.ShapeDtypeStruct((depth, rows, n), F32),
        compiler_params=pltpu.CompilerParams(
            dimension_semantics=("arbitrary", "arbitrary"),
            vmem_limit_bytes=_vmem_limit(blocks, 0, _nbytes((d, tn), BF16) + 2**20)),
        name="adaln_table",
    )(cvec, ada_w, ada_b.reshape(depth, 1, n))


def _head_inv_rms(t):
    ss = jnp.sum(t * t, axis=-1, keepdims=True)
    return lax.rsqrt(ss * (1.0 / QK_DIM) + EPS)


def _keys_values(cb, kr, wkv_ref, k_ref, vt_ref, finish_key):
    lane = lax.broadcasted_iota(jnp.int32, (1, PAIR_W), 1)
    ones_row = jnp.logical_or(lane == V_DIM, lane == HEAD_TILE).astype(F32)
    for pair in range(N_HEADS // 2):
        cols = slice(pair * PAIR_W, (pair + 1) * PAIR_W)
        t2 = _dot(cb, wkv_ref[:, cols])
        for hh in range(2):
            sl = slice((2 * pair + hh) * HEAD_TILE, (2 * pair + hh + 1) * HEAD_TILE)
            k_ref[:, sl] = finish_key(t2[:, hh * HEAD_TILE:(hh + 1) * HEAD_TILE] + kr).astype(BF16)
        v2 = _dot(cb, wkv_ref[:, HW + pair * PAIR_W:HW + (pair + 1) * PAIR_W]) + ones_row
        vt_ref[cols, :] = v2.T.astype(BF16)


def _in_kernel(*refs, d, n_p, paired):
    if paired:
        (xp_ref, xs_ref, mod_ref, g1_ref, win_ref, wkr_ref, cs_ref, qg_ref, wq_ref, kvg_ref, wkv_ref,
         gq_ref, gk_ref, rope_ref,
         fa_ref, ug_ref, q_ref, k_ref, vt_ref, ckv_ref, kr_ref, x_out_ref) = refs
        x = jnp.where(pl.program_id(0) < n_p, xp_ref[...], xs_ref[...])
        x_out_ref[...] = x
    else:
        (x_ref, mod_ref, g1_ref, win_ref, wkr_ref, cs_ref, qg_ref, wq_ref, kvg_ref, wkv_ref,
         gq_ref, gk_ref, rope_ref,
         fa_ref, ug_ref, q_ref, k_ref, vt_ref, ckv_ref, kr_ref) = refs
        x = x_ref[...]
    sh1 = mod_ref[:, 0:d]
    sc1 = mod_ref[:, d:2 * d]
    h = _rms(x) * g1_ref[...] * (1.0 + sc1) + sh1
    hb = h.astype(BF16)
    c_conv = F_W
    c_q = c_conv + 2 * CONV_W
    c0 = c_q

    cos = rope_ref[:, 0:HEAD_TILE]
    sin = rope_ref[:, HEAD_TILE:2 * HEAD_TILE]

    qn = (_rms(_dot(hb, win_ref[:, c0:c0 + Q_LORA])) * qg_ref[...]).astype(BF16)
    c0 += Q_LORA
    scale = math.log2(math.e) / math.sqrt(QK_DIM)
    q_cos = cos * (gq_ref[0:1, :] * scale)
    q_sin = sin * (gq_ref[1:2, :] * scale)
    partners = _dot(qn, wq_ref[:, HW:HW + N_HEADS * QK_ROPE])
    per_tile = HEAD_TILE // QK_ROPE
    for pair in range(N_HEADS // 2):
        t2 = _dot(qn, wq_ref[:, pair * PAIR_W:(pair + 1) * PAIR_W])
        for hd in (2 * pair, 2 * pair + 1):
            sl = slice(hd * HEAD_TILE, (hd + 1) * HEAD_TILE)
            t = t2[:, (hd % 2) * HEAD_TILE:(hd % 2 + 1) * HEAD_TILE]
            tp = partners[:, (hd // per_tile) * HEAD_TILE:(hd // per_tile + 1) * HEAD_TILE]
            shift = (QK_NOPE - (hd % per_tile) * QK_ROPE) % HEAD_TILE
            if shift:
                tp = pltpu.roll(tp, shift, axis=1)
            q_ref[:, sl] = ((t * q_cos + tp * q_sin) * _head_inv_rms(t)).astype(BF16)

    ckv = _rms(_dot(hb, win_ref[:, c0:c0 + KV_LORA])) * kvg_ref[...]
    ckv_ref[...] = ckv
    kr = _dot(hb, wkr_ref[:, 0:HEAD_TILE])
    kr_ref[...] = kr
    k_cos = cos * gk_ref[0:1, :]
    kr_rot = _dot(hb, wkr_ref[:, HEAD_TILE:2 * HEAD_TILE]) * (sin * gk_ref[1:2, :])
    _keys_values(ckv.astype(BF16), kr, wkv_ref, k_ref, vt_ref,
                 lambda t: (t * k_cos + kr_rot) * _head_inv_rms(t))

    fb = _dot(hb, win_ref[:, 0:F_W]).astype(BF16)
    for g in range(F_GROUPS):
        a = _dot(fb[:, g * F_GROUP_W:(g + 1) * F_GROUP_W], cs_ref[...])
        fa_ref[:, g * F_GROUP_W:(g + 1) * F_GROUP_W] = a[:, :F_GROUP_W].astype(BF16)
        fa_ref[:, F_W + g * F_GROUP_W:F_W + (g + 1) * F_GROUP_W] = a[:, F_GROUP_W:].astype(BF16)

    ga = _dot(hb, win_ref[:, c_conv:c_conv + CONV_W])
    gg = _dot(hb, win_ref[:, c_conv + CONV_W:c_conv + 2 * CONV_W])
    ug_ref[...] = ga * _sigmoid(gg)


def _in_proj(x, mod, l, P, rope_tab, *, n_p, per_seq):
    paired = isinstance(x, tuple)
    d = (x[0] if paired else x).shape[1]
    t_rows = x[0].shape[0] + x[1].shape[0] if paired else x.shape[0]
    n_tiles = t_rows // TM
    n_s = n_tiles - n_p
    win_cols = P["w_in"].shape[-1]

    def mod_idx(i):
        return (l, jnp.where(i < n_p, 0, 1 + (i - n_p) // per_seq), 0, 0)

    def rope_idx(i):
        return (jnp.where(i < n_p, per_seq, (i - n_p) % per_seq), 0)

    row = lambda w: pl.BlockSpec((TM, w), lambda i: (i, 0))
    lay = lambda *s: pl.BlockSpec((None,) + s, lambda i: (l,) + (0,) * len(s))
    wq_cols = P["w_q"].shape[-1]
    out_shapes = [
        jax.ShapeDtypeStruct((t_rows, 2 * F_W), BF16),
        jax.ShapeDtypeStruct((t_rows, CONV_W), F32),
        jax.ShapeDtypeStruct((t_rows, HW), BF16),
        jax.ShapeDtypeStruct((t_rows, HW), BF16),
        jax.ShapeDtypeStruct((HW, t_rows), BF16),
        jax.ShapeDtypeStruct((t_rows, KV_LORA), F32),
        jax.ShapeDtypeStruct((t_rows, HEAD_TILE), F32),
    ]
    if paired:
        out_shapes.append(jax.ShapeDtypeStruct((t_rows, d), F32))
        x_args = list(x)
        x_specs = [pl.BlockSpec((TM, d), lambda i: (jnp.minimum(i, n_p - 1), 0)),
                   pl.BlockSpec((TM, d), lambda i: (jnp.clip(i - n_p, 0, n_s - 1), 0))]
    else:
        x_args = [x]
        x_specs = [row(d)]
    out_specs = [pl.BlockSpec((HW, TM), lambda i: (0, i)) if s.shape[0] == HW else row(s.shape[1])
                 for s in out_shapes]
    blocks = (len(x_args) * _nbytes((TM, d), F32) + _nbytes((d, win_cols + 2 * HEAD_TILE), BF16)
              + _nbytes((Q_LORA, wq_cols), BF16)
              + _nbytes((KV_LORA, 2 * HW), BF16) + _nbytes((TM, 2 * HEAD_TILE), F32)
              + sum(_nbytes(s.shape, s.dtype) // (t_rows // TM) for s in out_shapes))
    return pl.pallas_call(
        functools.partial(_in_kernel, d=d, n_p=n_p, paired=paired),
        grid=(n_tiles,),
        in_specs=x_specs + [
            pl.BlockSpec((None, None, 1, N_MOD * d), mod_idx),
            lay(1, d),
            lay(d, win_cols),
            lay(d, 2 * HEAD_TILE),
            pl.BlockSpec((F_GROUP_W, 2 * F_GROUP_W), lambda i: (0, 0)),
            lay(1, Q_LORA),
            lay(Q_LORA, wq_cols),
            lay(1, KV_LORA),
            lay(KV_LORA, 2 * HW),
            lay(2, HEAD_TILE),
            lay(2, HEAD_TILE),
            pl.BlockSpec((TM, 2 * HEAD_TILE), rope_idx),
        ],
        out_specs=out_specs,
        out_shape=out_shapes,
        compiler_params=pltpu.CompilerParams(
            dimension_semantics=("arbitrary",),
            vmem_limit_bytes=_vmem_limit(blocks, 0, 12 * _nbytes((TM, d), F32))),
        name="in_proj",
    )(*x_args, mod, P["norm1_g"], P["w_in"], P["w_kr"], P["dft_cs"], P["q_norm_g"], P["w_q"],
      P["kv_norm_g"], P["w_kv"], P["gq"], P["gk"], rope_tab)


def _cache_kv_kernel(ckv_ref, kr_ref, wkv_ref, gk_ref, k_ref, vt_ref):
    k_gain = gk_ref[0:1, :]
    _keys_values(ckv_ref[...].astype(BF16), kr_ref[...], wkv_ref, k_ref, vt_ref,
                 lambda t: t * k_gain * _head_inv_rms(t))


def _cache_kv(cache_ckv, cache_kr_tile, P):
    nb, depth, past, _ = cache_ckv.shape
    blk = lambda w: pl.BlockSpec((None, None, past, w), lambda b, l: (b, l, 0, 0))
    lay = lambda *s: pl.BlockSpec((None,) + s, lambda b, l: (l,) + (0,) * len(s))
    blocks = (_nbytes((past, KV_LORA + HEAD_TILE), F32) + _nbytes((KV_LORA, 2 * HW), BF16)
              + 2 * _nbytes((past, HW), BF16))
    return pl.pallas_call(
        _cache_kv_kernel,
        grid=(nb, depth),
        in_specs=[blk(KV_LORA), blk(HEAD_TILE), lay(KV_LORA, 2 * HW), lay(2, HEAD_TILE)],
        out_specs=[blk(HW), pl.BlockSpec((None, None, HW, past), lambda b, l: (b, l, 0, 0))],
        out_shape=[jax.ShapeDtypeStruct((nb, depth, past, HW), BF16),
                   jax.ShapeDtypeStruct((nb, depth, HW, past), BF16)],
        compiler_params=pltpu.CompilerParams(
            dimension_semantics=("arbitrary", "arbitrary"),
            vmem_limit_bytes=_vmem_limit(blocks, 0, 8 * _nbytes((past, HW), F32))),
        name="cache_kv",
    )(cache_ckv, cache_kr_tile, P["w_kv"], P["gk"])


def _fourier_kernel(fa_ref, c_ref, s_ref, o_ref, *, seq, n_seq):
    for s in range(n_seq):
        rows = slice(s * seq, (s + 1) * seq)
        y = _dot(c_ref[...], fa_ref[rows, 0:F_W]) - _dot(s_ref[...], fa_ref[rows, F_W:2 * F_W])
        o_ref[rows, :] = y.astype(BF16)


def _fourier_prompt(fa, dft_c, dft_s, *, t_p, seq):
    n_seq = TM // seq
    blocks = _nbytes((TM, 2 * F_W), BF16) + 2 * _nbytes((seq, seq), BF16) + _nbytes((TM, F_W), BF16)
    return pl.pallas_call(
        functools.partial(_fourier_kernel, seq=seq, n_seq=n_seq),
        grid=(t_p // TM,),
        in_specs=[
            pl.BlockSpec((TM, 2 * F_W), lambda i: (i, 0)),
            pl.BlockSpec((seq, seq), lambda i: (0, 0)),
            pl.BlockSpec((seq, seq), lambda i: (0, 0)),
        ],
        out_specs=pl.BlockSpec((TM, F_W), lambda i: (i, 0)),
        out_shape=jax.ShapeDtypeStruct((t_p, F_W), BF16),
        compiler_params=pltpu.CompilerParams(
            dimension_semantics=("arbitrary",),
            vmem_limit_bytes=_vmem_limit(blocks, 0, 4 * _nbytes((TM, F_W), F32))),
        name="fourier_prompt",
    )(fa, dft_c, dft_s)


def _fourier_sample_kernel(fa_ref, c_ref, s_ref, o_ref):
    y = _dot(c_ref[...], fa_ref[:, 0:F_W]) - _dot(s_ref[...], fa_ref[:, F_W:2 * F_W])
    o_ref[...] = y.astype(BF16)


def _fourier_sample(fa, dft_c, dft_s, *, t_p, n_b, seq):
    tr = TM
    per = seq // tr
    first = t_p // seq
    blocks = _nbytes((seq, 2 * F_W), BF16) + 2 * _nbytes((tr, seq), BF16) + _nbytes((tr, F_W), BF16)
    return pl.pallas_call(
        _fourier_sample_kernel,
        grid=(n_b, per),
        in_specs=[
            pl.BlockSpec((seq, 2 * F_W), lambda b, j: (first + b, 0)),
            pl.BlockSpec((tr, seq), lambda b, j: (j, 0)),
            pl.BlockSpec((tr, seq), lambda b, j: (j, 0)),
        ],
        out_specs=pl.BlockSpec((tr, F_W), lambda b, j: (b * per + j, 0)),
        out_shape=jax.ShapeDtypeStruct((n_b * seq, F_W), BF16),
        compiler_params=pltpu.CompilerParams(
            dimension_semantics=("arbitrary", "arbitrary"),
            vmem_limit_bytes=_vmem_limit(blocks, 0, 4 * _nbytes((tr, F_W), F32))),
        name="fourier_sample",
    )(fa, dft_c, dft_s)


def _conv_block_groups(n_grp):
    return max(g for g in range(1, CONV_MAX_BLOCK_GROUPS + 1) if n_grp % (2 * g) == 0)


def _conv_kernel(u_ref, dw_ref, dwb_ref, lng_ref, lnb_ref, o_ref, nat_ref, ext_ref, *, seq):
    half = CONV_K // 2
    win = seq + 2 * CONV_PAD
    n_grp = win // SUBLANES
    n_slab = CONV_W // LANES
    lead = half * SUBLANES
    zeros = jnp.zeros((CONV_PAD, LANES), F32)
    for k in range(n_slab):
        nat_ref[k, 0:CONV_PAD, :] = zeros
        nat_ref[k, CONV_PAD + seq:win, :] = zeros
        nat_ref[k, CONV_PAD:CONV_PAD + seq, :] = u_ref[:, k * LANES:(k + 1) * LANES]
    for k in range(n_slab):
        for v in range(n_grp):
            g = nat_ref[k, pl.ds(v, SUBLANES, stride=n_grp), :]
            ext_ref[k, lead + v * SUBLANES:lead + (v + 1) * SUBLANES, :] = g
            if v >= n_grp - half:
                j = v - (n_grp - half)
                ext_ref[k, j * SUBLANES:(j + 1) * SUBLANES, :] = pltpu.roll(g, 1, axis=0)
            if v < half:
                j = half + n_grp + v
                ext_ref[k, j * SUBLANES:(j + 1) * SUBLANES, :] = pltpu.roll(g, SUBLANES - 1, axis=0)

    blk_grp = _conv_block_groups(n_grp)
    blk = blk_grp * SUBLANES
    n_blk = n_grp // blk_grp

    for k in range(n_slab):
        lanes = slice(k * LANES, (k + 1) * LANES)

        def conv_body(b, carry, k=k, lanes=lanes):
            r0 = pl.multiple_of(b * blk, SUBLANES)
            acc = jnp.zeros((blk, LANES), F32)
            for t in range(CONV_K):
                acc = acc + dw_ref[t:t + 1, lanes] * ext_ref[k, pl.ds(r0 + t * SUBLANES, blk), :]
            ext_ref[k, pl.ds(r0, blk), :] = acc + dwb_ref[:, lanes]
            return carry

        lax.fori_loop(0, n_blk, conv_body, 0)

    def norm_block(r0):
        acc = jnp.concatenate([ext_ref[k, pl.ds(r0, blk), :] for k in range(n_slab)], axis=1)
        mu = jnp.mean(acc, axis=-1, keepdims=True)
        cen = acc - mu
        var = jnp.mean(cen * cen, axis=-1, keepdims=True)
        y = _silu(cen * lax.rsqrt(var + EPS) * lng_ref[...] + lnb_ref[...])
        for k in range(n_slab):
            ext_ref[k, pl.ds(r0, blk), :] = y[:, k * LANES:(k + 1) * LANES]

    per_iter = max(u for u in (1, 2, 4) if n_blk % u == 0)

    def norm_body(b, carry):
        for u in range(per_iter):
            norm_block(pl.multiple_of((b + u * (n_blk // per_iter)) * blk, SUBLANES))
        return carry

    lax.fori_loop(0, n_blk // per_iter, norm_body, 0)

    for k in range(n_slab):
        for v in range(n_grp):
            nat_ref[k, pl.ds(v, SUBLANES, stride=n_grp), :] = ext_ref[k, v * SUBLANES:(v + 1) * SUBLANES, :]
        o_ref[:, k * LANES:(k + 1) * LANES] = nat_ref[k, CONV_PAD:CONV_PAD + seq, :].astype(BF16)


def _conv_module(ug, l, P, *, first, n_seq, seq):
    lay = lambda *s: pl.BlockSpec((None,) + s, lambda i: (l,) + (0,) * len(s))
    blocks = _nbytes((seq, CONV_W), F32) + _nbytes((seq, CONV_W), BF16) + _nbytes((CONV_K + 4, CONV_W), F32)
    win = seq + 2 * CONV_PAD
    ext = win + 2 * (CONV_K // 2) * SUBLANES
    scratch = _nbytes((win + ext, CONV_W), F32)
    return pl.pallas_call(
        functools.partial(_conv_kernel, seq=seq),
        grid=(n_seq,),
        in_specs=[
            pl.BlockSpec((seq, CONV_W), lambda i: (first + i, 0)),
            lay(CONV_K, CONV_W), lay(1, CONV_W), lay(1, CONV_W), lay(1, CONV_W),
        ],
        out_specs=pl.BlockSpec((seq, CONV_W), lambda i: (i, 0)),
        out_shape=jax.ShapeDtypeStruct((n_seq * seq, CONV_W), BF16),
        scratch_shapes=[pltpu.VMEM((CONV_W // LANES, win, LANES), F32),
                        pltpu.VMEM((CONV_W // LANES, ext, LANES), F32)],
        compiler_params=pltpu.CompilerParams(
            dimension_semantics=("arbitrary",),
            vmem_limit_bytes=_vmem_limit(blocks, scratch, 4 * 2**20)),
        name=f"conv_seq{seq}",
    )(ug, P["conv_dw"], P["conv_dw_b"], P["conv_ln_g"], P["conv_ln_b"])


def _attn_kernel(*refs, with_cache):
    if with_cache:
        q_ref, k_ref, vt_ref, kc_ref, vct_ref, o_ref = refs
    else:
        q_ref, k_ref, vt_ref, o_ref = refs
    tq = q_ref.shape[0]
    row = lax.broadcasted_iota(jnp.int32, (HEAD_TILE, tq), 0)

    def scores(hd):
        sl = slice(hd * HEAD_TILE, (hd + 1) * HEAD_TILE)
        qh = q_ref[:, sl]
        return _dot_nt(k_ref[:, sl], qh), (_dot_nt(kc_ref[:, sl], qh) if with_cache else None)

    nxt = scores(0)
    even = None
    for hd in range(N_HEADS):
        sl = slice(hd * HEAD_TILE, (hd + 1) * HEAD_TILE)
        s_m, s_c = nxt
        if hd + 1 < N_HEADS:
            nxt = scores(hd + 1)
        m = jnp.max(s_m, axis=0, keepdims=True)
        if with_cache:
            m = jnp.maximum(m, jnp.max(s_c, axis=0, keepdims=True))
        out_t = _dot(vt_ref[sl, :], jnp.exp2(s_m - m).astype(BF16))
        if with_cache:
            out_t = out_t + _dot(vct_ref[sl, :], jnp.exp2(s_c - m).astype(BF16))
        den_row = V_DIM if hd % 2 == 0 else 0
        out_t = out_t * (1.0 / out_t[den_row:den_row + 1, :])
        if hd % 2 == 0:
            even = out_t
        else:
            pair_t = jnp.where(row < V_DIM, even, out_t)
            o_ref[:, (hd // 2) * HEAD_TILE:(hd // 2 + 1) * HEAD_TILE] = pair_t.T.astype(BF16)


def _attn_prompt(q, k, vt, *, n_seq, seq):
    blk = pl.BlockSpec((seq, HW), lambda i: (i, 0))
    blocks = 3 * _nbytes((seq, HW), BF16) + _nbytes((seq, N_HEADS * V_DIM), BF16)
    return pl.pallas_call(
        functools.partial(_attn_kernel, with_cache=False),
        grid=(n_seq,),
        in_specs=[blk, blk, pl.BlockSpec((HW, seq), lambda i: (0, i))],
        out_specs=pl.BlockSpec((seq, N_HEADS * V_DIM), lambda i: (i, 0)),
        out_shape=jax.ShapeDtypeStruct((n_seq * seq, N_HEADS * V_DIM), BF16),
        compiler_params=pltpu.CompilerParams(
            dimension_semantics=("arbitrary",),
            vmem_limit_bytes=_vmem_limit(blocks, 0, 8 * _nbytes((seq, seq), F32) + 2**21)),
        name="attn_prompt",
    )(q, k, vt)


def _attn_sample(q, k, vt, kc, vct, l, *, t_p, n_b, seq, tq):
    per = seq // tq
    first_q = t_p // tq
    first_s = t_p // seq
    past = kc.shape[2]
    once = pl.Buffered(1)
    k_blk = pl.BlockSpec((seq, HW), lambda b, j: (first_s + b, 0), pipeline_mode=once)
    vt_blk = pl.BlockSpec((HW, seq), lambda b, j: (0, first_s + b), pipeline_mode=once)
    kc_blk = pl.BlockSpec((None, None, past, HW), lambda b, j: (b, l, 0, 0), pipeline_mode=once)
    vct_blk = pl.BlockSpec((None, None, HW, past), lambda b, j: (b, l, 0, 0), pipeline_mode=once)
    blocks = _nbytes((tq, HW), BF16) + _nbytes((tq, N_HEADS * V_DIM), BF16)
    resident = 2 * _nbytes((seq + past, HW), BF16)
    return pl.pallas_call(
        functools.partial(_attn_kernel, with_cache=True),
        grid=(n_b, per),
        in_specs=[pl.BlockSpec((tq, HW), lambda b, j: (first_q + b * per + j, 0)),
                  k_blk, vt_blk, kc_blk, vct_blk],
        out_specs=pl.BlockSpec((tq, N_HEADS * V_DIM), lambda b, j: (b * per + j, 0)),
        out_shape=jax.ShapeDtypeStruct((n_b * seq, N_HEADS * V_DIM), BF16),
        compiler_params=pltpu.CompilerParams(
            dimension_semantics=("arbitrary", "arbitrary"),
            vmem_limit_bytes=_vmem_limit(blocks, resident, 8 * _nbytes((tq, seq + past), F32))),
        name="attn_sample",
    )(q, k, vt, kc, vct)


def _merge_kernel(x_ref, mod_ref, g1_ref, fp_ref, fs_ref, cp_ref, cs_ref, ap_ref, as_ref,
                  wg_ref, bg_ref, wf_ref, wc_ref, wa_ref, wo_ref, o_ref, *, d, n_p):
    is_prompt = pl.program_id(0) < n_p
    x = x_ref[...]
    sh1 = mod_ref[:, 0:d]
    sc1 = mod_ref[:, d:2 * d]
    g1 = mod_ref[:, 2 * d:3 * d]
    hb = (_rms(x) * g1_ref[...] * (1.0 + sc1) + sh1).astype(BF16)
    merged = None
    branches = ((fp_ref, fs_ref, wf_ref), (cp_ref, cs_ref, wc_ref), (ap_ref, as_ref, wa_ref))
    for b, (yp_ref, ys_ref, w_ref) in enumerate(branches):
        y_in = jnp.where(is_prompt, yp_ref[...], ys_ref[...])
        y = _dot(y_in, w_ref[...])
        gate = _sigmoid(_dot(hb, wg_ref[:, b * d:(b + 1) * d]) + bg_ref[:, b * d:(b + 1) * d])
        merged = gate * y if merged is None else merged + gate * y
    o_ref[...] = x + g1 * _dot(merged.astype(BF16), wo_ref[...])


def _merge(x, mod, l, P, yf, yc, ya, *, n_p, per_seq):
    t_rows, d = x.shape
    n_tiles = t_rows // TM
    n_s = n_tiles - n_p

    def mod_idx(i):
        return (l, jnp.where(i < n_p, 0, 1 + (i - n_p) // per_seq), 0, 0)

    row = pl.BlockSpec((TM, d), lambda i: (i, 0))
    lay = lambda *s: pl.BlockSpec((None,) + s, lambda i: (l,) + (0,) * len(s))
    p_blk = pl.BlockSpec((TM, F_W), lambda i: (jnp.minimum(i, n_p - 1), 0))
    s_blk = pl.BlockSpec((TM, F_W), lambda i: (jnp.clip(i - n_p, 0, n_s - 1), 0))
    blocks = (2 * _nbytes((TM, d), F32) + 6 * _nbytes((TM, F_W), BF16) + _nbytes((d, 3 * d), BF16)
              + 3 * _nbytes((F_W, d), BF16) + _nbytes((d, d), BF16))
    return pl.pallas_call(
        functools.partial(_merge_kernel, d=d, n_p=n_p),
        grid=(n_tiles,),
        in_specs=[row, pl.BlockSpec((None, None, 1, N_MOD * d), mod_idx), lay(1, d),
                  p_blk, s_blk, p_blk, s_blk, p_blk, s_blk,
                  lay(d, 3 * d), lay(1, 3 * d), lay(F_W, d), lay(CONV_W, d), lay(N_HEADS * V_DIM, d),
                  lay(d, d)],
        out_specs=row,
        out_shape=jax.ShapeDtypeStruct((t_rows, d), F32),
        compiler_params=pltpu.CompilerParams(
            dimension_semantics=("arbitrary",),
            allow_input_fusion=[k in (9, 11, 12, 13, 14) for k in range(15)],
            vmem_limit_bytes=_vmem_limit(blocks, 0, 8 * _nbytes((TM, d), F32))),
        name="merge",
    )(x, mod, P["norm1_g"], yf[0], yf[1], yc[0], yc[1], ya[0], ya[1],
      P["w_gate"], P["b_gate"], P["w_fourier"], P["w_conv_out"], P["w_mla_out"], P["w_out"])


def _ffn_kernel(x_ref, xp_ref, xn_ref, mod_ref, g2_ref, up_ref, dw_ref, b_ref, dn_ref,
                o_ref, nat_ref, h_ref, act_ref, *, d, d_ff, n_p, seq_p, seq_s, first):
    i = pl.program_id(0) + first
    sh2 = mod_ref[:, 3 * d:4 * d]
    sc2 = mod_ref[:, 4 * d:5 * d]
    g2 = mod_ref[:, 5 * d:6 * d]

    def norm_mod(x):
        return _rms(x) * g2_ref[...] * (1.0 + sc2) + sh2

    is_prompt = i < n_p
    period_mask = jnp.where(is_prompt, seq_p - 1, seq_s - 1)
    has_prev = ((i * FFN_SUB) & period_mask) != 0
    has_next = (((i + 1) * FFN_SUB) & period_mask) != 0
    h_prev = jnp.where(has_prev, norm_mod(xp_ref[...]), 0.0)
    h_next = jnp.where(has_next, norm_mod(xn_ref[...]), 0.0)
    h_main = norm_mod(x_ref[...])

    n_grp = FFN_WIN // SUBLANES
    for k in range(d // LANES):
        lanes = slice(k * LANES, (k + 1) * LANES)
        nat_ref[k, 0:SUBLANES, :] = h_prev[:, lanes]
        nat_ref[k, SUBLANES:SUBLANES + FFN_SUB, :] = h_main[:, lanes]
        nat_ref[k, SUBLANES + FFN_SUB:FFN_WIN, :] = h_next[:, lanes]
    for k in range(d // LANES):
        groups = [nat_ref[k, pl.ds(v, SUBLANES, stride=n_grp), :] for v in range(n_grp)]
        h_ref[:, k * LANES:(k + 1) * LANES] = jnp.concatenate(groups, axis=0).astype(BF16)

    sub = lax.broadcasted_iota(jnp.int32, (SUBLANES, 1), 0)
    no_prev = jnp.zeros((SUBLANES, 1), jnp.bool_)
    no_next = jnp.zeros((SUBLANES, 1), jnp.bool_)
    for edge in range(seq_p, FFN_SUB, seq_p):
        assert (edge + SUBLANES) % n_grp == 0
        no_prev = jnp.logical_or(no_prev, sub == (edge + SUBLANES) // n_grp)
        no_next = jnp.logical_or(no_next, sub == (edge + SUBLANES - 1) // n_grp)
    no_prev = jnp.logical_and(is_prompt, no_prev)
    no_next = jnp.logical_and(is_prompt, no_next)

    def conv(u, c0):
        last = u[FFN_WIN - SUBLANES:FFN_WIN, :]
        first = u[0:SUBLANES, :]
        wrap_prev = jnp.where(no_prev, 0.0, pltpu.roll(last, 1, axis=0))
        wrap_next = jnp.where(no_next, 0.0, pltpu.roll(first, SUBLANES - 1, axis=0))
        prev = jnp.concatenate([wrap_prev, u[0:FFN_WIN - SUBLANES, :]], axis=0)
        nxt = jnp.concatenate([u[SUBLANES:FFN_WIN, :], wrap_next], axis=0)
        w = dw_ref[:, c0:c0 + FFN_CK]
        return w[0:1, :] * prev + w[1:2, :] * u + w[2:3, :] * nxt + b_ref[:, c0:c0 + FFN_CK]

    hx = h_ref[...]

    def up_proj(c):
        return (_dot(hx, up_ref[:, c * FFN_CK:(c + 1) * FFN_CK]),
                _dot(hx, up_ref[:, d_ff + c * FFN_CK:d_ff + (c + 1) * FFN_CK]))

    n_ck = d_ff // FFN_CK
    half_ck = (n_ck + 1) // 2
    part = None
    nxt = up_proj(0)
    for c in range(n_ck):
        u_a, u_b = nxt
        if c + 1 < n_ck:
            nxt = up_proj(c + 1)
        if c == half_ck:
            part = _dot(act_ref[:, 0:half_ck * FFN_CK], dn_ref[0:half_ck * FFN_CK, :])
        act = _silu(conv(u_a, c * FFN_CK)) * conv(u_b, d_ff + c * FFN_CK)
        act_ref[:, c * FFN_CK:(c + 1) * FFN_CK] = act.astype(BF16)
    y = part + _dot(act_ref[:, half_ck * FFN_CK:d_ff], dn_ref[half_ck * FFN_CK:d_ff, :])
    for k in range(d // LANES):
        lanes = slice(k * LANES, (k + 1) * LANES)
        for v in range(n_grp):
            nat_ref[k, pl.ds(v, SUBLANES, stride=n_grp), :] = y[v * SUBLANES:(v + 1) * SUBLANES, lanes]
        o_ref[:, lanes] = x_ref[:, lanes] + g2[:, lanes] * nat_ref[k, SUBLANES:SUBLANES + FFN_SUB, :]


def _ffn(x, mod, l, P, *, n_p, per_seq, seq_p, seq_s, first=0, count=None):
    t_rows, d = x.shape
    d_ff = P["ffn_down"].shape[1]
    n_tiles = t_rows // FFN_SUB if count is None else count
    halo_per_tile = FFN_SUB // SUBLANES
    n_halo = t_rows // SUBLANES

    def mod_idx(j):
        i = j + first
        return (l, jnp.where(i < n_p, 0, 1 + (i - n_p) // per_seq), 0, 0)

    row = pl.BlockSpec((FFN_SUB, d), lambda j: (j + first, 0))
    once = pl.Buffered(1)
    lay = lambda *s: pl.BlockSpec((None,) + s, lambda i: (l,) + (0,) * len(s))
    res = lambda *s: pl.BlockSpec((None,) + s, lambda i: (l,) + (0,) * len(s), pipeline_mode=once)
    weights = _nbytes((d, 2 * d_ff), BF16) + _nbytes((d_ff, d), BF16)
    blocks = 2 * _nbytes((FFN_SUB + 2 * SUBLANES, d), F32) + 4 * _nbytes((8, 2 * d_ff), F32)
    scratch = (_nbytes((FFN_WIN, d), F32) + _nbytes((FFN_WIN, d), BF16) + _nbytes((FFN_WIN, d_ff), BF16))
    temps = 4 * _nbytes((FFN_WIN, d), F32) + 8 * _nbytes((FFN_WIN, FFN_CK), F32)
    return pl.pallas_call(
        functools.partial(_ffn_kernel, d=d, d_ff=d_ff, n_p=n_p, seq_p=seq_p, seq_s=seq_s, first=first),
        grid=(n_tiles,),
        in_specs=[
            row,
            pl.BlockSpec((SUBLANES, d), lambda j: (jnp.maximum((j + first) * halo_per_tile - 1, 0), 0)),
            pl.BlockSpec((SUBLANES, d),
                         lambda j: (jnp.minimum((j + first + 1) * halo_per_tile, n_halo - 1), 0)),
            pl.BlockSpec((None, None, 1, N_MOD * d), mod_idx),
            lay(1, d),
            res(d, 2 * d_ff),
            lay(FFN_K, 2 * d_ff),
            lay(1, 2 * d_ff),
            res(d_ff, d),
        ],
        out_specs=pl.BlockSpec((FFN_SUB, d), lambda j: (j, 0)),
        out_shape=jax.ShapeDtypeStruct((n_tiles * FFN_SUB, d), F32),
        scratch_shapes=[pltpu.VMEM((d // LANES, FFN_WIN, LANES), F32), pltpu.VMEM((FFN_WIN, d), BF16),
                        pltpu.VMEM((FFN_WIN, d_ff), BF16)],
        compiler_params=pltpu.CompilerParams(
            dimension_semantics=("arbitrary",),
            allow_input_fusion=[k in (5, 8) for k in range(9)],
            vmem_limit_bytes=_vmem_limit(blocks, scratch + weights, temps)),
        name="conv_ffn",
    )(x, x, x, mod, P["norm2_g"], P["ffn_up"], P["ffn_dw"], P["ffn_dw_b"], P["ffn_down"])


def _dft_tables(n, scale):
    j = np.arange(n, dtype=np.int64)
    ang = ((j[:, None] * j[None, :]) % n).astype(np.float64) * (2.0 * math.pi / n)
    return (np.cos(ang) * scale).astype(np.float32), (np.sin(ang) * scale).astype(np.float32)


def _rope_partner():
    lane = np.arange(HEAD_TILE)
    dim = lane - QK_NOPE
    in_rope = (dim >= 0) & (dim < QK_ROPE)
    quarter = QK_ROPE // 4
    first_half = (dim % (2 * quarter)) // quarter == 0
    perm = np.where(in_rope, np.where(first_half, lane + quarter, lane - quarter), lane)
    return perm, in_rope.astype(np.float32)


def _rope_table(seq, ident_rows):
    rows = seq // GRID_W
    row = np.repeat(np.arange(rows), GRID_W).astype(np.float32)
    col = np.tile(np.arange(GRID_W), rows).astype(np.float32)
    half = QK_ROPE // 2
    inv = (np.float32(ROPE_BASE) ** (-np.arange(0, half, 2, dtype=np.float32) / half)).astype(np.float32)
    ang = np.stack([row[:, None] * inv, col[:, None] * inv], axis=1)
    cos, sin = np.cos(ang), np.sin(ang)
    cos_r = np.stack([cos, cos], axis=2).reshape(seq, QK_ROPE)
    sin_r = np.stack([-sin, sin], axis=2).reshape(seq, QK_ROPE)

    def tile(mid, fill):
        left = np.full((seq, QK_NOPE), fill, np.float32)
        right = np.full((seq, HEAD_TILE - QK_DIM), fill, np.float32)
        return np.concatenate([left, mid, right], axis=1)

    tab = np.concatenate([tile(cos_r, 1.0), tile(sin_r, 0.0)], axis=1)
    ident = np.concatenate([np.ones((ident_rows, HEAD_TILE), np.float32),
                            np.zeros((ident_rows, HEAD_TILE), np.float32)], axis=1)
    return np.concatenate([tab, ident], axis=0).astype(np.float32)


def _prepare_params(W):
    depth, d, _ = W["w_in"].shape
    P = {}
    perm, rope_mask = _rope_partner()
    split = F_W + 2 * CONV_W + Q_LORA + KV_LORA
    w_in = W["w_in"]
    kr_tile = jnp.concatenate([jnp.zeros((depth, d, QK_NOPE), F32), w_in[:, :, split:],
                               jnp.zeros((depth, d, HEAD_TILE - QK_DIM), F32)], axis=2)
    P["w_in"] = w_in.astype(BF16)
    P["w_kr"] = jnp.concatenate([kr_tile, kr_tile[:, :, perm] * rope_mask], axis=2).astype(BF16)
    wq = W["w_q_up"].reshape(depth, Q_LORA, N_HEADS, QK_DIM)
    wq = jnp.pad(wq, ((0, 0), (0, 0), (0, 0), (0, HEAD_TILE - QK_DIM)))
    wq_rot = wq[..., perm][..., QK_NOPE:QK_DIM]
    P["w_q"] = jnp.concatenate([wq.reshape(depth, Q_LORA, HW),
                                wq_rot.reshape(depth, Q_LORA, N_HEADS * QK_ROPE)], axis=2).astype(BF16)
    wkv = W["w_kv_up"].reshape(depth, KV_LORA, N_HEADS, QK_NOPE + V_DIM)
    zk = jnp.zeros((depth, KV_LORA, N_HEADS, HEAD_TILE - QK_NOPE), F32)
    wk = jnp.concatenate([wkv[..., :QK_NOPE], zk], axis=-1)
    zv = jnp.zeros((depth, KV_LORA, N_HEADS, HEAD_TILE - V_DIM), F32)
    wv_even = jnp.concatenate([wkv[..., QK_NOPE:], zv], axis=-1)
    wv_odd = jnp.concatenate([zv, wkv[..., QK_NOPE:]], axis=-1)
    odd = (jnp.arange(N_HEADS) % 2 == 1)[None, None, :, None]
    wv = jnp.where(odd, wv_odd, wv_even)
    P["w_kv"] = jnp.concatenate([wk.reshape(depth, KV_LORA, HW), wv.reshape(depth, KV_LORA, HW)],
                                axis=2).astype(BF16)
    def head_gains(g):
        g = jnp.pad(g, ((0, 0), (0, HEAD_TILE - QK_DIM)))
        return jnp.stack([g, g[:, perm] * rope_mask], axis=1)

    P["gq"] = head_gains(W["qk_q_g"])
    P["gk"] = head_gains(W["qk_k_g"])
    vec = lambda a: a.reshape(depth, 1, a.shape[-1])
    for name in ("norm1_g", "norm2_g", "q_norm_g", "kv_norm_g", "b_gate", "conv_dw_b", "conv_ln_g",
                 "conv_ln_b", "ffn_dw_b"):
        P[name] = vec(W[name])
    P["conv_dw"] = W["conv_dw"]
    P["ffn_dw"] = W["ffn_dw"]
    for name in ("w_gate", "w_fourier", "w_conv_out", "w_mla_out", "w_out", "ffn_up", "ffn_down"):
        P[name] = W[name].astype(BF16)
    return P


def kernel(x_prompt, x_sample, cache_ckv, cache_krope, c, c_ctx, ada_w, ada_b, norm1_g, norm2_g, w_in, w_gate, b_gate, w_fourier, conv_dw, conv_dw_b, conv_ln_g, conv_ln_b, w_conv_out, q_norm_g, w_q_up, kv_norm_g, w_kv_up, qk_q_g, qk_k_g, w_mla_out, w_out, ffn_up, ffn_dw, ffn_dw_b, ffn_down):
    W = dict(norm1_g=norm1_g, norm2_g=norm2_g, w_in=w_in, w_gate=w_gate, b_gate=b_gate,
             w_fourier=w_fourier, conv_dw=conv_dw, conv_dw_b=conv_dw_b, conv_ln_g=conv_ln_g,
             conv_ln_b=conv_ln_b, w_conv_out=w_conv_out, q_norm_g=q_norm_g, w_q_up=w_q_up,
             kv_norm_g=kv_norm_g, w_kv_up=w_kv_up, qk_q_g=qk_q_g, qk_k_g=qk_k_g,
             w_mla_out=w_mla_out, w_out=w_out, ffn_up=ffn_up, ffn_dw=ffn_dw, ffn_dw_b=ffn_dw_b,
             ffn_down=ffn_down)
    n_bp, seq_p, d = x_prompt.shape
    n_bs, seq_s, _ = x_sample.shape
    depth = ada_w.shape[0]
    t_p = n_bp * seq_p
    t_s = n_bs * seq_s
    assert t_p % seq_s == 0 and seq_s % TM == 0 and TM % seq_p == 0 and seq_s % FFN_SUB == 0
    assert seq_p & (seq_p - 1) == 0 and seq_s & (seq_s - 1) == 0 and n_bs + 1 <= 8
    n_p = t_p // TM
    per_seq = seq_s // TM

    P = _prepare_params(W)
    c_cs, s_cs = _dft_tables(F_GROUP_W, 1.0 / math.sqrt(F_GROUP_W))
    P["dft_cs"] = jnp.asarray(np.concatenate([c_cs, s_cs], axis=1), dtype=BF16)
    dft_p = [jnp.asarray(t, dtype=BF16) for t in _dft_tables(seq_p, 1.0 / math.sqrt(seq_p))]
    dft_s = [jnp.asarray(t, dtype=BF16) for t in _dft_tables(seq_s, 1.0 / math.sqrt(seq_s))]
    rope_tab = jnp.asarray(_rope_table(seq_s, TM))

    cvec = jnp.concatenate([c_ctx[None, :], c, jnp.zeros((8 - 1 - n_bs, d), F32)], axis=0)
    mod = _mod_table(cvec, ada_w, ada_b).reshape(depth, 8, 1, N_MOD * d)

    kr_tile = jnp.pad(cache_krope, ((0, 0), (0, 0), (0, 0), (QK_NOPE, HEAD_TILE - QK_DIM)))
    kc, vc = _cache_kv(cache_ckv, kr_tile, P)

    x = (x_prompt.reshape(t_p, d), x_sample.reshape(t_s, d))
    ffn_args = dict(n_p=t_p // FFN_SUB, per_seq=seq_s // FFN_SUB, seq_p=seq_p, seq_s=seq_s)
    ckv_out, kr_out = [], []
    hbm = lambda *arrays: [_in_hbm(a) for a in arrays]
    for l in range(depth):
        outs = _in_proj(x, mod, l, P, rope_tab, n_p=n_p, per_seq=per_seq)
        if l == 0:
            x = outs[-1]
        fa, ug, q, k, v, ckv, kr = outs[:7]
        fa, ug, q, k, v = hbm(fa, ug, q, k, v)
        ckv_out.append(ckv[:t_p].reshape(n_bp, seq_p, KV_LORA))
        kr_out.append(kr[:t_p, QK_NOPE:QK_DIM].reshape(n_bp, seq_p, QK_ROPE))
        yf = hbm(_fourier_prompt(fa, *dft_p, t_p=t_p, seq=seq_p),
                 _fourier_sample(fa, *dft_s, t_p=t_p, n_b=n_bs, seq=seq_s))
        yc = hbm(_conv_module(ug, l, P, first=0, n_seq=n_bp, seq=seq_p),
                 _conv_module(ug, l, P, first=t_p // seq_s, n_seq=n_bs, seq=seq_s))
        ya = hbm(_attn_prompt(q, k, v, n_seq=n_bp, seq=seq_p),
                 _attn_sample(q, k, v, kc, vc, l, t_p=t_p, n_b=n_bs, seq=seq_s, tq=ATTN_TQ))
        x = _merge(x, mod, l, P, yf, yc, ya, n_p=n_p, per_seq=per_seq)
        if l + 1 < depth:
            x = _ffn(x, mod, l, P, **ffn_args)
    y_prompt = _ffn(x, mod, depth - 1, P, first=0, count=t_p // FFN_SUB, **ffn_args)
    y_sample = _ffn(x, mod, depth - 1, P, first=t_p // FFN_SUB, count=t_s // FFN_SUB, **ffn_args)
    return (y_prompt.reshape(n_bp, seq_p, d), y_sample.reshape(n_bs, seq_s, d),
            jnp.stack(ckv_out, axis=1), jnp.stack(kr_out, axis=1))
```
